```python
import jax, jax.numpy as jnp
from jax import lax
import numpy as np

D_MODEL = 1024
BATCH = 8
SEQ = 2048
DEPTH = 2

GRID_W = 64
CTX_LEN = 256
MIX_W = D_MODEL
N_MIXERS = 4
GROUP_W = MIX_W // N_MIXERS
HEAD_DIM = 64
N_HEADS = GROUP_W // HEAD_DIM
GLA_DK = HEAD_DIM // 2
GLA_QK_W = N_HEADS * GLA_DK
GLA_RANK = 16
GLA_NORMALIZER = 16.0
GLA_CHUNK = 64
MLSTM_CHUNK = 64
CONV_K = 3
SGU_CHUNK = 128
D_FF = 256 * ((8 * D_MODEL // 3 + 255) // 256)
N_EXPERTS = 8
TOP_K = 2
D_FF_EXPERT = 7 * D_MODEL // 2
N_DENSE = (DEPTH + 1) // 2
N_MOE = DEPTH // 2
EPS = 1e-6
IN_SPLITS = (GLA_QK_W, GLA_QK_W, GROUP_W, GROUP_W, 2 * GLA_RANK,
             GROUP_W, GROUP_W, GROUP_W, GROUP_W, 4 * N_HEADS,
             GROUP_W, GROUP_W,
             GROUP_W)
REC_W = 2 * GLA_QK_W + 2 * GROUP_W + 2 * GLA_RANK + 4 * GROUP_W + 4 * N_HEADS
D_IN = REC_W + 3 * GROUP_W
F32 = jnp.float32

kernel_name = 'hybrid_parallel_mixer_dit_block'


def rmsnorm(x, g):
    x32 = x.astype(F32)
    y = x32 * lax.rsqrt(jnp.mean(x32 * x32, axis=-1, keepdims=True) + EPS)
    return (y * g.astype(F32)).astype(x.dtype)


def modulate(xn, shift, scale):
    return xn * (1 + scale) + shift


def heads(x, dh):
    b, t, _ = x.shape
    return x.reshape(b, t, -1, dh).transpose(0, 2, 1, 3)


def merge_heads(x):
    b, h, t, d = x.shape
    return x.transpose(0, 2, 1, 3).reshape(b, t, h * d)


def flip_t(x):
    return jnp.flip(x, axis=2)


def split_cols(p):
    parts, o = [], 0
    for s in IN_SPLITS:
        if o >= p.shape[-1]:
            break
        parts.append(p[..., o:o + s])
        o += s
    return parts


def conv2d_grid(x, w, rows, cols):
    b, t, ch = x.shape
    y = lax.conv_general_dilated(x.reshape(b, rows, cols, ch), w[:, :, None, :].astype(x.dtype),
                                 window_strides=(1, 1), padding='SAME',
                                 dimension_numbers=('NHWC', 'HWIO', 'NHWC'), feature_group_count=ch)
    return y.reshape(b, t, ch)


def gla_scan(q, k, v, log_a, s0, with_output):
    b, h, t, dk = q.shape
    dv = v.shape[-1]
    n = t // GLA_CHUNK
    q, k, log_a = (a.reshape(b, h, n, GLA_CHUNK, dk) for a in (q, k, log_a))
    v = v.reshape(b, h, n, GLA_CHUNK, dv)
    cum = jnp.cumsum(log_a, axis=3)
    cum_last = cum[:, :, :, -1:, :]
    upd = jnp.einsum('bhnld,bhnlv->bhndv', k * jnp.exp(cum_last - cum), v)
    decay = jnp.exp(cum_last[:, :, :, 0, :])

    def step(s, inp):
        d, u = inp
        return d[..., None] * s + u, s

    s_fin, s_prev = lax.scan(step, s0, (jnp.moveaxis(decay, 2, 0), jnp.moveaxis(upd, 2, 0)))
    if not with_output:
        return None, s_fin
    s_prev = jnp.moveaxis(s_prev, 0, 2)
    q_dec = q * jnp.exp(cum)
    mask = jnp.tril(jnp.ones((GLA_CHUNK, GLA_CHUNK), dtype=bool))
    scores = jnp.where(mask, jnp.einsum('bhnld,bhnsd->bhnls', q_dec, k * jnp.exp(-cum)), 0.0)
    o = jnp.einsum('bhnls,bhnsv->bhnlv', scores, v) + jnp.einsum('bhnld,bhndv->bhnlv', q_dec, s_prev)
    return o.reshape(b, h, t, dv), s_fin


def mlstm_scan(q, k, v, ig, lf, state, with_output):
    b, h, t, dk = q.shape
    dv = v.shape[-1]
    n = t // MLSTM_CHUNK
    L = MLSTM_CHUNK
    q, k = (a.reshape(b, h, n, L, dk) for a in (q, k))
    v = v.reshape(b, h, n, L, dv)
    ig, lf = (a.reshape(b, h, n, L) for a in (ig, lf))
    cum = jnp.cumsum(lf, axis=3)
    cum_last = cum[..., -1]
    w_end = cum_last[..., None] - cum + ig
    m_chunk = jnp.max(w_end, axis=-1)
    w_k = jnp.exp(w_end - m_chunk[..., None])
    upd_c = jnp.einsum('bhnl,bhnld,bhnlv->bhndv', w_k, k, v)
    upd_n = jnp.einsum('bhnl,bhnld->bhnd', w_k, k)

    def step(carry, inp):
        c_st, n_st, m_st = carry
        fl, mc, uc, un = inp
        m_new = jnp.maximum(fl + m_st, mc)
        a = jnp.exp(fl + m_st - m_new)
        g = jnp.exp(mc - m_new)
        c_new = a[..., None, None] * c_st + g[..., None, None] * uc
        n_new = a[..., None] * n_st + g[..., None] * un
        return (c_new, n_new, m_new), (c_st, n_st, m_st)

    xs = tuple(jnp.moveaxis(a, 2, 0) for a in (cum_last, m_chunk, upd_c, upd_n))
    fin, prev = lax.scan(step, state, xs)
    if not with_output:
        return None, fin
    c_prev, n_prev, m_prev = (jnp.moveaxis(a, 0, 2) for a in prev)
    mask = jnp.tril(jnp.ones((L, L), dtype=bool))
    d_log = jnp.where(mask, cum[..., :, None] - cum[..., None, :] + ig[..., None, :], -jnp.inf)
    inter_log = cum + m_prev[..., None]
    m_row = jnp.maximum(inter_log, jnp.max(d_log, axis=-1))
    s = jnp.exp(d_log - m_row[..., None]) * jnp.einsum('bhnld,bhnsd->bhnls', q, k)
    w_int = jnp.exp(inter_log - m_row)
    num = jnp.einsum('bhnls,bhnsv->bhnlv', s, v) + w_int[..., None] * jnp.einsum('bhnld,bhndv->bhnlv', q, c_prev)
    den = jnp.sum(s, axis=-1) + w_int * jnp.einsum('bhnld,bhnd->bhnl', q, n_prev)
    out = num / jnp.maximum(jnp.abs(den), jnp.exp(-m_row))[..., None]
    return out.reshape(b, h, t, dv), fin


def gla_inputs(parts, w_lr, b_lr):
    b, t, _ = parts[0].shape
    q = heads(parts[0], GLA_DK) * GLA_DK ** -0.5
    k = heads(parts[1], GLA_DK)
    v = heads(parts[2], HEAD_DIM)
    lr = parts[4].reshape(b, t, 2, GLA_RANK)
    la = jax.nn.log_sigmoid(jnp.einsum('btjr,jrd->jbtd', lr, w_lr) + b_lr[:, None, None, :]) / GLA_NORMALIZER
    return q, k, v, heads(la[0], GLA_DK), heads(la[1], GLA_DK)


def gla_output(o, g, norm_g):
    o = o * lax.rsqrt(jnp.mean(o * o, axis=-1, keepdims=True) + EPS)
    return merge_heads(o) * norm_g * jax.nn.silu(g)


def gla_mixer(parts, parts_c, w_lr, b_lr, norm_g, ctx_out):
    q, k, v, la_f, la_b = gla_inputs(parts, w_lr, b_lr)
    qc, kc, vc, lac_f, lac_b = gla_inputs(parts_c, w_lr, b_lr)
    s0 = jnp.zeros(qc.shape[:2] + (GLA_DK, HEAD_DIM), F32)
    oc_f, s_f = gla_scan(qc, kc, vc, lac_f, s0, ctx_out)
    oc_b, s_b = gla_scan(flip_t(qc), flip_t(kc), flip_t(vc), flip_t(lac_b), s0, ctx_out)
    o_f, _ = gla_scan(q, k, v, la_f, s_f, True)
    o_b, _ = gla_scan(flip_t(q), flip_t(k), flip_t(v), flip_t(la_b), s_b, True)
    out = gla_output(o_f + flip_t(o_b), parts[3], norm_g)
    out_c = gla_output(oc_f + flip_t(oc_b), parts_c[3], norm_g) if ctx_out else None
    return out, out_c


def mlstm_inputs(parts, conv_w, gate_b, rows, cols):
    b, t, _ = parts[5].shape
    qk = jax.nn.silu(conv2d_grid(jnp.concatenate([parts[5], parts[6]], axis=-1), conv_w, rows, cols))
    q = heads(qk[..., :GROUP_W], HEAD_DIM)
    k = heads(qk[..., GROUP_W:], HEAD_DIM) * HEAD_DIM ** -0.5
    v = heads(parts[7], HEAD_DIM)
    gates = (parts[9].reshape(b, t, 2, 2, N_HEADS) + gate_b).transpose(2, 3, 0, 4, 1)
    return (q, k, v, gates[0, 0], jax.nn.log_sigmoid(gates[0, 1]),
            gates[1, 0], jax.nn.log_sigmoid(gates[1, 1]))


def mlstm_mixer(parts, parts_c, conv_w, gate_b, rows, ctx_out):
    q, k, v, i_f, f_f, i_b, f_b = mlstm_inputs(parts, conv_w, gate_b, rows, GRID_W)
    qc, kc, vc, ic_f, fc_f, ic_b, fc_b = mlstm_inputs(parts_c, conv_w, gate_b, 1, parts_c[5].shape[1])
    nb, nh = qc.shape[:2]
    st0 = (jnp.zeros((nb, nh, HEAD_DIM, HEAD_DIM), F32), jnp.zeros((nb, nh, HEAD_DIM), F32),
           jnp.zeros((nb, nh), F32))
    hc_f, st_f = mlstm_scan(qc, kc, vc, ic_f, fc_f, st0, ctx_out)
    hc_b, st_b = mlstm_scan(flip_t(qc), flip_t(kc), flip_t(vc), flip_t(ic_b), flip_t(fc_b), st0, ctx_out)
    h_f, _ = mlstm_scan(q, k, v, i_f, f_f, st_f, True)
    h_b, _ = mlstm_scan(flip_t(q), flip_t(k), flip_t(v), flip_t(i_b), flip_t(f_b), st_b, True)
    out = jax.nn.sigmoid(parts[8]) * merge_heads(h_f + flip_t(h_b))
    out_c = jax.nn.sigmoid(parts_c[8]) * merge_heads(hc_f + flip_t(hc_b)) if ctx_out else None
    return out, out_c


def sgu(u, v, w_s, b_s):
    b, t, _ = v.shape
    mu = jnp.mean(v, axis=-1, keepdims=True)
    vn = (v - mu) * lax.rsqrt(jnp.mean((v - mu) ** 2, axis=-1, keepdims=True) + EPS)
    vg = vn.reshape(b, t // SGU_CHUNK, SGU_CHUNK, N_HEADS, HEAD_DIM)
    mixed = jnp.einsum('gls,bnsgc->bnlgc', w_s, vg) + b_s.T[:, :, None]
    return u * mixed.reshape(b, t, GROUP_W)


def fourier_mix(u):
    b, t, _ = u.shape
    ug = u.astype(F32).reshape(b, t, N_HEADS, HEAD_DIM)
    return jnp.fft.fft2(ug, axes=(1, 3), norm='ortho').real.reshape(b, t, GROUP_W)


def token_mixers(h, hc, w_in, gla_w_lr, gla_b_lr, gla_norm_g, mlstm_conv, mlstm_gate_b, sgu_w, sgu_b, ctx_out):
    rows = h.shape[1] // GRID_W
    parts = split_cols((h @ w_in).astype(F32))
    parts_c = split_cols((hc @ (w_in if ctx_out else w_in[:, :REC_W])).astype(F32))
    gla, gla_c = gla_mixer(parts, parts_c, gla_w_lr, gla_b_lr, gla_norm_g, ctx_out)
    ml, ml_c = mlstm_mixer(parts, parts_c, mlstm_conv, mlstm_gate_b, rows, ctx_out)
    sg = sgu(parts[10], parts[11], sgu_w, sgu_b)
    fo = fourier_mix(parts[12])
    mix = jnp.concatenate([gla, ml, sg, fo], axis=-1).astype(h.dtype)
    if not ctx_out:
        return mix, None
    sg_c = sgu(parts_c[10], parts_c[11], sgu_w, sgu_b)
    fo_c = fourier_mix(parts_c[12])
    mix_c = jnp.concatenate([gla_c, ml_c, sg_c, fo_c], axis=-1).astype(hc.dtype)
    return mix, mix_c


def swiglu(h, w_g, w_u, w_d):
    return (jax.nn.silu(h @ w_g) * (h @ w_u)) @ w_d


def moe_ffn(h, w_router, w_g, w_u, w_d):
    b, t, d = h.shape
    hf = h.reshape(-1, d)
    logits = (hf @ w_router).astype(F32)
    top_v, top_i = lax.top_k(logits, TOP_K)
    gates = jax.nn.softmax(top_v, axis=-1)
    flat_e = top_i.reshape(-1)
    order = jnp.argsort(flat_e)
    tok = order // TOP_K
    xs = hf[tok]
    sizes = jnp.bincount(flat_e, length=N_EXPERTS).astype(jnp.int32)
    g = lax.ragged_dot(xs, w_g, sizes)
    u = lax.ragged_dot(xs, w_u, sizes)
    y = lax.ragged_dot(jax.nn.silu(g) * u, w_d, sizes)
    y = y * gates.reshape(-1)[order][:, None].astype(y.dtype)
    return jnp.zeros_like(hf).at[tok].add(y).reshape(b, t, d)


def setup_inputs(seed: int = 0) -> dict:
    key = jax.random.key(seed)
    ks = jax.random.split(key, 25)
    nrm = jax.random.normal
    s = D_MODEL ** -0.5
    fbias = jnp.zeros((DEPTH, 2, 2, N_HEADS), F32).at[:, :, 1, :].set(jnp.linspace(3.0, 6.0, N_HEADS))
    return {
        'x': nrm(ks[0], (BATCH, SEQ, D_MODEL), F32),
        'c': nrm(ks[1], (BATCH, D_MODEL), F32),
        'ctx': nrm(ks[2], (BATCH, CTX_LEN, D_MODEL), F32),
        'c_ctx': nrm(ks[3], (D_MODEL,), F32),
        'w_ada': nrm(ks[4], (DEPTH, D_MODEL, 6 * D_MODEL), F32) * s,
        'b_ada': 0.02 * nrm(ks[5], (DEPTH, 6 * D_MODEL), F32),
        'norm_mix_g': 1.0 + 0.1 * nrm(ks[6], (DEPTH, D_MODEL), F32),
        'norm_ffn_g': 1.0 + 0.1 * nrm(ks[7], (DEPTH, D_MODEL), F32),
        'w_in': nrm(ks[8], (DEPTH, D_MODEL, D_IN), F32) * s,
        'w_out': nrm(ks[9], (DEPTH, MIX_W, D_MODEL), F32) * MIX_W ** -0.5,
        'gla_w_lr': nrm(ks[10], (DEPTH, 2, GLA_RANK, GLA_QK_W), F32) * GLA_RANK ** -0.5,
        'gla_b_lr': 0.1 * nrm(ks[11], (DEPTH, 2, GLA_QK_W), F32),
        'gla_norm_g': 1.0 + 0.1 * nrm(ks[12], (DEPTH, GROUP_W), F32),
        'mlstm_conv': nrm(ks[13], (DEPTH, CONV_K, CONV_K, 2 * GROUP_W), F32) / CONV_K,
        'mlstm_gate_b': fbias + 0.1 * nrm(ks[14], (DEPTH, 2, 2, N_HEADS), F32),
        'sgu_w': nrm(ks[15], (DEPTH, N_HEADS, SGU_CHUNK, SGU_CHUNK), F32) * SGU_CHUNK ** -0.5,
        'sgu_b': 1.0 + 0.1 * nrm(ks[16], (DEPTH, N_HEADS, SGU_CHUNK), F32),
        'ffn_w_gate': nrm(ks[17], (N_DENSE, D_MODEL, D_FF), F32) * s,
        'ffn_w_up': nrm(ks[18], (N_DENSE, D_MODEL, D_FF), F32) * s,
        'ffn_w_down': nrm(ks[19], (N_DENSE, D_FF, D_MODEL), F32) * D_FF ** -0.5,
        'moe_router': nrm(ks[20], (N_MOE, D_MODEL, N_EXPERTS), F32) * s,
        'moe_w_gate': nrm(ks[21], (N_MOE, N_EXPERTS, D_MODEL, D_FF_EXPERT), F32) * s,
        'moe_w_up': nrm(ks[22], (N_MOE, N_EXPERTS, D_MODEL, D_FF_EXPERT), F32) * s,
        'moe_w_down': nrm(ks[23], (N_MOE, N_EXPERTS, D_FF_EXPERT, D_MODEL), F32) * D_FF_EXPERT ** -0.5,
        'final_norm_g': 1.0 + 0.1 * nrm(ks[24], (D_MODEL,), F32),
    }


def reference(x, c, ctx, c_ctx, w_ada, b_ada, norm_mix_g, norm_ffn_g, w_in, w_out, gla_w_lr, gla_b_lr,
              gla_norm_g, mlstm_conv, mlstm_gate_b, sgu_w, sgu_b, ffn_w_gate, ffn_w_up, ffn_w_down,
              moe_router, moe_w_gate, moe_w_up, moe_w_down, final_norm_g):
    xc = ctx
    for l in range(DEPTH):
        ctx_out = l < DEPTH - 1
        mod = jnp.split(jax.nn.silu(c) @ w_ada[l] + b_ada[l], 6, axis=-1)
        mod_c = jnp.split(jax.nn.silu(c_ctx) @ w_ada[l] + b_ada[l], 6, axis=-1)
        sh1, sc1, g1, sh2, sc2, g2 = (m[:, None, :] for m in mod)
        csh1, csc1, cg1, csh2, csc2, cg2 = mod_c
        h = modulate(rmsnorm(x, norm_mix_g[l]), sh1, sc1)
        hc = modulate(rmsnorm(xc, norm_mix_g[l]), csh1, csc1)
        mix, mix_c = token_mixers(h, hc, w_in[l], gla_w_lr[l], gla_b_lr[l], gla_norm_g[l], mlstm_conv[l],
                                  mlstm_gate_b[l], sgu_w[l], sgu_b[l], ctx_out)
        x = x + g1 * (mix @ w_out[l])
        if ctx_out:
            xc = xc + cg1 * (mix_c @ w_out[l])
        h = modulate(rmsnorm(x, norm_ffn_g[l]), sh2, sc2)
        i = l // 2
        if l % 2 == 0:
            x = x + g2 * swiglu(h, ffn_w_gate[i], ffn_w_up[i], ffn_w_down[i])
            if ctx_out:
                hc = modulate(rmsnorm(xc, norm_ffn_g[l]), csh2, csc2)
                xc = xc + cg2 * swiglu(hc, ffn_w_gate[i], ffn_w_up[i], ffn_w_down[i])
        else:
            x = x + g2 * moe_ffn(h, moe_router[i], moe_w_gate[i], moe_w_up[i], moe_w_down[i])
            if ctx_out:
                hc = modulate(rmsnorm(xc, norm_ffn_g[l]), csh2, csc2)
                xc = xc + cg2 * moe_ffn(hc, moe_router[i], moe_w_gate[i], moe_w_up[i], moe_w_down[i])
    return rmsnorm(x, final_norm_g)
```

```python
import functools

import numpy as np
import jax
import jax.numpy as jnp
from jax import lax
from jax.experimental import pallas as pl
from jax.experimental.pallas import tpu as pltpu

F32 = jnp.float32
BF16 = jnp.bfloat16
HIGHEST = lax.Precision.HIGHEST

D_MODEL = 1024
GROUP_W = 256
N_HEADS = 4
HEAD_DIM = 64
GLA_DK = 32
GLA_QK_W = N_HEADS * GLA_DK
GLA_RANK = 16
GLA_NORMALIZER = 16.0
CHUNK = 64
GRID_W = 64
SGU_CHUNK = 128
N_EXPERTS = 8
TOP_K = 2
EPS = 1e-6

LANES = 128
SUBLANES = 8
VMEM_LIMIT = 52 * 1024 * 1024

P_GLA_QK, P_GLA_V, P_GLA_G, P_ML_V = 0, 1, 2, 3
P_ML_QK = 2
P_ML_OG, P_SGU_U, P_SGU_V, P_FOUR = 6, 7, 8, 9
P_SMALL = 20
P_WIDTH = 2688
SMALL_GATE_LANE = 32

MOE_TM = 1024
MOE_TF = 512


def _cparams(sem, **kw):
    return pltpu.CompilerParams(dimension_semantics=sem, vmem_limit_bytes=VMEM_LIMIT, **kw)


def _sigmoid(x):
    return 1.0 / (1.0 + jnp.exp(-x))


def _silu(x):
    return x * _sigmoid(x)


def _log_sigmoid(x):
    return jnp.minimum(x, 0.0) - jnp.log(1.0 + jnp.exp(-jnp.abs(x)))


def _norm_mod(x, g, shift, scale):
    ms = jnp.mean(x * x, axis=-1, keepdims=True)
    return (x * lax.rsqrt(ms + EPS) * g) * (1.0 + scale) + shift


def _bdot(a, b):
    return jnp.dot(a.astype(BF16), b.astype(BF16), preferred_element_type=F32)


def _bdot_nt(a, b):
    return lax.dot_general(a.astype(BF16), b.astype(BF16), (((1,), (1,)), ((), ())), preferred_element_type=F32)


def _fdot(a, b):
    return jnp.dot(a, b, precision=HIGHEST, preferred_element_type=F32)


def _ada_kernel(c_ref, w_ref, b_ref, o_ref):
    o_ref[0] = _fdot(_silu(c_ref[...]), w_ref[0]) + b_ref[0]


def _ada(cvec, w_ada, b_ada):
    depth, d, n = w_ada.shape
    rows = cvec.shape[0]
    tn = 1536
    return pl.pallas_call(
        _ada_kernel,
        grid=(depth, n // tn),
        in_specs=[pl.BlockSpec((rows, d), lambda l, j: (0, 0)),
                  pl.BlockSpec((1, d, tn), lambda l, j: (l, 0, j)),
                  pl.BlockSpec((1, 1, tn), lambda l, j: (l, 0, j))],
        out_specs=pl.BlockSpec((1, rows, tn), lambda l, j: (l, 0, j)),
        out_shape=jax.ShapeDtypeStruct((depth, rows, n), F32),
        compiler_params=_cparams(("parallel", "parallel")),
        name="ada",
    )(cvec, w_ada, b_ada.reshape(depth, 1, n))


def _in_kernel(x_ref, mod_ref, g_ref, w_ref, o_ref):
    h = _norm_mod(x_ref[0], g_ref[...], mod_ref[0, 0:1, :], mod_ref[0, 1:2, :])
    o_ref[0] = jnp.dot(h.astype(BF16), w_ref[...], preferred_element_type=F32)


def _in_proj(x, mod, g, w):
    b, t, d = x.shape
    n = w.shape[1]
    tm = min(t, 512)
    mod_map = (lambda i, j: (i, 0, 0)) if mod.shape[0] == b else (lambda i, j: (0, 0, 0))
    return pl.pallas_call(
        _in_kernel,
        grid=(b, t // tm),
        in_specs=[pl.BlockSpec((1, tm, d), lambda i, j: (i, j, 0)),
                  pl.BlockSpec((1, 6, d), mod_map),
                  pl.BlockSpec((1, d), lambda i, j: (0, 0)),
                  pl.BlockSpec((d, n), lambda i, j: (0, 0))],
        out_specs=pl.BlockSpec((1, tm, n), lambda i, j: (i, j, 0)),
        out_shape=jax.ShapeDtypeStruct((b, t, n), F32),
        compiler_params=_cparams(("parallel", "parallel")),
        name="in_proj",
    )(x, mod, g.reshape(1, d), w)


def _out_kernel(x_ref, mod_ref, a_ref, b_ref, c_ref, d_ref, w_ref, o_ref):
    acc = jnp.dot(a_ref[0], w_ref[0:GROUP_W, :], preferred_element_type=F32)
    acc += jnp.dot(b_ref[0], w_ref[GROUP_W:2 * GROUP_W, :], preferred_element_type=F32)
    acc += jnp.dot(c_ref[0], w_ref[2 * GROUP_W:3 * GROUP_W, :], preferred_element_type=F32)
    acc += jnp.dot(d_ref[0], w_ref[3 * GROUP_W:4 * GROUP_W, :], preferred_element_type=F32)
    o_ref[0] = x_ref[0] + mod_ref[0, 2:3, :] * acc


def _out_proj(x, mod, mixes, w):
    b, t, d = x.shape
    tm = min(t, 512)
    mod_map = (lambda i, j: (i, 0, 0)) if mod.shape[0] == b else (lambda i, j: (0, 0, 0))
    mix_spec = pl.BlockSpec((1, tm, GROUP_W), lambda i, j: (i, j, 0))
    return pl.pallas_call(
        _out_kernel,
        grid=(b, t // tm),
        in_specs=[pl.BlockSpec((1, tm, d), lambda i, j: (i, j, 0)),
                  pl.BlockSpec((1, 6, d), mod_map),
                  mix_spec, mix_spec, mix_spec, mix_spec,
                  pl.BlockSpec((d, d), lambda i, j: (0, 0))],
        out_specs=pl.BlockSpec((1, tm, d), lambda i, j: (i, j, 0)),
        out_shape=jax.ShapeDtypeStruct((b, t, d), F32),
        compiler_params=_cparams(("parallel", "parallel")),
        name="out_proj",
    )(x, mod, *mixes, w)


def _ffn_kernel(x_ref, mod_ref, g_ref, wg_ref, wu_ref, wd_ref, o_ref, h_sc, acc_sc):
    j = pl.program_id(2)

    @pl.when(j == 0)
    def _():
        h = _norm_mod(x_ref[0], g_ref[...], mod_ref[0, 3:4, :], mod_ref[0, 4:5, :])
        h_sc[...] = h.astype(BF16)
        acc_sc[...] = jnp.zeros_like(acc_sc)

    h = h_sc[...]
    gate = jnp.dot(h, wg_ref[...], preferred_element_type=F32)
    up = jnp.dot(h, wu_ref[...], preferred_element_type=F32)
    acc_sc[...] += jnp.dot((_silu(gate) * up).astype(BF16), wd_ref[...], preferred_element_type=F32)

    @pl.when(j == pl.num_programs(2) - 1)
    def _():
        o_ref[0] = x_ref[0] + mod_ref[0, 5:6, :] * acc_sc[...]


def _ffn(x, mod, g, wg, wu, wd):
    b, t, d = x.shape
    f = wg.shape[1]
    tm = min(t, 512)
    tf = f // 2
    mod_map = (lambda i, j, k: (i, 0, 0)) if mod.shape[0] == b else (lambda i, j, k: (0, 0, 0))
    return pl.pallas_call(
        _ffn_kernel,
        grid=(b, t // tm, f // tf),
        in_specs=[pl.BlockSpec((1, tm, d), lambda i, j, k: (i, j, 0)),
                  pl.BlockSpec((1, 6, d), mod_map),
                  pl.BlockSpec((1, d), lambda i, j, k: (0, 0)),
                  pl.BlockSpec((d, tf), lambda i, j, k: (0, k)),
                  pl.BlockSpec((d, tf), lambda i, j, k: (0, k)),
                  pl.BlockSpec((tf, d), lambda i, j, k: (k, 0))],
        out_specs=pl.BlockSpec((1, tm, d), lambda i, j, k: (i, j, 0)),
        out_shape=jax.ShapeDtypeStruct((b, t, d), F32),
        scratch_shapes=[pltpu.VMEM((tm, d), BF16), pltpu.VMEM((tm, d), F32)],
        compiler_params=_cparams(("parallel", "parallel", "arbitrary")),
        name="ffn",
    )(x, mod, g.reshape(1, d), wg, wu, wd)


def _iota(shape, dim):
    return lax.broadcasted_iota(jnp.int32, shape, dim)


def _gla_chunk(q, k, v, la, s_ref, lower, want_out):
    L = q.shape[0]
    r, c = _iota((L, L), 0), _iota((L, L), 1)
    keep = (c <= r) if lower else (c >= r)
    tri = jnp.where(keep, 1.0, 0.0).astype(F32)
    cum = _fdot(tri, la)
    tot_col = _fdot(la.T, jnp.ones((L, GROUP_W), F32))
    tot_row = jnp.sum(la, axis=0, keepdims=True)
    s_prev = s_ref[...]
    out = None
    if want_out:
        qd = q * jnp.exp(cum)
        kd = k * jnp.exp(-cum)
        lane_q = _iota((1, GLA_QK_W), 1) // GLA_DK
        qs = jnp.concatenate([jnp.where(lane_q == h, qd, 0.0) for h in range(N_HEADS)], axis=0)
        sc = _bdot_nt(qs, kd)
        r4, c4 = _iota((N_HEADS * L, L), 0) & (L - 1), _iota((N_HEADS * L, L), 1)
        sc = jnp.where((c4 <= r4) if lower else (c4 >= r4), sc, 0.0)
        pv = _bdot(sc, v)
        lane_v = _iota((1, GROUP_W), 1) // HEAD_DIM
        out = _bdot(qd, s_prev)
        for h in range(N_HEADS):
            out = out + jnp.where(lane_v == h, pv[h * L:(h + 1) * L, :], 0.0)
    ke = k * jnp.exp(tot_row - cum)
    upd = jnp.dot(ke.T.astype(BF16), v.astype(BF16), preferred_element_type=F32)
    bd = (_iota((GLA_QK_W, GROUP_W), 0) // GLA_DK) == (_iota((GLA_QK_W, GROUP_W), 1) // HEAD_DIM)
    s_ref[...] = jnp.exp(tot_col) * s_prev + jnp.where(bd, upd, 0.0)
    return out


def _gla_kernel(qk_ref, v_ref, g_ref, sm_ref, qkc_ref, vc_ref, gc_ref, smc_ref, wlr_ref, blr_ref, ng_ref,
                *rest, ctx_out):
    if ctx_out:
        o_ref, oc_ref, la_sc, lac_sc, of_sc, ob_sc, ofc_sc, obc_sc, sf_sc, sb_sc = rest
    else:
        o_ref, la_sc, lac_sc, of_sc, ob_sc, sf_sc, sb_sc = rest
        oc_ref = ofc_sc = obc_sc = None
    t = qk_ref.shape[1]
    tc = qkc_ref.shape[1]
    qscale = GLA_DK ** -0.5

    def log_decay(sm):
        z = _fdot(sm, wlr_ref[...]) + blr_ref[...]
        return _log_sigmoid(z) * (1.0 / GLA_NORMALIZER)

    lac_sc[...] = log_decay(smc_ref[0])
    rows_blk = 256

    def la_body(i, carry):
        r0 = pl.multiple_of(i * rows_blk, rows_blk)
        la_sc[pl.ds(r0, rows_blk), :] = log_decay(sm_ref[0, pl.ds(r0, rows_blk), :])
        return carry

    lax.fori_loop(0, t // rows_blk, la_body, 0)
    sf_sc[...] = jnp.zeros_like(sf_sc)
    sb_sc[...] = jnp.zeros_like(sb_sc)

    def run(n_rows, qk, v, la, of, ob, want_out):
        n = n_rows // CHUNK

        def body(i, carry):
            rf = pl.multiple_of(i * CHUNK, CHUNK)
            rb = pl.multiple_of((n - 1 - i) * CHUNK, CHUNK)
            for r0, s_ref, lower, dst, lane0 in ((rf, sf_sc, True, of, 0), (rb, sb_sc, False, ob, GLA_QK_W)):
                qkb = qk[0, pl.ds(r0, CHUNK), :]
                out = _gla_chunk(qkb[:, :GLA_QK_W] * qscale, qkb[:, GLA_QK_W:], v[0, pl.ds(r0, CHUNK), :],
                                 la[pl.ds(r0, CHUNK), lane0:lane0 + GLA_QK_W], s_ref, lower, want_out)
                if want_out:
                    dst[pl.ds(r0, CHUNK), :] = out
            return carry

        lax.fori_loop(0, n, body, 0)

    run(tc, qkc_ref, vc_ref, lac_sc, ofc_sc, obc_sc, ctx_out)
    run(t, qk_ref, v_ref, la_sc, of_sc, ob_sc, True)

    grp = jnp.where((_iota((GROUP_W, GROUP_W), 0) // HEAD_DIM) == (_iota((GROUP_W, GROUP_W), 1) // HEAD_DIM),
                    1.0 / HEAD_DIM, 0.0).astype(F32)

    def finish(n_rows, of, ob, g, dst):
        blk = min(n_rows, 256)

        def body(i, carry):
            r0 = pl.multiple_of(i * blk, blk)
            o = of[pl.ds(r0, blk), :] + ob[pl.ds(r0, blk), :]
            ms = _fdot(o * o, grp)
            res = o * lax.rsqrt(ms + EPS) * ng_ref[...] * _silu(g[0, pl.ds(r0, blk), :])
            dst[0, pl.ds(r0, blk), :] = res.astype(dst.dtype)
            return carry

        lax.fori_loop(0, n_rows // blk, body, 0)

    finish(t, of_sc, ob_sc, g_ref, o_ref)
    if ctx_out:
        finish(tc, ofc_sc, obc_sc, gc_ref, oc_ref)


def _gla(p, pc, wlr, blr, ng, ctx_out):
    b, t, _ = p.shape
    tc = pc.shape[1]

    def col(width, idx, rows):
        return pl.BlockSpec((1, rows, width), lambda i: (i, 0, idx))

    full = lambda shape: pl.BlockSpec(shape, lambda i: tuple(0 for _ in shape))
    in_specs = [col(GROUP_W, P_GLA_QK, t), col(GROUP_W, P_GLA_V, t), col(GROUP_W, P_GLA_G, t), col(LANES, P_SMALL, t),
                col(GROUP_W, P_GLA_QK, tc), col(GROUP_W, P_GLA_V, tc), col(GROUP_W, P_GLA_G, tc),
                col(LANES, P_SMALL, tc),
                full(wlr.shape), full(blr.shape), full(ng.shape)]
    out_specs = [pl.BlockSpec((1, t, GROUP_W), lambda i: (i, 0, 0))]
    out_shape = [jax.ShapeDtypeStruct((b, t, GROUP_W), BF16)]
    scratch = [pltpu.VMEM((t, 2 * GLA_QK_W), F32), pltpu.VMEM((tc, 2 * GLA_QK_W), F32),
               pltpu.VMEM((t, GROUP_W), F32), pltpu.VMEM((t, GROUP_W), F32)]
    if ctx_out:
        out_specs.append(pl.BlockSpec((1, tc, GROUP_W), lambda i: (i, 0, 0)))
        out_shape.append(jax.ShapeDtypeStruct((b, tc, GROUP_W), BF16))
        scratch += [pltpu.VMEM((tc, GROUP_W), F32), pltpu.VMEM((tc, GROUP_W), F32)]
    scratch += [pltpu.VMEM((GLA_QK_W, GROUP_W), F32), pltpu.VMEM((GLA_QK_W, GROUP_W), F32)]
    res = pl.pallas_call(
        functools.partial(_gla_kernel, ctx_out=ctx_out),
        grid=(b,),
        in_specs=in_specs, out_specs=out_specs, out_shape=out_shape, scratch_shapes=scratch,
        compiler_params=_cparams(("parallel",)),
        name="gla",
    )(p, p, p, p, pc, pc, pc, pc, wlr, blr, ng)
    return (res[0], res[1]) if ctx_out else (res[0], None)


def _mlstm_chunk(q, k, va, gz, lf, cn_ref, m_prev, lower, want_out, d):
    L = q.shape[0]
    r, c = _iota((L, L), 0), _iota((L, L), 1)
    keep = (c <= r) if lower else (c >= r)
    tri = jnp.where(keep, 1.0, 0.0).astype(F32)
    cum_all = _fdot(tri, lf)
    lane = _iota((1, LANES), 1)
    f_lane0 = SMALL_GATE_LANE + d * 8 + 4
    i_lane0 = SMALL_GATE_LANE + d * 8
    is_f = (lane >= f_lane0) & (lane < f_lane0 + N_HEADS)
    g_cols = jnp.where(is_f, cum_all, gz)
    g_rows = g_cols.T
    last = L - 1 if lower else 0
    outs, m_new = [], []
    for h in range(N_HEADS):
        cum_col = g_cols[:, f_lane0 + h:f_lane0 + h + 1]
        ig_col = g_cols[:, i_lane0 + h:i_lane0 + h + 1]
        cum_row = g_rows[f_lane0 + h:f_lane0 + h + 1, :]
        ig_row = g_rows[i_lane0 + h:i_lane0 + h + 1, :]
        tot = cum_row[:, last:last + 1]
        qh = q[:, h * HEAD_DIM:(h + 1) * HEAD_DIM]
        kh = k[:, h * HEAD_DIM:(h + 1) * HEAD_DIM]
        cn_prev = cn_ref[h]
        mp = m_prev[h]
        if want_out:
            d_log = jnp.where(keep, cum_col - cum_row + ig_row, -jnp.inf)
            inter_log = cum_col + mp
            m_row = jnp.maximum(inter_log, jnp.max(d_log, axis=1, keepdims=True))
            s = jnp.exp(d_log - m_row) * _bdot_nt(qh, kh)
            w_int = jnp.exp(inter_log - m_row)
            nd = _bdot(s, va[h]) + w_int * _bdot(qh, cn_prev)
            den = pltpu.roll(nd, HEAD_DIM, 1)[:, :HEAD_DIM]
            outs.append(nd[:, :HEAD_DIM] / jnp.maximum(jnp.abs(den), jnp.exp(-m_row)))
        w_end_col = tot - cum_col + ig_col
        w_end_row = tot - cum_row + ig_row
        m_chunk = jnp.max(w_end_row, axis=1, keepdims=True)
        kw = kh * jnp.exp(w_end_col - m_chunk)
        upd = jnp.dot(kw.T.astype(BF16), va[h].astype(BF16), preferred_element_type=F32)
        mn = jnp.maximum(tot + mp, m_chunk)
        cn_ref[h] = jnp.exp(tot + mp - mn) * cn_prev + jnp.exp(m_chunk - mn) * upd
        m_new.append(mn)
    return (outs if want_out else None), m_new


def _mlstm_kernel(qk_ref, v_ref, og_ref, sm_ref, qkc_ref, vc_ref, ogc_ref, smc_ref, cw_ref, gb_ref,
                  *rest, ctx_out):
    if ctx_out:
        o_ref, oc_ref, qc_sc, qcc_sc, hf_sc, hb_sc, hfc_sc, hbc_sc, cnf_sc, cnb_sc = rest
    else:
        o_ref, qc_sc, qcc_sc, hf_sc, hb_sc, cnf_sc, cnb_sc = rest
        oc_ref = hfc_sc = hbc_sc = None
    t = qk_ref.shape[1]
    tc = qkc_ref.shape[1]
    w2 = 2 * GROUP_W
    kscale = HEAD_DIM ** -0.5
    n_rows_grid = t // GRID_W

    def tap(dr, dc):
        return cw_ref[dr * 3 + dc:dr * 3 + dc + 1, :]

    def shifted(blk, dc):
        n = blk.shape[0]
        if dc == 1:
            return blk
        ridx = _iota((n, 1), 0)
        if dc == 0:
            return jnp.where(ridx == 0, 0.0, pltpu.roll(blk, 1, 0))
        return jnp.where(ridx == n - 1, 0.0, pltpu.roll(blk, n - 1, 0))

    def finish_qk(acc):
        a = _silu(acc)
        lane = _iota((1, w2), 1)
        return jnp.where(lane >= GROUP_W, a * kscale, a)

    def conv_body(i, carry):
        r0 = pl.multiple_of(i * GRID_W, GRID_W)
        r_up = pl.multiple_of(jnp.maximum(i - 1, 0) * GRID_W, GRID_W)
        r_dn = pl.multiple_of(jnp.minimum(i + 1, n_rows_grid - 1) * GRID_W, GRID_W)
        up = qk_ref[0, pl.ds(r_up, GRID_W), :] * jnp.where(i > 0, 1.0, 0.0)
        mid = qk_ref[0, pl.ds(r0, GRID_W), :]
        dn = qk_ref[0, pl.ds(r_dn, GRID_W), :] * jnp.where(i < n_rows_grid - 1, 1.0, 0.0)
        acc = jnp.zeros((GRID_W, w2), F32)
        for dr, blk in enumerate((up, mid, dn)):
            for dc in range(3):
                acc = acc + shifted(blk, dc) * tap(dr, dc)
        qc_sc[pl.ds(r0, GRID_W), :] = finish_qk(acc)
        return carry

    lax.fori_loop(0, n_rows_grid, conv_body, 0)
    xc = qkc_ref[0]
    acc = jnp.zeros((tc, w2), F32)
    for dc in range(3):
        acc = acc + shifted(xc, dc) * tap(1, dc)
    qcc_sc[...] = finish_qk(acc)

    cnf_sc[...] = jnp.zeros_like(cnf_sc)
    cnb_sc[...] = jnp.zeros_like(cnb_sc)
    ones = jnp.ones((CHUNK, HEAD_DIM), F32)

    def run(n_rows, qc, v, sm, hf, hb, want_out, m0):
        n = n_rows // CHUNK

        def body(i, m):
            rf = pl.multiple_of(i * CHUNK, CHUNK)
            rb = pl.multiple_of((n - 1 - i) * CHUNK, CHUNK)
            m_out = []
            for d, (r0, cn_ref, lower, dst) in enumerate(((rf, cnf_sc, True, hf), (rb, cnb_sc, False, hb))):
                qk = qc[pl.ds(r0, CHUNK), :]
                vb = v[0, pl.ds(r0, CHUNK), :]
                va = [jnp.concatenate([vb[:, h * HEAD_DIM:(h + 1) * HEAD_DIM], ones], axis=1)
                      for h in range(N_HEADS)]
                gz = sm[0, pl.ds(r0, CHUNK), :] + gb_ref[...]
                outs, mn = _mlstm_chunk(qk[:, :GROUP_W], qk[:, GROUP_W:], va, gz, _log_sigmoid(gz), cn_ref,
                                        m[d * N_HEADS:(d + 1) * N_HEADS], lower, want_out, d)
                if want_out:
                    for h in range(N_HEADS):
                        dst[pl.ds(r0, CHUNK), h * HEAD_DIM:(h + 1) * HEAD_DIM] = outs[h]
                m_out += mn
            return tuple(m_out)

        return lax.fori_loop(0, n, body, m0)

    m0 = tuple(jnp.zeros((1, 1), F32) for _ in range(2 * N_HEADS))
    m1 = run(tc, qcc_sc, vc_ref, smc_ref, hfc_sc, hbc_sc, ctx_out, m0)
    run(t, qc_sc, v_ref, sm_ref, hf_sc, hb_sc, True, m1)

    def finish(n_rows, hf, hb, og, dst):
        blk = min(n_rows, 256)

        def body(i, carry):
            r0 = pl.multiple_of(i * blk, blk)
            res = _sigmoid(og[0, pl.ds(r0, blk), :]) * (hf[pl.ds(r0, blk), :] + hb[pl.ds(r0, blk), :])
            dst[0, pl.ds(r0, blk), :] = res.astype(dst.dtype)
            return carry

        lax.fori_loop(0, n_rows // blk, body, 0)

    finish(t, hf_sc, hb_sc, og_ref, o_ref)
    if ctx_out:
        finish(tc, hfc_sc, hbc_sc, ogc_ref, oc_ref)


def _mlstm(p, pc, conv_w, gate_b, ctx_out):
    b, t, _ = p.shape
    tc = pc.shape[1]

    def col(width, idx, rows):
        return pl.BlockSpec((1, rows, width), lambda i: (i, 0, idx))

    full = lambda shape: pl.BlockSpec(shape, lambda i: tuple(0 for _ in shape))
    in_specs = [col(2 * GROUP_W, P_ML_QK, t), col(GROUP_W, P_ML_V, t), col(GROUP_W, P_ML_OG, t),
                col(LANES, P_SMALL, t),
                col(2 * GROUP_W, P_ML_QK, tc), col(GROUP_W, P_ML_V, tc), col(GROUP_W, P_ML_OG, tc),
                col(LANES, P_SMALL, tc),
                full(conv_w.shape), full(gate_b.shape)]
    out_specs = [pl.BlockSpec((1, t, GROUP_W), lambda i: (i, 0, 0))]
    out_shape = [jax.ShapeDtypeStruct((b, t, GROUP_W), BF16)]
    scratch = [pltpu.VMEM((t, 2 * GROUP_W), F32), pltpu.VMEM((tc, 2 * GROUP_W), F32),
               pltpu.VMEM((t, GROUP_W), F32), pltpu.VMEM((t, GROUP_W), F32)]
    if ctx_out:
        out_specs.append(pl.BlockSpec((1, tc, GROUP_W), lambda i: (i, 0, 0)))
        out_shape.append(jax.ShapeDtypeStruct((b, tc, GROUP_W), BF16))
        scratch += [pltpu.VMEM((tc, GROUP_W), F32), pltpu.VMEM((tc, GROUP_W), F32)]
    scratch += [pltpu.VMEM((N_HEADS, HEAD_DIM, 2 * HEAD_DIM), F32), pltpu.VMEM((N_HEADS, HEAD_DIM, 2 * HEAD_DIM), F32)]
    res = pl.pallas_call(
        functools.partial(_mlstm_kernel, ctx_out=ctx_out),
        grid=(b,),
        in_specs=in_specs, out_specs=out_specs, out_shape=out_shape, scratch_shapes=scratch,
        compiler_params=_cparams(("parallel",)),
        name="mlstm",
    )(p, p, p, p, pc, pc, pc, pc, conv_w, gate_b)
    return (res[0], res[1]) if ctx_out else (res[0], None)


def _sgu_kernel(u_ref, v_ref, f_ref, ws_ref, bias_ref, dft_ref, o_ref, z_ref):
    t = u_ref.shape[1]
    lane_g = _iota((1, GROUP_W), 1) // HEAD_DIM

    def body(i, carry):
        r0 = pl.multiple_of(i * SGU_CHUNK, SGU_CHUNK)
        v = v_ref[0, pl.ds(r0, SGU_CHUNK), :]
        mu = jnp.mean(v, axis=-1, keepdims=True)
        vc = v - mu
        vn = vc * lax.rsqrt(jnp.mean(vc * vc, axis=-1, keepdims=True) + EPS)
        mixed = bias_ref[...]
        for g in range(N_HEADS):
            mixed = mixed + _bdot(ws_ref[g], jnp.where(lane_g == g, vn, 0.0))
        o_ref[0, pl.ds(r0, SGU_CHUNK), :] = (u_ref[0, pl.ds(r0, SGU_CHUNK), :] * mixed).astype(o_ref.dtype)
        z = _bdot(f_ref[0, pl.ds(r0, SGU_CHUNK), :], dft_ref[...])
        z_ref[0, 0, pl.ds(r0, SGU_CHUNK), :] = z[:, :GROUP_W].astype(z_ref.dtype)
        z_ref[1, 0, pl.ds(r0, SGU_CHUNK), :] = z[:, GROUP_W:].astype(z_ref.dtype)
        return carry

    lax.fori_loop(0, t // SGU_CHUNK, body, 0)


def _sgu_fourier1(p, ws, bias, dft_c):
    b, t, _ = p.shape

    def col(idx):
        return pl.BlockSpec((1, t, GROUP_W), lambda i: (i, 0, idx))

    full = lambda shape: pl.BlockSpec(shape, lambda i: tuple(0 for _ in shape))
    return pl.pallas_call(
        _sgu_kernel,
        grid=(b,),
        in_specs=[col(P_SGU_U), col(P_SGU_V), col(P_FOUR), full(ws.shape), full(bias.shape), full(dft_c.shape)],
        out_specs=[pl.BlockSpec((1, t, GROUP_W), lambda i: (i, 0, 0)),
                   pl.BlockSpec((2, 1, t, GROUP_W), lambda i: (0, i, 0, 0))],
        out_shape=[jax.ShapeDtypeStruct((b, t, GROUP_W), BF16), jax.ShapeDtypeStruct((2, b, t, GROUP_W), BF16)],
        compiler_params=_cparams(("parallel",)),
        name="sgu_fourier1",
    )(p, p, p, ws, bias, dft_c)


def _fourier2_kernel(ct_ref, st_ref, z_ref, o_ref, *, scale):
    for i in range(z_ref.shape[1]):
        acc = jnp.dot(ct_ref[...], z_ref[0, i], preferred_element_type=F32)
        acc -= jnp.dot(st_ref[...], z_ref[1, i], preferred_element_type=F32)
        o_ref[i] = (acc * scale).astype(o_ref.dtype)


def _fourier2(z, cos_t, sin_t):
    _, b, t, w = z.shape
    tm = min(t, 256)
    return pl.pallas_call(
        functools.partial(_fourier2_kernel, scale=float((t * HEAD_DIM) ** -0.5)),
        grid=(t // tm,),
        in_specs=[pl.BlockSpec((tm, t), lambda i: (i, 0)),
                  pl.BlockSpec((tm, t), lambda i: (i, 0)),
                  pl.BlockSpec((2, b, t, w), lambda i: (0, 0, 0, 0), pipeline_mode=pl.Buffered(1))],
        out_specs=pl.BlockSpec((b, tm, w), lambda i: (0, i, 0)),
        out_shape=jax.ShapeDtypeStruct((b, t, w), BF16),
        compiler_params=_cparams(("parallel",)),
        name="fourier2",
    )(cos_t, sin_t, z)


def _dft_tables(n):
    k = np.arange(n, dtype=np.int64)
    ang = 2.0 * np.pi * ((k[:, None] * k[None, :]) % n).astype(np.float64) / n
    return np.cos(ang).astype(np.float32), np.sin(ang).astype(np.float32)


def _channel_dft():
    c, s = _dft_tables(HEAD_DIM)
    eye = np.eye(N_HEADS, dtype=np.float32)
    return np.concatenate([np.kron(eye, c), np.kron(eye, s)], axis=1)


def _route_kernel(x_ref, mod_ref, g_ref, wr_ref, h_ref, mi_ref, mf_ref, cnt_ref, cnt_sc):
    first = (pl.program_id(0) == 0) & (pl.program_id(1) == 0)

    @pl.when(first)
    def _():
        cnt_sc[...] = jnp.zeros_like(cnt_sc)

    tm = x_ref.shape[1]
    h = _norm_mod(x_ref[0], g_ref[...], mod_ref[0, 3:4, :], mod_ref[0, 4:5, :])
    for j in range(D_MODEL // LANES):
        h_ref[pl.ds(j, tm, stride=SUBLANES), :] = h[:, j * LANES:(j + 1) * LANES]
    logits = _fdot(h, wr_ref[...])
    lane = _iota((tm, LANES), 1).astype(F32)
    lg = jnp.where(lane < N_EXPERTS, logits, -jnp.inf)
    m1 = jnp.max(lg, axis=1, keepdims=True)
    i1 = jnp.min(jnp.where(lg == m1, lane, float(LANES)), axis=1, keepdims=True)
    lg2 = jnp.where(lane == i1, -jnp.inf, lg)
    m2 = jnp.max(lg2, axis=1, keepdims=True)
    i2 = jnp.min(jnp.where(lg2 == m2, lane, float(LANES)), axis=1, keepdims=True)
    e = jnp.exp(m2 - m1)
    g0 = 1.0 / (1.0 + e)
    g1 = e / (1.0 + e)
    onehot = jnp.where((lane == i1) | (lane == i2), 1.0, 0.0)
    strict = jnp.where(_iota((tm, tm), 1) < _iota((tm, tm), 0), 1.0, 0.0)
    before = _bdot(strict, onehot) + cnt_sc[...]
    r0 = jnp.sum(jnp.where(lane == i1, before, 0.0), axis=1, keepdims=True)
    r1 = jnp.sum(jnp.where(lane == i2, before, 0.0), axis=1, keepdims=True)
    cnt_sc[...] += jnp.sum(onehot, axis=0, keepdims=True)
    cnt_ref[...] = cnt_sc[...]
    l8 = _iota((tm, SUBLANES), 1)
    mi = jnp.where(l8 == 0, i1, jnp.where(l8 == 1, i2, jnp.where(l8 == 2, r0, r1)))
    mi_ref[...] = mi.astype(jnp.int32)
    mf_ref[...] = jnp.where(l8 == 0, g0, g1)


def _route(x, mod, g, w_router):
    b, t, d = x.shape
    m = b * t
    tm = 512
    nt = t // tm
    wr = jnp.zeros((d, LANES), F32).at[:, :N_EXPERTS].set(w_router)
    return pl.pallas_call(
        _route_kernel,
        grid=(b, nt),
        in_specs=[pl.BlockSpec((1, tm, d), lambda i, j: (i, j, 0)),
                  pl.BlockSpec((1, 6, d), lambda i, j: (i, 0, 0)),
                  pl.BlockSpec((1, d), lambda i, j: (0, 0)),
                  pl.BlockSpec((d, LANES), lambda i, j: (0, 0))],
        out_specs=[pl.BlockSpec((tm * SUBLANES, LANES), lambda i, j: (i * nt + j, 0)),
                   pl.BlockSpec((tm, SUBLANES), lambda i, j: (i * nt + j, 0)),
                   pl.BlockSpec((tm, SUBLANES), lambda i, j: (i * nt + j, 0)),
                   pl.BlockSpec((1, LANES), lambda i, j: (0, 0))],
        out_shape=[jax.ShapeDtypeStruct((m * SUBLANES, LANES), F32),
                   jax.ShapeDtypeStruct((m, SUBLANES), jnp.int32),
                   jax.ShapeDtypeStruct((m, SUBLANES), F32),
                   jax.ShapeDtypeStruct((1, LANES), F32)],
        scratch_shapes=[pltpu.VMEM((1, LANES), F32)],
        compiler_params=_cparams(("arbitrary", "arbitrary")),
        name="route",
    )(x, mod, g.reshape(1, d), wr)


def _moe_kernel(texp_ref, trows_ref, pos0_ref, pos1_ref, h_hbm, wg_ref, wu_ref, wd_ref, y_hbm,
                order_sm, xbuf, x_sc, acc_sc, ybuf, gsem, ssem):
    i = pl.program_id(0)
    j = pl.program_id(1)
    nf = pl.num_programs(1)
    tm = MOE_TM
    n_tok = pos0_ref.shape[0]
    rows = trows_ref[i]

    @pl.when((i == 0) & (j == 0))
    def _():
        def inv_body(t, carry):
            order_sm[pos0_ref[t]] = 2 * t
            order_sm[pos1_ref[t]] = 2 * t + 1
            return carry

        lax.fori_loop(0, n_tok, inv_body, 0)
        xbuf[...] = jnp.zeros_like(xbuf)

    def row_copy(r, gather):
        flat = order_sm[i * tm + r]
        off = pl.multiple_of(r * SUBLANES, SUBLANES)
        if gather:
            return pltpu.make_async_copy(h_hbm.at[jnp.right_shift(flat, 1)], xbuf.at[pl.ds(off, SUBLANES), :], gsem)
        return pltpu.make_async_copy(ybuf.at[pl.ds(off, SUBLANES), :], y_hbm.at[flat], ssem)

    @pl.when((j == 0) & (rows > 0))
    def _():
        def start(r, carry):
            row_copy(r, True).start()
            return carry

        lax.fori_loop(0, rows, start, 0)

        def wait(r, carry):
            row_copy(r, True).wait()
            return carry

        lax.fori_loop(0, rows, wait, 0)
        for c in range(D_MODEL // LANES):
            x_sc[:, c * LANES:(c + 1) * LANES] = xbuf[pl.ds(c, tm, stride=SUBLANES), :].astype(BF16)
        acc_sc[...] = jnp.zeros_like(acc_sc)

    @pl.when(rows > 0)
    def _():
        x = x_sc[...]
        gate = jnp.dot(x, wg_ref[0].astype(BF16), preferred_element_type=F32)
        up = jnp.dot(x, wu_ref[0].astype(BF16), preferred_element_type=F32)
        acc_sc[...] += jnp.dot((_silu(gate) * up).astype(BF16), wd_ref[0].astype(BF16),
                               preferred_element_type=F32)

    @pl.when((j == nf - 1) & (rows > 0))
    def _():
        for c in range(D_MODEL // LANES):
            ybuf[pl.ds(c, tm, stride=SUBLANES), :] = acc_sc[:, c * LANES:(c + 1) * LANES]

        def start(r, carry):
            row_copy(r, False).start()
            return carry

        lax.fori_loop(0, rows, start, 0)

        def wait(r, carry):
            row_copy(r, False).wait()
            return carry

        lax.fori_loop(0, rows, wait, 0)


def _moe(h_tiles, pos0, pos1, tile_expert, tile_rows, wg, wu, wd):
    m = h_tiles.shape[0]
    n_exp, d, f = wg.shape
    nt = tile_expert.shape[0]
    tm, tf = MOE_TM, MOE_TF
    nf = f // tf

    def col_tile(i, j, tr):
        return jnp.where(tr[i] > 0, j, nf - 1)

    grid_spec = pltpu.PrefetchScalarGridSpec(
        num_scalar_prefetch=4,
        grid=(nt, nf),
        in_specs=[pl.BlockSpec(memory_space=pl.ANY),
                  pl.BlockSpec((1, d, tf), lambda i, j, te, tr, p0, p1: (te[i], 0, col_tile(i, j, tr))),
                  pl.BlockSpec((1, d, tf), lambda i, j, te, tr, p0, p1: (te[i], 0, col_tile(i, j, tr))),
                  pl.BlockSpec((1, tf, d), lambda i, j, te, tr, p0, p1: (te[i], col_tile(i, j, tr), 0))],
        out_specs=pl.BlockSpec(memory_space=pl.ANY),
        scratch_shapes=[pltpu.SMEM((nt * tm,), jnp.int32),
                        pltpu.VMEM((tm * SUBLANES, LANES), F32),
                        pltpu.VMEM((tm, d), BF16),
                        pltpu.VMEM((tm, d), F32),
                        pltpu.VMEM((tm * SUBLANES, LANES), F32),
                        pltpu.SemaphoreType.DMA(()),
                        pltpu.SemaphoreType.DMA(())],
    )
    return pl.pallas_call(
        _moe_kernel,
        grid_spec=grid_spec,
        out_shape=jax.ShapeDtypeStruct((TOP_K * m, SUBLANES, LANES), F32),
        compiler_params=_cparams(("arbitrary", "arbitrary")),
        name="moe",
    )(tile_expert, tile_rows, pos0, pos1, h_tiles, wg, wu, wd)


def _combine_kernel(x_ref, mod_ref, y_ref, mf_ref, fg_ref, o_ref):
    tm = x_ref.shape[1]
    stride = TOP_K * SUBLANES
    g0 = mf_ref[:, 0:1]
    g1 = mf_ref[:, 1:2]
    parts = []
    for c in range(D_MODEL // LANES):
        y0 = y_ref[pl.ds(c, tm, stride=stride), :]
        y1 = y_ref[pl.ds(SUBLANES + c, tm, stride=stride), :]
        parts.append(g0 * y0 + g1 * y1)
    moe = jnp.concatenate(parts, axis=1)
    x = x_ref[0] + mod_ref[0, 5:6, :] * moe
    ms = jnp.mean(x * x, axis=-1, keepdims=True)
    o_ref[0] = x * lax.rsqrt(ms + EPS) * fg_ref[...]


def _combine(x, mod, y, mf, final_g):
    b, t, d = x.shape
    tm = 256
    nt = t // tm
    rows = tm * TOP_K * SUBLANES
    return pl.pallas_call(
        _combine_kernel,
        grid=(b, nt),
        in_specs=[pl.BlockSpec((1, tm, d), lambda i, j: (i, j, 0)),
                  pl.BlockSpec((1, 6, d), lambda i, j: (i, 0, 0)),
                  pl.BlockSpec((rows, LANES), lambda i, j: (i * nt + j, 0)),
                  pl.BlockSpec((tm, SUBLANES), lambda i, j: (i * nt + j, 0)),
                  pl.BlockSpec((1, d), lambda i, j: (0, 0))],
        out_specs=pl.BlockSpec((1, tm, d), lambda i, j: (i, j, 0)),
        out_shape=jax.ShapeDtypeStruct((b, t, d), F32),
        compiler_params=_cparams(("parallel", "parallel")),
        name="combine",
    )(x, mod, y, mf, final_g.reshape(1, d))


def _moe_layer(x, mod, norm_g, w_router, wg, wu, wd, final_g):
    b, t, d = x.shape
    m = b * t
    h_tiles, mi, mf, cnt = _route(x, mod, norm_g, w_router)
    counts = cnt[0, :N_EXPERTS].astype(jnp.int32)
    tiles_per = (counts + MOE_TM - 1) // MOE_TM
    tile_end = jnp.cumsum(tiles_per)
    tile_start = tile_end - tiles_per
    seg_start = tile_start * MOE_TM
    nt = (TOP_K * m) // MOE_TM + N_EXPERTS
    tile_id = jnp.arange(nt, dtype=jnp.int32)
    used_id = jnp.minimum(tile_id, tile_end[-1] - 1)
    tile_expert = jnp.sum(used_id[:, None] >= tile_end[None, :], axis=1).astype(jnp.int32)
    tile_rows = jnp.clip(counts[tile_expert] - (tile_id - tile_start[tile_expert]) * MOE_TM, 0, MOE_TM)
    tile_rows = jnp.where(tile_id < tile_end[-1], tile_rows, 0).astype(jnp.int32)
    pos0 = seg_start[mi[:, 0]] + mi[:, 2]
    pos1 = seg_start[mi[:, 1]] + mi[:, 3]
    y = _moe(h_tiles.reshape(m, SUBLANES, LANES), pos0, pos1, tile_expert, tile_rows, wg, wu, wd)
    return _combine(x, mod, y.reshape(TOP_K * m * SUBLANES, LANES), mf, final_g)


def _permute_w_in(w):
    d = w.shape[0]
    return jnp.concatenate([w[:, 0:768], w[:, 1312:1568], w[:, 800:1312], w[:, 1568:1824], w[:, 1840:2608],
                            w[:, 768:800], w[:, 1824:1840], jnp.zeros((d, P_WIDTH - 2608), w.dtype)],
                           axis=1).astype(BF16)


def _mixers(p, pc, gla_w_lr, gla_b_lr, gla_norm_g, mlstm_conv, mlstm_gate_b, sgu_w, sgu_b, ctx_out, tables):
    wlr = jnp.zeros((LANES, 2 * GLA_QK_W), F32)
    wlr = wlr.at[0:GLA_RANK, 0:GLA_QK_W].set(gla_w_lr[0]).at[GLA_RANK:2 * GLA_RANK, GLA_QK_W:].set(gla_w_lr[1])
    blr = gla_b_lr.reshape(1, 2 * GLA_QK_W)
    gla, gla_c = _gla(p, pc, wlr, blr, gla_norm_g.reshape(1, GROUP_W), ctx_out)
    conv_w = mlstm_conv.reshape(9, 2 * GROUP_W)
    gate_b = jnp.zeros((1, LANES), F32).at[0, SMALL_GATE_LANE:SMALL_GATE_LANE + 4 * N_HEADS].set(
        mlstm_gate_b.reshape(-1))
    ml, ml_c = _mlstm(p, pc, conv_w, gate_b, ctx_out)
    bias = jnp.repeat(sgu_b.T, HEAD_DIM, axis=1)
    dft_c, tabs = tables
    sg, z = _sgu_fourier1(p, sgu_w, bias, dft_c)
    fo = _fourier2(z, *tabs[p.shape[1]])
    mix = (gla, ml, sg, fo)
    if not ctx_out:
        return mix, None
    sg_c, z_c = _sgu_fourier1(pc, sgu_w, bias, dft_c)
    fo_c = _fourier2(z_c, *tabs[pc.shape[1]])
    return mix, (gla_c, ml_c, sg_c, fo_c)


def kernel(x, c, ctx, c_ctx, w_ada, b_ada, norm_mix_g, norm_ffn_g, w_in, w_out, gla_w_lr, gla_b_lr, gla_norm_g,
           mlstm_conv, mlstm_gate_b, sgu_w, sgu_b, ffn_w_gate, ffn_w_up, ffn_w_down, moe_router, moe_w_gate,
           moe_w_up, moe_w_down, final_norm_g):
    depth = w_ada.shape[0]
    assert depth == 2, "layer 0 is the dense layer with context output, layer 1 the MoE layer"
    b, t, d = x.shape
    tc = ctx.shape[1]
    cvec = jnp.zeros((16, d), F32).at[:b].set(c).at[b].set(c_ctx)
    mods = _ada(cvec, w_ada, b_ada).reshape(depth, 16, 6, d)
    tables = (jnp.asarray(_channel_dft()).astype(BF16),
              {n: tuple(jnp.asarray(a).astype(BF16) for a in _dft_tables(n)) for n in {t, tc}})
    xc = ctx
    for l in range(depth):
        ctx_out = l < depth - 1
        mod, mod_c = mods[l, :b], mods[l, b:b + 1]
        w_in_l = _permute_w_in(w_in[l])
        w_out_l = w_out[l].astype(BF16)
        p = _in_proj(x, mod, norm_mix_g[l], w_in_l)
        pc = _in_proj(xc, mod_c, norm_mix_g[l], w_in_l)
        mix, mix_c = _mixers(p, pc, gla_w_lr[l], gla_b_lr[l], gla_norm_g[l], mlstm_conv[l], mlstm_gate_b[l],
                             sgu_w[l], sgu_b[l], ctx_out, tables)
        x = _out_proj(x, mod, mix, w_out_l)
        if ctx_out:
            xc = _out_proj(xc, mod_c, mix_c, w_out_l)
        i = l // 2
        if l % 2 == 0:
            wg, wu, wd = ffn_w_gate[i].astype(BF16), ffn_w_up[i].astype(BF16), ffn_w_down[i].astype(BF16)
            x = _ffn(x, mod, norm_ffn_g[l], wg, wu, wd)
            if ctx_out:
                xc = _ffn(xc, mod_c, norm_ffn_g[l], wg, wu, wd)
        else:
            x = _moe_layer(x, mod, norm_ffn_g[l], moe_router[i], moe_w_gate[i], moe_w_up[i], moe_w_down[i],
                           final_norm_g)
    return x
```

```python
import functools

import numpy as np
import jax
import jax.numpy as jnp
from jax import lax
from jax.experimental import pallas as pl
from jax.experimental.pallas import tpu as pltpu

F32 = jnp.float32
BF16 = jnp.bfloat16
HIGHEST = lax.Precision.HIGHEST

D_MODEL = 1024
GROUP_W = 256
N_HEADS = 4
HEAD_DIM = 64
GLA_DK = 32
GLA_QK_W = N_HEADS * GLA_DK
GLA_RANK = 16
GLA_NORMALIZER = 16.0
GLA_BLOCK = 128
GRID_W = 64
SGU_CHUNK = 128
N_EXPERTS = 8
TOP_K = 2
EPS = 1e-6

LANES = 128
SUBLANES = 8
VMEM_LIMIT = 52 * 1024 * 1024

P_GLA_QK, P_GLA_V, P_GLA_G, P_ML_V = 0, 1, 2, 3
P_ML_QK = 2
P_ML_OG, P_SGU_U, P_SGU_V, P_FOUR = 6, 7, 8, 9
P_SMALL = 20
P_WIDTH = 2688
SMALL_GATE_LANE = 32

MOE_TM = 1024
MOE_TF = 512


def _cparams(sem, **kw):
    return pltpu.CompilerParams(dimension_semantics=sem, vmem_limit_bytes=VMEM_LIMIT, **kw)


def _sigmoid(x):
    return 1.0 / (1.0 + jnp.exp(-x))


def _silu(x):
    return x * _sigmoid(x)


def _log_sigmoid(x):
    return jnp.minimum(x, 0.0) - jnp.log(1.0 + jnp.exp(-jnp.abs(x)))


def _norm_mod(x, g, shift, scale):
    ms = jnp.mean(x * x, axis=-1, keepdims=True)
    return (x * lax.rsqrt(ms + EPS) * g) * (1.0 + scale) + shift


def _bdot(a, b):
    return jnp.dot(a.astype(BF16), b.astype(BF16), preferred_element_type=F32)


def _bdot_nt(a, b):
    return lax.dot_general(a.astype(BF16), b.astype(BF16), (((1,), (1,)), ((), ())), preferred_element_type=F32)


def _fdot(a, b):
    return jnp.dot(a, b, precision=HIGHEST, preferred_element_type=F32)


def _split3(x):
    hi = x.astype(BF16)
    r = x - hi.astype(F32)
    mid = r.astype(BF16)
    lo = (r - mid.astype(F32)).astype(BF16)
    return hi, mid, lo


def _dot3(a, b):
    a_hi, a_lo, _ = _split3(a)
    b_hi, b_lo, _ = _split3(b)
    dot = functools.partial(jnp.dot, preferred_element_type=F32)
    return dot(a_hi, b_hi) + dot(a_lo, b_hi) + dot(a_hi, b_lo)


def _tri_dot(tri, x):
    n = x.shape[1]
    y = jnp.dot(tri.astype(BF16), jnp.concatenate(_split3(x), axis=1), preferred_element_type=F32)
    return y[:, :n] + y[:, n:2 * n] + y[:, 2 * n:]


def _ada_kernel(c_ref, w_ref, b_ref, o_ref):
    o_ref[0] = _fdot(_silu(c_ref[...]), w_ref[0]) + b_ref[0]


def _ada(cvec, w_ada, b_ada):
    depth, d, n = w_ada.shape
    rows = cvec.shape[0]
    tn = 1536
    return pl.pallas_call(
        _ada_kernel,
        grid=(depth, n // tn),
        in_specs=[pl.BlockSpec((rows, d), lambda l, j: (0, 0)),
                  pl.BlockSpec((1, d, tn), lambda l, j: (l, 0, j)),
                  pl.BlockSpec((1, 1, tn), lambda l, j: (l, 0, j))],
        out_specs=pl.BlockSpec((1, rows, tn), lambda l, j: (l, 0, j)),
        out_shape=jax.ShapeDtypeStruct((depth, rows, n), F32),
        compiler_params=_cparams(("parallel", "parallel")),
        name="ada",
    )(cvec, w_ada, b_ada.reshape(depth, 1, n))


def _in_kernel(x_ref, mod_ref, g_ref, w_ref, o_ref):
    h = _norm_mod(x_ref[0], g_ref[...], mod_ref[0, 0:1, :], mod_ref[0, 1:2, :])
    o_ref[0] = jnp.dot(h.astype(BF16), w_ref[...], preferred_element_type=F32)


def _in_proj(x, mod, g, w):
    b, t, d = x.shape
    n = w.shape[1]
    tm = min(t, 512)
    mod_map = (lambda i, j: (i, 0, 0)) if mod.shape[0] == b else (lambda i, j: (0, 0, 0))
    return pl.pallas_call(
        _in_kernel,
        grid=(b, t // tm),
        in_specs=[pl.BlockSpec((1, tm, d), lambda i, j: (i, j, 0)),
                  pl.BlockSpec((1, 6, d), mod_map),
                  pl.BlockSpec((1, d), lambda i, j: (0, 0)),
                  pl.BlockSpec((d, n), lambda i, j: (0, 0))],
        out_specs=pl.BlockSpec((1, tm, n), lambda i, j: (i, j, 0)),
        out_shape=jax.ShapeDtypeStruct((b, t, n), F32),
        compiler_params=_cparams(("parallel", "parallel")),
        name="in_proj",
    )(x, mod, g.reshape(1, d), w)


def _out_kernel(x_ref, mod_ref, a_ref, b_ref, c_ref, d_ref, w_ref, o_ref):
    acc = jnp.dot(a_ref[0], w_ref[0:GROUP_W, :], preferred_element_type=F32)
    acc += jnp.dot(b_ref[0], w_ref[GROUP_W:2 * GROUP_W, :], preferred_element_type=F32)
    acc += jnp.dot(c_ref[0], w_ref[2 * GROUP_W:3 * GROUP_W, :], preferred_element_type=F32)
    acc += jnp.dot(d_ref[0], w_ref[3 * GROUP_W:4 * GROUP_W, :], preferred_element_type=F32)
    o_ref[0] = x_ref[0] + mod_ref[0, 2:3, :] * acc


def _out_proj(x, mod, mixes, w):
    b, t, d = x.shape
    tm = min(t, 512)
    mod_map = (lambda i, j: (i, 0, 0)) if mod.shape[0] == b else (lambda i, j: (0, 0, 0))
    mix_spec = pl.BlockSpec((1, tm, GROUP_W), lambda i, j: (i, j, 0))
    return pl.pallas_call(
        _out_kernel,
        grid=(b, t // tm),
        in_specs=[pl.BlockSpec((1, tm, d), lambda i, j: (i, j, 0)),
                  pl.BlockSpec((1, 6, d), mod_map),
                  mix_spec, mix_spec, mix_spec, mix_spec,
                  pl.BlockSpec((d, d), lambda i, j: (0, 0))],
        out_specs=pl.BlockSpec((1, tm, d), lambda i, j: (i, j, 0)),
        out_shape=jax.ShapeDtypeStruct((b, t, d), F32),
        compiler_params=_cparams(("parallel", "parallel")),
        name="out_proj",
    )(x, mod, *mixes, w)


def _ffn_kernel(x_ref, mod_ref, g_ref, wg_ref, wu_ref, wd_ref, o_ref, h_sc, acc_sc):
    j = pl.program_id(2)

    @pl.when(j == 0)
    def _():
        h = _norm_mod(x_ref[0], g_ref[...], mod_ref[0, 3:4, :], mod_ref[0, 4:5, :])
        h_sc[...] = h.astype(BF16)
        acc_sc[...] = jnp.zeros_like(acc_sc)

    h = h_sc[...]
    gate = jnp.dot(h, wg_ref[...], preferred_element_type=F32)
    up = jnp.dot(h, wu_ref[...], preferred_element_type=F32)
    acc_sc[...] += jnp.dot((_silu(gate) * up).astype(BF16), wd_ref[...], preferred_element_type=F32)

    @pl.when(j == pl.num_programs(2) - 1)
    def _():
        o_ref[0] = x_ref[0] + mod_ref[0, 5:6, :] * acc_sc[...]


def _ffn(x, mod, g, wg, wu, wd):
    b, t, d = x.shape
    f = wg.shape[1]
    tm = min(t, 512)
    tf = f // 2
    mod_map = (lambda i, j, k: (i, 0, 0)) if mod.shape[0] == b else (lambda i, j, k: (0, 0, 0))
    return pl.pallas_call(
        _ffn_kernel,
        grid=(b, t // tm, f // tf),
        in_specs=[pl.BlockSpec((1, tm, d), lambda i, j, k: (i, j, 0)),
                  pl.BlockSpec((1, 6, d), mod_map),
                  pl.BlockSpec((1, d), lambda i, j, k: (0, 0)),
                  pl.BlockSpec((d, tf), lambda i, j, k: (0, k)),
                  pl.BlockSpec((d, tf), lambda i, j, k: (0, k)),
                  pl.BlockSpec((tf, d), lambda i, j, k: (k, 0))],
        out_specs=pl.BlockSpec((1, tm, d), lambda i, j, k: (i, j, 0)),
        out_shape=jax.ShapeDtypeStruct((b, t, d), F32),
        scratch_shapes=[pltpu.VMEM((tm, d), BF16), pltpu.VMEM((tm, d), F32)],
        compiler_params=_cparams(("parallel", "parallel", "arbitrary")),
        name="ffn",
    )(x, mod, g.reshape(1, d), wg, wu, wd)


def _iota(shape, dim):
    return lax.broadcasted_iota(jnp.int32, shape, dim)


def _gla_chunk(q, k, v, la, s_ref, lower, want_out):
    L = q.shape[0]
    r, c = _iota((L, L), 0), _iota((L, L), 1)
    keep = (c <= r) if lower else (c >= r)
    cum = _tri_dot(jnp.where(keep, 1.0, 0.0), la)
    tot_row = jnp.sum(la, axis=0, keepdims=True)
    s_prev = s_ref[...]
    out = None
    if want_out:
        mid = L // 2 - 1 if lower else L // 2
        ref = cum[mid:mid + 1, :]
        qr = q * jnp.exp(cum - ref)
        kc = (k * jnp.exp(ref - cum)).astype(BF16)
        lane_q = _iota((1, GLA_QK_W), 1) // GLA_DK
        lane_v = _iota((1, GROUP_W), 1) // HEAD_DIM
        out = _bdot_nt(q * jnp.exp(cum), s_prev)
        for h in range(N_HEADS):
            sc = jnp.where(keep, _bdot_nt(jnp.where(lane_q == h, qr, 0.0), kc), 0.0)
            out = out + _bdot(sc, jnp.where(lane_v == h, v, 0.0))
    ke = k * jnp.exp(tot_row - cum)
    upd = jnp.dot(v.T.astype(BF16), ke.astype(BF16), preferred_element_type=F32)
    bd = (_iota((GROUP_W, GLA_QK_W), 0) // HEAD_DIM) == (_iota((GROUP_W, GLA_QK_W), 1) // GLA_DK)
    s_ref[...] = jnp.exp(tot_row) * s_prev + jnp.where(bd, upd, 0.0)
    return out


def _gla_kernel(qk_ref, v_ref, g_ref, sm_ref, qkc_ref, vc_ref, gc_ref, smc_ref, wlr_ref, blr_ref, ng_ref,
                *rest, ctx_out):
    if ctx_out:
        o_ref, oc_ref, la_sc, lac_sc, of_sc, ob_sc, ofc_sc, obc_sc, sf_sc, sb_sc = rest
    else:
        o_ref, la_sc, lac_sc, of_sc, ob_sc, sf_sc, sb_sc = rest
        oc_ref = ofc_sc = obc_sc = None
    t = qk_ref.shape[1]
    tc = qkc_ref.shape[1]
    qscale = GLA_DK ** -0.5

    def log_decay(sm):
        z = _dot3(sm, wlr_ref[...]) + blr_ref[...]
        return _log_sigmoid(z) * (1.0 / GLA_NORMALIZER)

    lac_sc[...] = log_decay(smc_ref[0])
    rows_blk = 256

    def la_body(i, carry):
        r0 = pl.multiple_of(i * rows_blk, rows_blk)
        la_sc[pl.ds(r0, rows_blk), :] = log_decay(sm_ref[0, pl.ds(r0, rows_blk), :])
        return carry

    lax.fori_loop(0, t // rows_blk, la_body, 0)
    sf_sc[...] = jnp.zeros_like(sf_sc)
    sb_sc[...] = jnp.zeros_like(sb_sc)

    def run(n_rows, qk, v, la, of, ob, want_out):
        n = n_rows // GLA_BLOCK

        def body(i, carry):
            rf = pl.multiple_of(i * GLA_BLOCK, GLA_BLOCK)
            rb = pl.multiple_of((n - 1 - i) * GLA_BLOCK, GLA_BLOCK)
            for r0, s_ref, lower, dst, lane0 in ((rf, sf_sc, True, of, 0), (rb, sb_sc, False, ob, GLA_QK_W)):
                qkb = qk[0, pl.ds(r0, GLA_BLOCK), :]
                out = _gla_chunk(qkb[:, :GLA_QK_W] * qscale, qkb[:, GLA_QK_W:], v[0, pl.ds(r0, GLA_BLOCK), :],
                                 la[pl.ds(r0, GLA_BLOCK), lane0:lane0 + GLA_QK_W], s_ref, lower, want_out)
                if want_out:
                    dst[pl.ds(r0, GLA_BLOCK), :] = out
            return carry

        lax.fori_loop(0, n, body, 0)

    run(tc, qkc_ref, vc_ref, lac_sc, ofc_sc, obc_sc, ctx_out)
    run(t, qk_ref, v_ref, la_sc, of_sc, ob_sc, True)

    grp = jnp.where((_iota((GROUP_W, GROUP_W), 0) // HEAD_DIM) == (_iota((GROUP_W, GROUP_W), 1) // HEAD_DIM),
                    1.0 / HEAD_DIM, 0.0).astype(BF16)

    def finish(n_rows, of, ob, g, dst):
        blk = min(n_rows, 256)

        def body(i, carry):
            r0 = pl.multiple_of(i * blk, blk)
            o = of[pl.ds(r0, blk), :] + ob[pl.ds(r0, blk), :]
            sq_hi, sq_lo, _ = _split3(o * o)
            ms = (jnp.dot(sq_hi, grp, preferred_element_type=F32) + jnp.dot(sq_lo, grp, preferred_element_type=F32))
            res = o * lax.rsqrt(ms + EPS) * ng_ref[...] * _silu(g[0, pl.ds(r0, blk), :])
            dst[0, pl.ds(r0, blk), :] = res.astype(dst.dtype)
            return carry

        lax.fori_loop(0, n_rows // blk, body, 0)

    finish(t, of_sc, ob_sc, g_ref, o_ref)
    if ctx_out:
        finish(tc, ofc_sc, obc_sc, gc_ref, oc_ref)


def _gla(p, pc, wlr, blr, ng, ctx_out):
    b, t, _ = p.shape
    tc = pc.shape[1]

    def col(width, idx, rows):
        return pl.BlockSpec((1, rows, width), lambda i: (i, 0, idx))

    full = lambda shape: pl.BlockSpec(shape, lambda i: tuple(0 for _ in shape))
    in_specs = [col(GROUP_W, P_GLA_QK, t), col(GROUP_W, P_GLA_V, t), col(GROUP_W, P_GLA_G, t), col(LANES, P_SMALL, t),
                col(GROUP_W, P_GLA_QK, tc), col(GROUP_W, P_GLA_V, tc), col(GROUP_W, P_GLA_G, tc),
                col(LANES, P_SMALL, tc),
                full(wlr.shape), full(blr.shape), full(ng.shape)]
    out_specs = [pl.BlockSpec((1, t, GROUP_W), lambda i: (i, 0, 0))]
    out_shape = [jax.ShapeDtypeStruct((b, t, GROUP_W), BF16)]
    scratch = [pltpu.VMEM((t, 2 * GLA_QK_W), F32), pltpu.VMEM((tc, 2 * GLA_QK_W), F32),
               pltpu.VMEM((t, GROUP_W), F32), pltpu.VMEM((t, GROUP_W), F32)]
    if ctx_out:
        out_specs.append(pl.BlockSpec((1, tc, GROUP_W), lambda i: (i, 0, 0)))
        out_shape.append(jax.ShapeDtypeStruct((b, tc, GROUP_W), BF16))
        scratch += [pltpu.VMEM((tc, GROUP_W), F32), pltpu.VMEM((tc, GROUP_W), F32)]
    scratch += [pltpu.VMEM((GROUP_W, GLA_QK_W), F32), pltpu.VMEM((GROUP_W, GLA_QK_W), F32)]
    res = pl.pallas_call(
        functools.partial(_gla_kernel, ctx_out=ctx_out),
        grid=(b,),
        in_specs=in_specs, out_specs=out_specs, out_shape=out_shape, scratch_shapes=scratch,
        compiler_params=_cparams(("parallel",)),
        name="gla",
    )(p, p, p, p, pc, pc, pc, pc, wlr, blr, ng)
    return (res[0], res[1]) if ctx_out else (res[0], None)


ML_BLOCK = 128
N_PAIRS = N_HEADS // 2


def _gate_lane(d, h):
    return SMALL_GATE_LANE + 8 * d + h


def _mlstm_selectors():
    sel_head = np.zeros((LANES, 2 * N_HEADS * LANES), np.float32)
    sel_pair = np.zeros((LANES, 2 * N_PAIRS * LANES), np.float32)
    for d in range(2):
        for h in range(N_HEADS):
            sel_head[_gate_lane(d, h), (d * N_HEADS + h) * LANES:(d * N_HEADS + h + 1) * LANES] = 1.0
            p, hh = divmod(h, 2)
            lo = (d * N_PAIRS + p) * LANES + hh * HEAD_DIM
            sel_pair[_gate_lane(d, h), lo:lo + HEAD_DIM] = 1.0
    return sel_head, sel_pair


def _mlstm_block(blk_f, blk_b, cn_ref, m_prev, selh_ref, selp_ref, want_out):
    L = ML_BLOCK
    lane = _iota((1, LANES), 1)
    is_b = (lane >= _gate_lane(1, 0)) & (lane < _gate_lane(1, 0) + N_HEADS)
    valid = ((lane >= _gate_lane(0, 0)) & (lane < _gate_lane(0, 0) + N_HEADS)) | is_b
    r, c = _iota((L, L), 0), _iota((L, L), 1)
    keep_f, keep_b = c <= r, c >= r
    lf_f = pltpu.roll(_log_sigmoid(blk_f[2]), LANES - 4, 1)
    lf_b = pltpu.roll(_log_sigmoid(blk_b[2]), LANES - 4, 1)
    cum_f = _tri_dot(jnp.where(keep_f, 1.0, 0.0), lf_f)
    cum_b = _tri_dot(jnp.where(keep_b, 1.0, 0.0), lf_b)
    cum = jnp.where(valid, jnp.where(is_b, cum_b, cum_f), 0.0)
    ig = jnp.where(valid, jnp.where(is_b, blk_b[2], blk_f[2]), 0.0)
    tot = jnp.where(valid, jnp.where(is_b, cum_b[0:1, :], cum_f[L - 1:L, :]), 0.0)
    b = ig - cum
    row = _iota((L, 1), 0)
    pm = b
    k = 1
    while k < L:
        from_prev = jnp.where(row >= k, pltpu.roll(pm, k, 0), -jnp.inf)
        from_next = jnp.where(row < L - k, pltpu.roll(pm, L - k, 0), -jnp.inf)
        pm = jnp.maximum(pm, jnp.where(is_b, from_next, from_prev))
        k *= 2
    mm = jnp.maximum(m_prev, pm)
    w_end = tot + b
    m_chunk = jnp.max(w_end, axis=0, keepdims=True)
    m_new = jnp.maximum(tot + m_prev, m_chunk)
    a_row = jnp.exp(tot + m_prev - m_new)
    g_row = jnp.exp(m_chunk - m_new)
    cols = [jnp.exp(w_end - m_chunk)]
    if want_out:
        cols += [jnp.exp(m_prev - mm), jnp.exp(-(cum + mm))]
        mm_hi, mm_lo, _ = _split3(mm)
        mm_t = (jnp.dot(mm_hi, selh_ref[...], preferred_element_type=F32)
                + jnp.dot(mm_lo, selh_ref[...], preferred_element_type=F32))
        b_rows = b.T
    pair_t = _bdot(jnp.concatenate(cols, axis=0), selp_ref[...])
    lo_half = lane < HEAD_DIM
    ones_t = jnp.ones((L, LANES), F32)
    bd = (_iota((LANES, 2 * LANES), 0) // HEAD_DIM) == ((_iota((LANES, 2 * LANES), 1) % LANES) // HEAD_DIM)
    outs = {}
    for d, (blk, keep) in enumerate(((blk_f, keep_f), (blk_b, keep_b))):
        qk, v, _ = blk
        for p in range(N_PAIRS):
            qp = qk[:, p * LANES:(p + 1) * LANES]
            kp = qk[:, GROUP_W + p * LANES:GROUP_W + (p + 1) * LANES]
            vp = v[:, p * LANES:(p + 1) * LANES]
            cn = cn_ref[d * N_PAIRS + p]
            tcol = (d * N_PAIRS + p) * LANES
            if want_out:
                nd = None
                for hh in range(2):
                    h = 2 * p + hh
                    mine = lo_half if hh == 0 else jnp.logical_not(lo_half)
                    gl = _gate_lane(d, h)
                    hcol = (d * N_HEADS + h) * LANES
                    e = jnp.where(keep, jnp.exp(b_rows[gl:gl + 1, :] - mm_t[:, hcol:hcol + LANES]), 0.0)
                    s = e * _bdot_nt(jnp.where(mine, qp, 0.0), kp)
                    va = jnp.concatenate([jnp.where(mine, vp, 0.0), jnp.where(mine, ones_t, 0.0)], axis=1)
                    t = _bdot(s, va)
                    nd = t if nd is None else nd + t
                w_int = pair_t[L:2 * L, tcol:tcol + LANES]
                nd = nd + jnp.concatenate([w_int, w_int], axis=1) * _bdot(qp, cn)
                outs[(d, p)] = nd[:, :LANES] / jnp.maximum(jnp.abs(nd[:, LANES:]),
                                                           pair_t[2 * L:3 * L, tcol:tcol + LANES])
            kw_t = (kp * pair_t[0:L, tcol:tcol + LANES]).T
            upd = _bdot(kw_t, jnp.concatenate([vp, ones_t], axis=1))
            ga, gb = _gate_lane(d, 2 * p), _gate_lane(d, 2 * p + 1)
            half = (HEAD_DIM, 2 * LANES)
            a_t = jnp.concatenate([jnp.broadcast_to(a_row[:, ga:ga + 1], half),
                                   jnp.broadcast_to(a_row[:, gb:gb + 1], half)], axis=0)
            g_t = jnp.concatenate([jnp.broadcast_to(g_row[:, ga:ga + 1], half),
                                   jnp.broadcast_to(g_row[:, gb:gb + 1], half)], axis=0)
            cn_ref[d * N_PAIRS + p] = a_t * cn + g_t * jnp.where(bd, upd, 0.0)
    return (outs if want_out else None), m_new


def _mlstm_kernel(qk_ref, v_ref, og_ref, sm_ref, qkc_ref, vc_ref, ogc_ref, smc_ref, cw_ref, gb_ref,
                  selh_ref, selp_ref, *rest, ctx_out):
    if ctx_out:
        o_ref, oc_ref, qc_sc, qcc_sc, hf_sc, hb_sc, hfc_sc, hbc_sc, cn_sc = rest
    else:
        o_ref, qc_sc, qcc_sc, hf_sc, hb_sc, cn_sc = rest
        oc_ref = hfc_sc = hbc_sc = None
    t = qk_ref.shape[1]
    tc = qkc_ref.shape[1]
    w2 = 2 * GROUP_W
    kscale = HEAD_DIM ** -0.5
    n_rows_grid = t // GRID_W

    def tap(dr, dc):
        return cw_ref[dr * 3 + dc:dr * 3 + dc + 1, :]

    def shifted(blk, dc):
        n = blk.shape[0]
        if dc == 1:
            return blk
        ridx = _iota((n, 1), 0)
        if dc == 0:
            return jnp.where(ridx == 0, 0.0, pltpu.roll(blk, 1, 0))
        return jnp.where(ridx == n - 1, 0.0, pltpu.roll(blk, n - 1, 0))

    def finish_qk(acc):
        a = _silu(acc)
        lane = _iota((1, w2), 1)
        return jnp.where(lane >= GROUP_W, a * kscale, a)

    def conv_body(i, carry):
        r0 = pl.multiple_of(i * GRID_W, GRID_W)
        r_up = pl.multiple_of(jnp.maximum(i - 1, 0) * GRID_W, GRID_W)
        r_dn = pl.multiple_of(jnp.minimum(i + 1, n_rows_grid - 1) * GRID_W, GRID_W)
        up = qk_ref[0, pl.ds(r_up, GRID_W), :] * jnp.where(i > 0, 1.0, 0.0)
        mid = qk_ref[0, pl.ds(r0, GRID_W), :]
        dn = qk_ref[0, pl.ds(r_dn, GRID_W), :] * jnp.where(i < n_rows_grid - 1, 1.0, 0.0)
        acc = jnp.zeros((GRID_W, w2), F32)
        for dr, blk in enumerate((up, mid, dn)):
            for dc in range(3):
                acc = acc + shifted(blk, dc) * tap(dr, dc)
        qc_sc[pl.ds(r0, GRID_W), :] = finish_qk(acc)
        return carry

    lax.fori_loop(0, n_rows_grid, conv_body, 0)
    xc = qkc_ref[0]
    acc = jnp.zeros((tc, w2), F32)
    for dc in range(3):
        acc = acc + shifted(xc, dc) * tap(1, dc)
    qcc_sc[...] = finish_qk(acc)

    cn_sc[...] = jnp.zeros_like(cn_sc)

    def run(n_rows, qc, v, sm, hf, hb, want_out, m0):
        n = n_rows // ML_BLOCK

        def body(i, m):
            rf = pl.multiple_of(i * ML_BLOCK, ML_BLOCK)
            rb = pl.multiple_of((n - 1 - i) * ML_BLOCK, ML_BLOCK)
            blks = [(qc[pl.ds(r0, ML_BLOCK), :], v[0, pl.ds(r0, ML_BLOCK), :],
                     sm[0, pl.ds(r0, ML_BLOCK), :] + gb_ref[...]) for r0 in (rf, rb)]
            outs, m_new = _mlstm_block(blks[0], blks[1], cn_sc, m, selh_ref, selp_ref, want_out)
            if want_out:
                for p in range(N_PAIRS):
                    hf[pl.ds(rf, ML_BLOCK), p * LANES:(p + 1) * LANES] = outs[(0, p)]
                    hb[pl.ds(rb, ML_BLOCK), p * LANES:(p + 1) * LANES] = outs[(1, p)]
            return m_new

        return lax.fori_loop(0, n, body, m0)

    m1 = run(tc, qcc_sc, vc_ref, smc_ref, hfc_sc, hbc_sc, ctx_out, jnp.zeros((1, LANES), F32))
    run(t, qc_sc, v_ref, sm_ref, hf_sc, hb_sc, True, m1)

    def finish(n_rows, hf, hb, og, dst):
        blk = min(n_rows, 256)

        def body(i, carry):
            r0 = pl.multiple_of(i * blk, blk)
            res = _sigmoid(og[0, pl.ds(r0, blk), :]) * (hf[pl.ds(r0, blk), :] + hb[pl.ds(r0, blk), :])
            dst[0, pl.ds(r0, blk), :] = res.astype(dst.dtype)
            return carry

        lax.fori_loop(0, n_rows // blk, body, 0)

    finish(t, hf_sc, hb_sc, og_ref, o_ref)
    if ctx_out:
        finish(tc, hfc_sc, hbc_sc, ogc_ref, oc_ref)


def _mlstm(p, pc, conv_w, gate_b, ctx_out):
    b, t, _ = p.shape
    tc = pc.shape[1]
    sel_head, sel_pair = (jnp.asarray(a).astype(BF16) for a in _mlstm_selectors())

    def col(width, idx, rows):
        return pl.BlockSpec((1, rows, width), lambda i: (i, 0, idx))

    full = lambda shape: pl.BlockSpec(shape, lambda i: tuple(0 for _ in shape))
    in_specs = [col(2 * GROUP_W, P_ML_QK, t), col(GROUP_W, P_ML_V, t), col(GROUP_W, P_ML_OG, t),
                col(LANES, P_SMALL, t),
                col(2 * GROUP_W, P_ML_QK, tc), col(GROUP_W, P_ML_V, tc), col(GROUP_W, P_ML_OG, tc),
                col(LANES, P_SMALL, tc),
                full(conv_w.shape), full(gate_b.shape), full(sel_head.shape), full(sel_pair.shape)]
    out_specs = [pl.BlockSpec((1, t, GROUP_W), lambda i: (i, 0, 0))]
    out_shape = [jax.ShapeDtypeStruct((b, t, GROUP_W), BF16)]
    scratch = [pltpu.VMEM((t, 2 * GROUP_W), F32), pltpu.VMEM((tc, 2 * GROUP_W), F32),
               pltpu.VMEM((t, GROUP_W), F32), pltpu.VMEM((t, GROUP_W), F32)]
    if ctx_out:
        out_specs.append(pl.BlockSpec((1, tc, GROUP_W), lambda i: (i, 0, 0)))
        out_shape.append(jax.ShapeDtypeStruct((b, tc, GROUP_W), BF16))
        scratch += [pltpu.VMEM((tc, GROUP_W), F32), pltpu.VMEM((tc, GROUP_W), F32)]
    scratch += [pltpu.VMEM((2 * N_PAIRS, LANES, 2 * LANES), F32)]
    res = pl.pallas_call(
        functools.partial(_mlstm_kernel, ctx_out=ctx_out),
        grid=(b,),
        in_specs=in_specs, out_specs=out_specs, out_shape=out_shape, scratch_shapes=scratch,
        compiler_params=_cparams(("parallel",)),
        name="mlstm",
    )(p, p, p, p, pc, pc, pc, pc, conv_w, gate_b, sel_head, sel_pair)
    return (res[0], res[1]) if ctx_out else (res[0], None)


def _sgu_kernel(u_ref, v_ref, f_ref, ws_ref, bias_ref, dft_ref, o_ref, z_ref):
    t = u_ref.shape[1]
    lane_g = _iota((1, GROUP_W), 1) // HEAD_DIM

    def body(i, carry):
        r0 = pl.multiple_of(i * SGU_CHUNK, SGU_CHUNK)
        v = v_ref[0, pl.ds(r0, SGU_CHUNK), :]
        mu = jnp.mean(v, axis=-1, keepdims=True)
        vc = v - mu
        vn = vc * lax.rsqrt(jnp.mean(vc * vc, axis=-1, keepdims=True) + EPS)
        mixed = bias_ref[...]
        for g in range(N_HEADS):
            mixed = mixed + _bdot(ws_ref[g], jnp.where(lane_g == g, vn, 0.0))
        o_ref[0, pl.ds(r0, SGU_CHUNK), :] = (u_ref[0, pl.ds(r0, SGU_CHUNK), :] * mixed).astype(o_ref.dtype)
        z = _bdot(f_ref[0, pl.ds(r0, SGU_CHUNK), :], dft_ref[...])
        z_ref[0, 0, pl.ds(r0, SGU_CHUNK), :] = z[:, :GROUP_W].astype(z_ref.dtype)
        z_ref[1, 0, pl.ds(r0, SGU_CHUNK), :] = z[:, GROUP_W:].astype(z_ref.dtype)
        return carry

    lax.fori_loop(0, t // SGU_CHUNK, body, 0)


def _sgu_fourier1(p, ws, bias, dft_c):
    b, t, _ = p.shape

    def col(idx):
        return pl.BlockSpec((1, t, GROUP_W), lambda i: (i, 0, idx))

    full = lambda shape: pl.BlockSpec(shape, lambda i: tuple(0 for _ in shape))
    return pl.pallas_call(
        _sgu_kernel,
        grid=(b,),
        in_specs=[col(P_SGU_U), col(P_SGU_V), col(P_FOUR), full(ws.shape), full(bias.shape), full(dft_c.shape)],
        out_specs=[pl.BlockSpec((1, t, GROUP_W), lambda i: (i, 0, 0)),
                   pl.BlockSpec((2, 1, t, GROUP_W), lambda i: (0, i, 0, 0))],
        out_shape=[jax.ShapeDtypeStruct((b, t, GROUP_W), BF16), jax.ShapeDtypeStruct((2, b, t, GROUP_W), BF16)],
        compiler_params=_cparams(("parallel",)),
        name="sgu_fourier1",
    )(p, p, p, ws, bias, dft_c)


def _fourier2_kernel(ct_ref, st_ref, z_ref, o_ref, *, scale):
    for i in range(z_ref.shape[1]):
        acc = jnp.dot(ct_ref[...], z_ref[0, i], preferred_element_type=F32)
        acc -= jnp.dot(st_ref[...], z_ref[1, i], preferred_element_type=F32)
        o_ref[i] = (acc * scale).astype(o_ref.dtype)


def _fourier2(z, cos_t, sin_t):
    _, b, t, w = z.shape
    tm = min(t, 256)
    return pl.pallas_call(
        functools.partial(_fourier2_kernel, scale=float((t * HEAD_DIM) ** -0.5)),
        grid=(t // tm,),
        in_specs=[pl.BlockSpec((tm, t), lambda i: (i, 0)),
                  pl.BlockSpec((tm, t), lambda i: (i, 0)),
                  pl.BlockSpec((2, b, t, w), lambda i: (0, 0, 0, 0), pipeline_mode=pl.Buffered(1))],
        out_specs=pl.BlockSpec((b, tm, w), lambda i: (0, i, 0)),
        out_shape=jax.ShapeDtypeStruct((b, t, w), BF16),
        compiler_params=_cparams(("parallel",)),
        name="fourier2",
    )(cos_t, sin_t, z)


def _dft_tables(n):
    k = np.arange(n, dtype=np.int64)
    ang = 2.0 * np.pi * ((k[:, None] * k[None, :]) % n).astype(np.float64) / n
    return np.cos(ang).astype(np.float32), np.sin(ang).astype(np.float32)


def _channel_dft():
    c, s = _dft_tables(HEAD_DIM)
    eye = np.eye(N_HEADS, dtype=np.float32)
    return np.concatenate([np.kron(eye, c), np.kron(eye, s)], axis=1)


def _route_kernel(x_ref, mod_ref, g_ref, wr_ref, h_ref, mi_ref, mf_ref, cnt_ref, cnt_sc):
    first = (pl.program_id(0) == 0) & (pl.program_id(1) == 0)

    @pl.when(first)
    def _():
        cnt_sc[...] = jnp.zeros_like(cnt_sc)

    tm = x_ref.shape[1]
    h = _norm_mod(x_ref[0], g_ref[...], mod_ref[0, 3:4, :], mod_ref[0, 4:5, :])
    for j in range(D_MODEL // LANES):
        h_ref[pl.ds(j, tm, stride=SUBLANES), :] = h[:, j * LANES:(j + 1) * LANES]
    logits = _fdot(h, wr_ref[...])
    lane = _iota((tm, LANES), 1).astype(F32)
    lg = jnp.where(lane < N_EXPERTS, logits, -jnp.inf)
    m1 = jnp.max(lg, axis=1, keepdims=True)
    i1 = jnp.min(jnp.where(lg == m1, lane, float(LANES)), axis=1, keepdims=True)
    lg2 = jnp.where(lane == i1, -jnp.inf, lg)
    m2 = jnp.max(lg2, axis=1, keepdims=True)
    i2 = jnp.min(jnp.where(lg2 == m2, lane, float(LANES)), axis=1, keepdims=True)
    e = jnp.exp(m2 - m1)
    g0 = 1.0 / (1.0 + e)
    g1 = e / (1.0 + e)
    onehot = jnp.where((lane == i1) | (lane == i2), 1.0, 0.0)
    strict = jnp.where(_iota((tm, tm), 1) < _iota((tm, tm), 0), 1.0, 0.0)
    before = _bdot(strict, onehot) + cnt_sc[...]
    r0 = jnp.sum(jnp.where(lane == i1, before, 0.0), axis=1, keepdims=True)
    r1 = jnp.sum(jnp.where(lane == i2, before, 0.0), axis=1, keepdims=True)
    cnt_sc[...] += jnp.sum(onehot, axis=0, keepdims=True)
    cnt_ref[...] = cnt_sc[...]
    l8 = _iota((tm, SUBLANES), 1)
    mi = jnp.where(l8 == 0, i1, jnp.where(l8 == 1, i2, jnp.where(l8 == 2, r0, r1)))
    mi_ref[...] = mi.astype(jnp.int32)
    mf_ref[...] = jnp.where(l8 == 0, g0, g1)


def _route(x, mod, g, w_router):
    b, t, d = x.shape
    m = b * t
    tm = 512
    nt = t // tm
    wr = jnp.zeros((d, LANES), F32).at[:, :N_EXPERTS].set(w_router)
    return pl.pallas_call(
        _route_kernel,
        grid=(b, nt),
        in_specs=[pl.BlockSpec((1, tm, d), lambda i, j: (i, j, 0)),
                  pl.BlockSpec((1, 6, d), lambda i, j: (i, 0, 0)),
                  pl.BlockSpec((1, d), lambda i, j: (0, 0)),
                  pl.BlockSpec((d, LANES), lambda i, j: (0, 0))],
        out_specs=[pl.BlockSpec((tm * SUBLANES, LANES), lambda i, j: (i * nt + j, 0)),
                   pl.BlockSpec((tm, SUBLANES), lambda i, j: (i * nt + j, 0)),
                   pl.BlockSpec((tm, SUBLANES), lambda i, j: (i * nt + j, 0)),
                   pl.BlockSpec((1, LANES), lambda i, j: (0, 0))],
        out_shape=[jax.ShapeDtypeStruct((m * SUBLANES, LANES), F32),
                   jax.ShapeDtypeStruct((m, SUBLANES), jnp.int32),
                   jax.ShapeDtypeStruct((m, SUBLANES), F32),
                   jax.ShapeDtypeStruct((1, LANES), F32)],
        scratch_shapes=[pltpu.VMEM((1, LANES), F32)],
        compiler_params=_cparams(("arbitrary", "arbitrary")),
        name="route",
    )(x, mod, g.reshape(1, d), wr)


def _moe_kernel(texp_ref, trows_ref, pos0_ref, pos1_ref, h_hbm, wg_ref, wu_ref, wd_ref, y_hbm,
                order_sm, xbuf, x_sc, acc_sc, ybuf, gsem, ssem):
    i = pl.program_id(0)
    j = pl.program_id(1)
    nf = pl.num_programs(1)
    tm = MOE_TM
    n_tok = pos0_ref.shape[0]
    rows = trows_ref[i]

    @pl.when((i == 0) & (j == 0))
    def _():
        def inv_body(t, carry):
            order_sm[pos0_ref[t]] = 2 * t
            order_sm[pos1_ref[t]] = 2 * t + 1
            return carry

        lax.fori_loop(0, n_tok, inv_body, 0, unroll=8)
        xbuf[...] = jnp.zeros_like(xbuf)

    def row_copy(r, gather):
        flat = order_sm[i * tm + r]
        off = pl.multiple_of(r * SUBLANES, SUBLANES)
        if gather:
            return pltpu.make_async_copy(h_hbm.at[jnp.right_shift(flat, 1)], xbuf.at[pl.ds(off, SUBLANES), :], gsem)
        return pltpu.make_async_copy(ybuf.at[pl.ds(off, SUBLANES), :], y_hbm.at[flat], ssem)

    @pl.when((j == 0) & (rows > 0))
    def _():
        def start(r, carry):
            row_copy(r, True).start()
            return carry

        lax.fori_loop(0, rows, start, 0)

        def wait(r, carry):
            row_copy(r, True).wait()
            return carry

        lax.fori_loop(0, rows, wait, 0)
        for c in range(D_MODEL // LANES):
            x_sc[:, c * LANES:(c + 1) * LANES] = xbuf[pl.ds(c, tm, stride=SUBLANES), :].astype(BF16)
        acc_sc[...] = jnp.zeros_like(acc_sc)

    @pl.when(rows > 0)
    def _():
        x = x_sc[...]
        gate = jnp.dot(x, wg_ref[0].astype(BF16), preferred_element_type=F32)
        up = jnp.dot(x, wu_ref[0].astype(BF16), preferred_element_type=F32)
        acc_sc[...] += jnp.dot((_silu(gate) * up).astype(BF16), wd_ref[0].astype(BF16),
                               preferred_element_type=F32)

    @pl.when((j == nf - 1) & (rows > 0))
    def _():
        for c in range(D_MODEL // LANES):
            ybuf[pl.ds(c, tm, stride=SUBLANES), :] = acc_sc[:, c * LANES:(c + 1) * LANES]

        def start(r, carry):
            row_copy(r, False).start()
            return carry

        lax.fori_loop(0, rows, start, 0)

        def wait(r, carry):
            row_copy(r, False).wait()
            return carry

        lax.fori_loop(0, rows, wait, 0)


def _moe(h_tiles, pos0, pos1, tile_expert, tile_rows, wg, wu, wd):
    m = h_tiles.shape[0]
    n_exp, d, f = wg.shape
    nt = tile_expert.shape[0]
    tm, tf = MOE_TM, MOE_TF
    nf = f // tf

    def col_tile(i, j, tr):
        return jnp.where(tr[i] > 0, j, nf - 1)

    grid_spec = pltpu.PrefetchScalarGridSpec(
        num_scalar_prefetch=4,
        grid=(nt, nf),
        in_specs=[pl.BlockSpec(memory_space=pl.ANY),
                  pl.BlockSpec((1, d, tf), lambda i, j, te, tr, p0, p1: (te[i], 0, col_tile(i, j, tr))),
                  pl.BlockSpec((1, d, tf), lambda i, j, te, tr, p0, p1: (te[i], 0, col_tile(i, j, tr))),
                  pl.BlockSpec((1, tf, d), lambda i, j, te, tr, p0, p1: (te[i], col_tile(i, j, tr), 0))],
        out_specs=pl.BlockSpec(memory_space=pl.ANY),
        scratch_shapes=[pltpu.SMEM((nt * tm,), jnp.int32),
                        pltpu.VMEM((tm * SUBLANES, LANES), F32),
                        pltpu.VMEM((tm, d), BF16),
                        pltpu.VMEM((tm, d), F32),
                        pltpu.VMEM((tm * SUBLANES, LANES), F32),
                        pltpu.SemaphoreType.DMA(()),
                        pltpu.SemaphoreType.DMA(())],
    )
    return pl.pallas_call(
        _moe_kernel,
        grid_spec=grid_spec,
        out_shape=jax.ShapeDtypeStruct((TOP_K * m, SUBLANES, LANES), F32),
        compiler_params=_cparams(("arbitrary", "arbitrary")),
        name="moe",
    )(tile_expert, tile_rows, pos0, pos1, h_tiles, wg, wu, wd)


def _combine_kernel(x_ref, mod_ref, y_ref, mf_ref, fg_ref, o_ref):
    tm = x_ref.shape[1]
    stride = TOP_K * SUBLANES
    g0 = mf_ref[:, 0:1]
    g1 = mf_ref[:, 1:2]
    parts = []
    for c in range(D_MODEL // LANES):
        y0 = y_ref[pl.ds(c, tm, stride=stride), :]
        y1 = y_ref[pl.ds(SUBLANES + c, tm, stride=stride), :]
        parts.append(g0 * y0 + g1 * y1)
    moe = jnp.concatenate(parts, axis=1)
    x = x_ref[0] + mod_ref[0, 5:6, :] * moe
    ms = jnp.mean(x * x, axis=-1, keepdims=True)
    o_ref[0] = x * lax.rsqrt(ms + EPS) * fg_ref[...]


def _combine(x, mod, y, mf, final_g):
    b, t, d = x.shape
    tm = 256
    nt = t // tm
    rows = tm * TOP_K * SUBLANES
    return pl.pallas_call(
        _combine_kernel,
        grid=(b, nt),
        in_specs=[pl.BlockSpec((1, tm, d), lambda i, j: (i, j, 0)),
                  pl.BlockSpec((1, 6, d), lambda i, j: (i, 0, 0)),
                  pl.BlockSpec((rows, LANES), lambda i, j: (i * nt + j, 0)),
                  pl.BlockSpec((tm, SUBLANES), lambda i, j: (i * nt + j, 0)),
                  pl.BlockSpec((1, d), lambda i, j: (0, 0))],
        out_specs=pl.BlockSpec((1, tm, d), lambda i, j: (i, j, 0)),
        out_shape=jax.ShapeDtypeStruct((b, t, d), F32),
        compiler_params=_cparams(("parallel", "parallel")),
        name="combine",
    )(x, mod, y, mf, final_g.reshape(1, d))


def _moe_layer(x, mod, norm_g, w_router, wg, wu, wd, final_g):
    b, t, d = x.shape
    m = b * t
    h_tiles, mi, mf, cnt = _route(x, mod, norm_g, w_router)
    counts = cnt[0, :N_EXPERTS].astype(jnp.int32)
    tiles_per = (counts + MOE_TM - 1) // MOE_TM
    tile_end = jnp.cumsum(tiles_per)
    tile_start = tile_end - tiles_per
    seg_start = tile_start * MOE_TM
    nt = (TOP_K * m) // MOE_TM + N_EXPERTS
    tile_id = jnp.arange(nt, dtype=jnp.int32)
    used_id = jnp.minimum(tile_id, tile_end[-1] - 1)
    tile_expert = jnp.sum(used_id[:, None] >= tile_end[None, :], axis=1).astype(jnp.int32)
    tile_rows = jnp.clip(counts[tile_expert] - (tile_id - tile_start[tile_expert]) * MOE_TM, 0, MOE_TM)
    tile_rows = jnp.where(tile_id < tile_end[-1], tile_rows, 0).astype(jnp.int32)
    pos0 = seg_start[mi[:, 0]] + mi[:, 2]
    pos1 = seg_start[mi[:, 1]] + mi[:, 3]
    y = _moe(h_tiles.reshape(m, SUBLANES, LANES), pos0, pos1, tile_expert, tile_rows, wg, wu, wd)
    return _combine(x, mod, y.reshape(-1, LANES), mf, final_g)


def _permute_w_in(w):
    d = w.shape[0]
    return jnp.concatenate([w[:, 0:768], w[:, 1312:1568], w[:, 800:1312], w[:, 1568:1824], w[:, 1840:2608],
                            w[:, 768:800], w[:, 1824:1840], jnp.zeros((d, P_WIDTH - 2608), w.dtype)],
                           axis=1).astype(BF16)


def _mixers(p, pc, gla_w_lr, gla_b_lr, gla_norm_g, mlstm_conv, mlstm_gate_b, sgu_w, sgu_b, ctx_out, tables):
    wlr = jnp.zeros((LANES, 2 * GLA_QK_W), F32)
    wlr = wlr.at[0:GLA_RANK, 0:GLA_QK_W].set(gla_w_lr[0]).at[GLA_RANK:2 * GLA_RANK, GLA_QK_W:].set(gla_w_lr[1])
    blr = gla_b_lr.reshape(1, 2 * GLA_QK_W)
    gla, gla_c = _gla(p, pc, wlr, blr, gla_norm_g.reshape(1, GROUP_W), ctx_out)
    conv_w = mlstm_conv.reshape(9, 2 * GROUP_W)
    gate_b = jnp.zeros((1, LANES), F32).at[0, SMALL_GATE_LANE:SMALL_GATE_LANE + 4 * N_HEADS].set(
        mlstm_gate_b.reshape(-1))
    ml, ml_c = _mlstm(p, pc, conv_w, gate_b, ctx_out)
    bias = jnp.repeat(sgu_b.T, HEAD_DIM, axis=1)
    dft_c, tabs = tables
    sg, z = _sgu_fourier1(p, sgu_w, bias, dft_c)
    fo = _fourier2(z, *tabs[p.shape[1]])
    mix = (gla, ml, sg, fo)
    if not ctx_out:
        return mix, None
    sg_c, z_c = _sgu_fourier1(pc, sgu_w, bias, dft_c)
    fo_c = _fourier2(z_c, *tabs[pc.shape[1]])
    return mix, (gla_c, ml_c, sg_c, fo_c)


def kernel(x, c, ctx, c_ctx, w_ada, b_ada, norm_mix_g, norm_ffn_g, w_in, w_out, gla_w_lr, gla_b_lr, gla_norm_g,
           mlstm_conv, mlstm_gate_b, sgu_w, sgu_b, ffn_w_gate, ffn_w_up, ffn_w_down, moe_router, moe_w_gate,
           moe_w_up, moe_w_down, final_norm_g):
    depth = w_ada.shape[0]
    assert depth == 2, "layer 0 is the dense layer with context output, layer 1 the MoE layer"
    b, t, d = x.shape
    tc = ctx.shape[1]
    cvec = jnp.zeros((16, d), F32).at[:b].set(c).at[b].set(c_ctx)
    mods = _ada(cvec, w_ada, b_ada).reshape(depth, 16, 6, d)
    tables = (jnp.asarray(_channel_dft()).astype(BF16),
              {n: tuple(jnp.asarray(a).astype(BF16) for a in _dft_tables(n)) for n in {t, tc}})
    xc = ctx
    for l in range(depth):
        ctx_out = l < depth - 1
        mod, mod_c = mods[l, :b], mods[l, b:b + 1]
        w_in_l = _permute_w_in(w_in[l])
        w_out_l = w_out[l].astype(BF16)
        p = _in_proj(x, mod, norm_mix_g[l], w_in_l)
        pc = _in_proj(xc, mod_c, norm_mix_g[l], w_in_l)
        mix, mix_c = _mixers(p, pc, gla_w_lr[l], gla_b_lr[l], gla_norm_g[l], mlstm_conv[l], mlstm_gate_b[l],
                             sgu_w[l], sgu_b[l], ctx_out, tables)
        x = _out_proj(x, mod, mix, w_out_l)
        if ctx_out:
            xc = _out_proj(xc, mod_c, mix_c, w_out_l)
        i = l // 2
        if l % 2 == 0:
            wg, wu, wd = ffn_w_gate[i].astype(BF16), ffn_w_up[i].astype(BF16), ffn_w_down[i].astype(BF16)
            x = _ffn(x, mod, norm_ffn_g[l], wg, wu, wd)
            if ctx_out:
                xc = _ffn(xc, mod_c, norm_ffn_g[l], wg, wu, wd)
        else:
            x = _moe_layer(x, mod, norm_ffn_g[l], moe_router[i], moe_w_gate[i], moe_w_up[i], moe_w_down[i],
                           final_norm_g)
    return x
```

```python
import functools

import numpy as np
import jax
import jax.numpy as jnp
from jax import lax
from jax.experimental import pallas as pl
from jax.experimental.pallas import tpu as pltpu

F32 = jnp.float32
BF16 = jnp.bfloat16
HIGHEST = lax.Precision.HIGHEST

D_MODEL = 1024
GROUP_W = 256
N_HEADS = 4
HEAD_DIM = 64
GLA_DK = 32
GLA_QK_W = N_HEADS * GLA_DK
GLA_RANK = 16
GLA_NORMALIZER = 16.0
GLA_BLOCK = 128
GRID_W = 64
SGU_CHUNK = 128
N_EXPERTS = 8
TOP_K = 2
EPS = 1e-6

LANES = 128
SUBLANES = 8
VMEM_LIMIT = 52 * 1024 * 1024

P_GLA_QK, P_GLA_V, P_GLA_G, P_ML_V = 0, 1, 2, 3
P_ML_QK = 2
P_ML_OG, P_SGU_U, P_SGU_V, P_FOUR = 6, 7, 8, 9
P_SMALL = 20
P_WIDTH = 2688
SMALL_GATE_LANE = 32

MOE_TM = 1024
MOE_TF = 512
MOE_COPY_ROWS = 256


def _cparams(sem, **kw):
    return pltpu.CompilerParams(dimension_semantics=sem, vmem_limit_bytes=VMEM_LIMIT, **kw)


def _sigmoid(x):
    return 1.0 / (1.0 + jnp.exp(-x))


def _silu(x):
    return x * _sigmoid(x)


def _log_sigmoid(x):
    return jnp.minimum(x, 0.0) - jnp.log(1.0 + jnp.exp(-jnp.abs(x)))


def _norm_mod(x, g, shift, scale):
    ms = jnp.mean(x * x, axis=-1, keepdims=True)
    return (x * lax.rsqrt(ms + EPS) * g) * (1.0 + scale) + shift


def _bdot(a, b):
    return jnp.dot(a.astype(BF16), b.astype(BF16), preferred_element_type=F32)


def _bdot_nt(a, b):
    return lax.dot_general(a.astype(BF16), b.astype(BF16), (((1,), (1,)), ((), ())), preferred_element_type=F32)


def _fdot(a, b):
    return jnp.dot(a, b, precision=HIGHEST, preferred_element_type=F32)


def _split3(x):
    hi = x.astype(BF16)
    r = x - hi.astype(F32)
    mid = r.astype(BF16)
    lo = (r - mid.astype(F32)).astype(BF16)
    return hi, mid, lo


def _dot3(a, b):
    a_hi, a_lo, _ = _split3(a)
    b_hi, b_lo, _ = _split3(b)
    dot = functools.partial(jnp.dot, preferred_element_type=F32)
    return dot(a_hi, b_hi) + dot(a_lo, b_hi) + dot(a_hi, b_lo)


def _tri_dot(tri, x):
    n = x.shape[1]
    y = jnp.dot(tri.astype(BF16), jnp.concatenate(_split3(x), axis=1), preferred_element_type=F32)
    return y[:, :n] + y[:, n:2 * n] + y[:, 2 * n:]


def _ada_kernel(c_ref, w_ref, b_ref, o_ref):
    o_ref[0] = _fdot(_silu(c_ref[...]), w_ref[0]) + b_ref[0]


def _ada(cvec, w_ada, b_ada):
    depth, d, n = w_ada.shape
    rows = cvec.shape[0]
    tn = 1536
    return pl.pallas_call(
        _ada_kernel,
        grid=(depth, n // tn),
        in_specs=[pl.BlockSpec((rows, d), lambda l, j: (0, 0)),
                  pl.BlockSpec((1, d, tn), lambda l, j: (l, 0, j)),
                  pl.BlockSpec((1, 1, tn), lambda l, j: (l, 0, j))],
        out_specs=pl.BlockSpec((1, rows, tn), lambda l, j: (l, 0, j)),
        out_shape=jax.ShapeDtypeStruct((depth, rows, n), F32),
        compiler_params=_cparams(("parallel", "parallel")),
        name="ada",
    )(cvec, w_ada, b_ada.reshape(depth, 1, n))


def _in_kernel(x_ref, mod_ref, g_ref, w_ref, o_ref):
    h = _norm_mod(x_ref[0], g_ref[...], mod_ref[0, 0:1, :], mod_ref[0, 1:2, :])
    o_ref[0] = jnp.dot(h.astype(BF16), w_ref[...], preferred_element_type=F32)


def _in_proj(x, mod, g, w):
    b, t, d = x.shape
    n = w.shape[1]
    tm = min(t, 512)
    mod_map = (lambda i, j: (i, 0, 0)) if mod.shape[0] == b else (lambda i, j: (0, 0, 0))
    return pl.pallas_call(
        _in_kernel,
        grid=(b, t // tm),
        in_specs=[pl.BlockSpec((1, tm, d), lambda i, j: (i, j, 0)),
                  pl.BlockSpec((1, 6, d), mod_map),
                  pl.BlockSpec((1, d), lambda i, j: (0, 0)),
                  pl.BlockSpec((d, n), lambda i, j: (0, 0))],
        out_specs=pl.BlockSpec((1, tm, n), lambda i, j: (i, j, 0)),
        out_shape=jax.ShapeDtypeStruct((b, t, n), F32),
        compiler_params=_cparams(("parallel", "parallel")),
        name="in_proj",
    )(x, mod, g.reshape(1, d), w)


def _out_kernel(x_ref, mod_ref, a_ref, b_ref, c_ref, d_ref, w_ref, o_ref):
    acc = jnp.dot(a_ref[0], w_ref[0:GROUP_W, :], preferred_element_type=F32)
    acc += jnp.dot(b_ref[0], w_ref[GROUP_W:2 * GROUP_W, :], preferred_element_type=F32)
    acc += jnp.dot(c_ref[0], w_ref[2 * GROUP_W:3 * GROUP_W, :], preferred_element_type=F32)
    acc += jnp.dot(d_ref[0], w_ref[3 * GROUP_W:4 * GROUP_W, :], preferred_element_type=F32)
    o_ref[0] = x_ref[0] + mod_ref[0, 2:3, :] * acc


def _out_proj(x, mod, mixes, w):
    b, t, d = x.shape
    tm = min(t, 512)
    mod_map = (lambda i, j: (i, 0, 0)) if mod.shape[0] == b else (lambda i, j: (0, 0, 0))
    mix_spec = pl.BlockSpec((1, tm, GROUP_W), lambda i, j: (i, j, 0))
    return pl.pallas_call(
        _out_kernel,
        grid=(b, t // tm),
        in_specs=[pl.BlockSpec((1, tm, d), lambda i, j: (i, j, 0)),
                  pl.BlockSpec((1, 6, d), mod_map),
                  mix_spec, mix_spec, mix_spec, mix_spec,
                  pl.BlockSpec((d, d), lambda i, j: (0, 0))],
        out_specs=pl.BlockSpec((1, tm, d), lambda i, j: (i, j, 0)),
        out_shape=jax.ShapeDtypeStruct((b, t, d), F32),
        compiler_params=_cparams(("parallel", "parallel")),
        name="out_proj",
    )(x, mod, *mixes, w)


def _ffn_kernel(x_ref, mod_ref, g_ref, wg_ref, wu_ref, wd_ref, o_ref, h_sc, acc_sc):
    j = pl.program_id(2)

    @pl.when(j == 0)
    def _():
        h = _norm_mod(x_ref[0], g_ref[...], mod_ref[0, 3:4, :], mod_ref[0, 4:5, :])
        h_sc[...] = h.astype(BF16)
        acc_sc[...] = jnp.zeros_like(acc_sc)

    h = h_sc[...]
    gate = jnp.dot(h, wg_ref[...], preferred_element_type=F32)
    up = jnp.dot(h, wu_ref[...], preferred_element_type=F32)
    acc_sc[...] += jnp.dot((_silu(gate) * up).astype(BF16), wd_ref[...], preferred_element_type=F32)

    @pl.when(j == pl.num_programs(2) - 1)
    def _():
        o_ref[0] = x_ref[0] + mod_ref[0, 5:6, :] * acc_sc[...]


def _ffn(x, mod, g, wg, wu, wd):
    b, t, d = x.shape
    f = wg.shape[1]
    tm = min(t, 512)
    tf = f // 2
    mod_map = (lambda i, j, k: (i, 0, 0)) if mod.shape[0] == b else (lambda i, j, k: (0, 0, 0))
    return pl.pallas_call(
        _ffn_kernel,
        grid=(b, t // tm, f // tf),
        in_specs=[pl.BlockSpec((1, tm, d), lambda i, j, k: (i, j, 0)),
                  pl.BlockSpec((1, 6, d), mod_map),
                  pl.BlockSpec((1, d), lambda i, j, k: (0, 0)),
                  pl.BlockSpec((d, tf), lambda i, j, k: (0, k)),
                  pl.BlockSpec((d, tf), lambda i, j, k: (0, k)),
                  pl.BlockSpec((tf, d), lambda i, j, k: (k, 0))],
        out_specs=pl.BlockSpec((1, tm, d), lambda i, j, k: (i, j, 0)),
        out_shape=jax.ShapeDtypeStruct((b, t, d), F32),
        scratch_shapes=[pltpu.VMEM((tm, d), BF16), pltpu.VMEM((tm, d), F32)],
        compiler_params=_cparams(("parallel", "parallel", "arbitrary")),
        name="ffn",
    )(x, mod, g.reshape(1, d), wg, wu, wd)


def _iota(shape, dim):
    return lax.broadcasted_iota(jnp.int32, shape, dim)


def _gla_chunk(q, k, v, la, s_ref, lower, want_out):
    L = q.shape[0]
    r, c = _iota((L, L), 0), _iota((L, L), 1)
    keep = (c <= r) if lower else (c >= r)
    cum = _tri_dot(jnp.where(keep, 1.0, 0.0), la)
    tot_row = jnp.sum(la, axis=0, keepdims=True)
    s_prev = s_ref[...]
    out = None
    if want_out:
        mid = L // 2 - 1 if lower else L // 2
        ref = cum[mid:mid + 1, :]
        qr = q * jnp.exp(cum - ref)
        kc = (k * jnp.exp(ref - cum)).astype(BF16)
        lane_q = _iota((1, GLA_QK_W), 1) // GLA_DK
        lane_v = _iota((1, GROUP_W), 1) // HEAD_DIM
        out = _bdot_nt(q * jnp.exp(cum), s_prev)
        for h in range(N_HEADS):
            sc = jnp.where(keep, _bdot_nt(jnp.where(lane_q == h, qr, 0.0), kc), 0.0)
            out = out + _bdot(sc, jnp.where(lane_v == h, v, 0.0))
    ke = k * jnp.exp(tot_row - cum)
    upd = jnp.dot(v.T.astype(BF16), ke.astype(BF16), preferred_element_type=F32)
    bd = (_iota((GROUP_W, GLA_QK_W), 0) // HEAD_DIM) == (_iota((GROUP_W, GLA_QK_W), 1) // GLA_DK)
    s_ref[...] = jnp.exp(tot_row) * s_prev + jnp.where(bd, upd, 0.0)
    return out


def _gla_kernel(qk_ref, v_ref, g_ref, sm_ref, qkc_ref, vc_ref, gc_ref, smc_ref, wlr_ref, blr_ref, ng_ref,
                *rest, ctx_out):
    if ctx_out:
        o_ref, oc_ref, la_sc, lac_sc, of_sc, ob_sc, ofc_sc, obc_sc, sf_sc, sb_sc = rest
    else:
        o_ref, la_sc, lac_sc, of_sc, ob_sc, sf_sc, sb_sc = rest
        oc_ref = ofc_sc = obc_sc = None
    t = qk_ref.shape[1]
    tc = qkc_ref.shape[1]
    qscale = GLA_DK ** -0.5

    def log_decay(sm):
        z = _dot3(sm, wlr_ref[...]) + blr_ref[...]
        return _log_sigmoid(z) * (1.0 / GLA_NORMALIZER)

    lac_sc[...] = log_decay(smc_ref[0])
    rows_blk = 256

    def la_body(i, carry):
        r0 = pl.multiple_of(i * rows_blk, rows_blk)
        la_sc[pl.ds(r0, rows_blk), :] = log_decay(sm_ref[0, pl.ds(r0, rows_blk), :])
        return carry

    lax.fori_loop(0, t // rows_blk, la_body, 0)
    sf_sc[...] = jnp.zeros_like(sf_sc)
    sb_sc[...] = jnp.zeros_like(sb_sc)

    def run(n_rows, qk, v, la, of, ob, want_out):
        n = n_rows // GLA_BLOCK

        def body(i, carry):
            rf = pl.multiple_of(i * GLA_BLOCK, GLA_BLOCK)
            rb = pl.multiple_of((n - 1 - i) * GLA_BLOCK, GLA_BLOCK)
            for r0, s_ref, lower, dst, lane0 in ((rf, sf_sc, True, of, 0), (rb, sb_sc, False, ob, GLA_QK_W)):
                qkb = qk[0, pl.ds(r0, GLA_BLOCK), :]
                out = _gla_chunk(qkb[:, :GLA_QK_W] * qscale, qkb[:, GLA_QK_W:], v[0, pl.ds(r0, GLA_BLOCK), :],
                                 la[pl.ds(r0, GLA_BLOCK), lane0:lane0 + GLA_QK_W], s_ref, lower, want_out)
                if want_out:
                    dst[pl.ds(r0, GLA_BLOCK), :] = out
            return carry

        lax.fori_loop(0, n, body, 0)

    run(tc, qkc_ref, vc_ref, lac_sc, ofc_sc, obc_sc, ctx_out)
    run(t, qk_ref, v_ref, la_sc, of_sc, ob_sc, True)

    grp = jnp.where((_iota((GROUP_W, GROUP_W), 0) // HEAD_DIM) == (_iota((GROUP_W, GROUP_W), 1) // HEAD_DIM),
                    1.0 / HEAD_DIM, 0.0).astype(BF16)

    def finish(n_rows, of, ob, g, dst):
        blk = min(n_rows, 256)

        def body(i, carry):
            r0 = pl.multiple_of(i * blk, blk)
            o = of[pl.ds(r0, blk), :] + ob[pl.ds(r0, blk), :]
            sq_hi, sq_lo, _ = _split3(o * o)
            ms = (jnp.dot(sq_hi, grp, preferred_element_type=F32) + jnp.dot(sq_lo, grp, preferred_element_type=F32))
            res = o * lax.rsqrt(ms + EPS) * ng_ref[...] * _silu(g[0, pl.ds(r0, blk), :])
            dst[0, pl.ds(r0, blk), :] = res.astype(dst.dtype)
            return carry

        lax.fori_loop(0, n_rows // blk, body, 0)

    finish(t, of_sc, ob_sc, g_ref, o_ref)
    if ctx_out:
        finish(tc, ofc_sc, obc_sc, gc_ref, oc_ref)


def _gla(p, pc, wlr, blr, ng, ctx_out):
    b, t, _ = p.shape
    tc = pc.shape[1]

    def col(width, idx, rows):
        return pl.BlockSpec((1, rows, width), lambda i: (i, 0, idx))

    full = lambda shape: pl.BlockSpec(shape, lambda i: tuple(0 for _ in shape))
    in_specs = [col(GROUP_W, P_GLA_QK, t), col(GROUP_W, P_GLA_V, t), col(GROUP_W, P_GLA_G, t), col(LANES, P_SMALL, t),
                col(GROUP_W, P_GLA_QK, tc), col(GROUP_W, P_GLA_V, tc), col(GROUP_W, P_GLA_G, tc),
                col(LANES, P_SMALL, tc),
                full(wlr.shape), full(blr.shape), full(ng.shape)]
    out_specs = [pl.BlockSpec((1, t, GROUP_W), lambda i: (i, 0, 0))]
    out_shape = [jax.ShapeDtypeStruct((b, t, GROUP_W), BF16)]
    scratch = [pltpu.VMEM((t, 2 * GLA_QK_W), F32), pltpu.VMEM((tc, 2 * GLA_QK_W), F32),
               pltpu.VMEM((t, GROUP_W), F32), pltpu.VMEM((t, GROUP_W), F32)]
    if ctx_out:
        out_specs.append(pl.BlockSpec((1, tc, GROUP_W), lambda i: (i, 0, 0)))
        out_shape.append(jax.ShapeDtypeStruct((b, tc, GROUP_W), BF16))
        scratch += [pltpu.VMEM((tc, GROUP_W), F32), pltpu.VMEM((tc, GROUP_W), F32)]
    scratch += [pltpu.VMEM((GROUP_W, GLA_QK_W), F32), pltpu.VMEM((GROUP_W, GLA_QK_W), F32)]
    res = pl.pallas_call(
        functools.partial(_gla_kernel, ctx_out=ctx_out),
        grid=(b,),
        in_specs=in_specs, out_specs=out_specs, out_shape=out_shape, scratch_shapes=scratch,
        compiler_params=_cparams(("parallel",)),
        name="gla",
    )(p, p, p, p, pc, pc, pc, pc, wlr, blr, ng)
    return (res[0], res[1]) if ctx_out else (res[0], None)


ML_BLOCK = 128
N_PAIRS = N_HEADS // 2


def _gate_lane(d, h):
    return SMALL_GATE_LANE + 8 * d + h


def _mlstm_selectors():
    sel_head = np.zeros((LANES, 2 * N_HEADS * LANES), np.float32)
    sel_pair = np.zeros((LANES, 2 * N_PAIRS * LANES), np.float32)
    for d in range(2):
        for h in range(N_HEADS):
            sel_head[_gate_lane(d, h), (d * N_HEADS + h) * LANES:(d * N_HEADS + h + 1) * LANES] = 1.0
            p, hh = divmod(h, 2)
            lo = (d * N_PAIRS + p) * LANES + hh * HEAD_DIM
            sel_pair[_gate_lane(d, h), lo:lo + HEAD_DIM] = 1.0
    return sel_head, sel_pair


def _mlstm_block(blk_f, blk_b, cn_ref, m_prev, selh_ref, selp_ref, want_out):
    L = ML_BLOCK
    lane = _iota((1, LANES), 1)
    is_b = (lane >= _gate_lane(1, 0)) & (lane < _gate_lane(1, 0) + N_HEADS)
    valid = ((lane >= _gate_lane(0, 0)) & (lane < _gate_lane(0, 0) + N_HEADS)) | is_b
    r, c = _iota((L, L), 0), _iota((L, L), 1)
    keep_f, keep_b = c <= r, c >= r
    lf_f = pltpu.roll(_log_sigmoid(blk_f[2]), LANES - 4, 1)
    lf_b = pltpu.roll(_log_sigmoid(blk_b[2]), LANES - 4, 1)
    cum_f = _tri_dot(jnp.where(keep_f, 1.0, 0.0), lf_f)
    cum_b = _tri_dot(jnp.where(keep_b, 1.0, 0.0), lf_b)
    cum = jnp.where(valid, jnp.where(is_b, cum_b, cum_f), 0.0)
    ig = jnp.where(valid, jnp.where(is_b, blk_b[2], blk_f[2]), 0.0)
    tot = jnp.where(valid, jnp.where(is_b, cum_b[0:1, :], cum_f[L - 1:L, :]), 0.0)
    b = ig - cum
    row = _iota((L, 1), 0)
    pm = b
    k = 1
    while k < L:
        from_prev = jnp.where(row >= k, pltpu.roll(pm, k, 0), -jnp.inf)
        from_next = jnp.where(row < L - k, pltpu.roll(pm, L - k, 0), -jnp.inf)
        pm = jnp.maximum(pm, jnp.where(is_b, from_next, from_prev))
        k *= 2
    mm = jnp.maximum(m_prev, pm)
    w_end = tot + b
    m_chunk = jnp.max(w_end, axis=0, keepdims=True)
    m_new = jnp.maximum(tot + m_prev, m_chunk)
    a_row = jnp.exp(tot + m_prev - m_new)
    g_row = jnp.exp(m_chunk - m_new)
    cols = [jnp.exp(w_end - m_chunk)]
    if want_out:
        cols += [jnp.exp(m_prev - mm), jnp.exp(-(cum + mm))]
        mm_hi, mm_lo, _ = _split3(mm)
        mm_t = (jnp.dot(mm_hi, selh_ref[...], preferred_element_type=F32)
                + jnp.dot(mm_lo, selh_ref[...], preferred_element_type=F32))
        b_rows = b.T
    pair_t = _bdot(jnp.concatenate(cols, axis=0), selp_ref[...])
    lo_half = lane < HEAD_DIM
    ones_t = jnp.ones((L, LANES), F32)
    bd = (_iota((LANES, 2 * LANES), 0) // HEAD_DIM) == ((_iota((LANES, 2 * LANES), 1) % LANES) // HEAD_DIM)
    outs = {}
    for d, (blk, keep) in enumerate(((blk_f, keep_f), (blk_b, keep_b))):
        qk, v, _ = blk
        for p in range(N_PAIRS):
            qp = qk[:, p * LANES:(p + 1) * LANES]
            kp = qk[:, GROUP_W + p * LANES:GROUP_W + (p + 1) * LANES]
            vp = v[:, p * LANES:(p + 1) * LANES]
            cn = cn_ref[d * N_PAIRS + p]
            tcol = (d * N_PAIRS + p) * LANES
            if want_out:
                nd = None
                for hh in range(2):
                    h = 2 * p + hh
                    mine = lo_half if hh == 0 else jnp.logical_not(lo_half)
                    gl = _gate_lane(d, h)
                    hcol = (d * N_HEADS + h) * LANES
                    e = jnp.where(keep, jnp.exp(b_rows[gl:gl + 1, :] - mm_t[:, hcol:hcol + LANES]), 0.0)
                    s = e * _bdot_nt(jnp.where(mine, qp, 0.0), kp)
                    va = jnp.concatenate([jnp.where(mine, vp, 0.0), jnp.where(mine, ones_t, 0.0)], axis=1)
                    t = _bdot(s, va)
                    nd = t if nd is None else nd + t
                w_int = pair_t[L:2 * L, tcol:tcol + LANES]
                nd = nd + jnp.concatenate([w_int, w_int], axis=1) * _bdot(qp, cn)
                outs[(d, p)] = nd[:, :LANES] / jnp.maximum(jnp.abs(nd[:, LANES:]),
                                                           pair_t[2 * L:3 * L, tcol:tcol + LANES])
            kw_t = (kp * pair_t[0:L, tcol:tcol + LANES]).T
            upd = _bdot(kw_t, jnp.concatenate([vp, ones_t], axis=1))
            ga, gb = _gate_lane(d, 2 * p), _gate_lane(d, 2 * p + 1)
            half = (HEAD_DIM, 2 * LANES)
            a_t = jnp.concatenate([jnp.broadcast_to(a_row[:, ga:ga + 1], half),
                                   jnp.broadcast_to(a_row[:, gb:gb + 1], half)], axis=0)
            g_t = jnp.concatenate([jnp.broadcast_to(g_row[:, ga:ga + 1], half),
                                   jnp.broadcast_to(g_row[:, gb:gb + 1], half)], axis=0)
            cn_ref[d * N_PAIRS + p] = a_t * cn + g_t * jnp.where(bd, upd, 0.0)
    return (outs if want_out else None), m_new


def _mlstm_kernel(qk_ref, v_ref, og_ref, sm_ref, qkc_ref, vc_ref, ogc_ref, smc_ref, cw_ref, gb_ref,
                  selh_ref, selp_ref, *rest, ctx_out):
    if ctx_out:
        o_ref, oc_ref, qc_sc, qcc_sc, hf_sc, hb_sc, hfc_sc, hbc_sc, cn_sc = rest
    else:
        o_ref, qc_sc, qcc_sc, hf_sc, hb_sc, cn_sc = rest
        oc_ref = hfc_sc = hbc_sc = None
    t = qk_ref.shape[1]
    tc = qkc_ref.shape[1]
    w2 = 2 * GROUP_W
    kscale = HEAD_DIM ** -0.5
    n_rows_grid = t // GRID_W

    def tap(dr, dc):
        return cw_ref[dr * 3 + dc:dr * 3 + dc + 1, :]

    def shifted(blk, dc):
        n = blk.shape[0]
        if dc == 1:
            return blk
        ridx = _iota((n, 1), 0)
        if dc == 0:
            return jnp.where(ridx == 0, 0.0, pltpu.roll(blk, 1, 0))
        return jnp.where(ridx == n - 1, 0.0, pltpu.roll(blk, n - 1, 0))

    def finish_qk(acc):
        a = _silu(acc)
        lane = _iota((1, w2), 1)
        return jnp.where(lane >= GROUP_W, a * kscale, a)

    def conv_body(i, carry):
        r0 = pl.multiple_of(i * GRID_W, GRID_W)
        r_up = pl.multiple_of(jnp.maximum(i - 1, 0) * GRID_W, GRID_W)
        r_dn = pl.multiple_of(jnp.minimum(i + 1, n_rows_grid - 1) * GRID_W, GRID_W)
        up = qk_ref[0, pl.ds(r_up, GRID_W), :] * jnp.where(i > 0, 1.0, 0.0)
        mid = qk_ref[0, pl.ds(r0, GRID_W), :]
        dn = qk_ref[0, pl.ds(r_dn, GRID_W), :] * jnp.where(i < n_rows_grid - 1, 1.0, 0.0)
        acc = jnp.zeros((GRID_W, w2), F32)
        for dr, blk in enumerate((up, mid, dn)):
            for dc in range(3):
                acc = acc + shifted(blk, dc) * tap(dr, dc)
        qc_sc[pl.ds(r0, GRID_W), :] = finish_qk(acc)
        return carry

    lax.fori_loop(0, n_rows_grid, conv_body, 0)
    xc = qkc_ref[0]
    acc = jnp.zeros((tc, w2), F32)
    for dc in range(3):
        acc = acc + shifted(xc, dc) * tap(1, dc)
    qcc_sc[...] = finish_qk(acc)

    cn_sc[...] = jnp.zeros_like(cn_sc)

    def run(n_rows, qc, v, sm, hf, hb, want_out, m0):
        n = n_rows // ML_BLOCK

        def body(i, m):
            rf = pl.multiple_of(i * ML_BLOCK, ML_BLOCK)
            rb = pl.multiple_of((n - 1 - i) * ML_BLOCK, ML_BLOCK)
            blks = [(qc[pl.ds(r0, ML_BLOCK), :], v[0, pl.ds(r0, ML_BLOCK), :],
                     sm[0, pl.ds(r0, ML_BLOCK), :] + gb_ref[...]) for r0 in (rf, rb)]
            outs, m_new = _mlstm_block(blks[0], blks[1], cn_sc, m, selh_ref, selp_ref, want_out)
            if want_out:
                for p in range(N_PAIRS):
                    hf[pl.ds(rf, ML_BLOCK), p * LANES:(p + 1) * LANES] = outs[(0, p)]
                    hb[pl.ds(rb, ML_BLOCK), p * LANES:(p + 1) * LANES] = outs[(1, p)]
            return m_new

        return lax.fori_loop(0, n, body, m0)

    m1 = run(tc, qcc_sc, vc_ref, smc_ref, hfc_sc, hbc_sc, ctx_out, jnp.zeros((1, LANES), F32))
    run(t, qc_sc, v_ref, sm_ref, hf_sc, hb_sc, True, m1)

    def finish(n_rows, hf, hb, og, dst):
        blk = min(n_rows, 256)

        def body(i, carry):
            r0 = pl.multiple_of(i * blk, blk)
            res = _sigmoid(og[0, pl.ds(r0, blk), :]) * (hf[pl.ds(r0, blk), :] + hb[pl.ds(r0, blk), :])
            dst[0, pl.ds(r0, blk), :] = res.astype(dst.dtype)
            return carry

        lax.fori_loop(0, n_rows // blk, body, 0)

    finish(t, hf_sc, hb_sc, og_ref, o_ref)
    if ctx_out:
        finish(tc, hfc_sc, hbc_sc, ogc_ref, oc_ref)


def _mlstm(p, pc, conv_w, gate_b, ctx_out):
    b, t, _ = p.shape
    tc = pc.shape[1]
    sel_head, sel_pair = (jnp.asarray(a).astype(BF16) for a in _mlstm_selectors())

    def col(width, idx, rows):
        return pl.BlockSpec((1, rows, width), lambda i: (i, 0, idx))

    full = lambda shape: pl.BlockSpec(shape, lambda i: tuple(0 for _ in shape))
    in_specs = [col(2 * GROUP_W, P_ML_QK, t), col(GROUP_W, P_ML_V, t), col(GROUP_W, P_ML_OG, t),
                col(LANES, P_SMALL, t),
                col(2 * GROUP_W, P_ML_QK, tc), col(GROUP_W, P_ML_V, tc), col(GROUP_W, P_ML_OG, tc),
                col(LANES, P_SMALL, tc),
                full(conv_w.shape), full(gate_b.shape), full(sel_head.shape), full(sel_pair.shape)]
    out_specs = [pl.BlockSpec((1, t, GROUP_W), lambda i: (i, 0, 0))]
    out_shape = [jax.ShapeDtypeStruct((b, t, GROUP_W), BF16)]
    scratch = [pltpu.VMEM((t, 2 * GROUP_W), F32), pltpu.VMEM((tc, 2 * GROUP_W), F32),
               pltpu.VMEM((t, GROUP_W), F32), pltpu.VMEM((t, GROUP_W), F32)]
    if ctx_out:
        out_specs.append(pl.BlockSpec((1, tc, GROUP_W), lambda i: (i, 0, 0)))
        out_shape.append(jax.ShapeDtypeStruct((b, tc, GROUP_W), BF16))
        scratch += [pltpu.VMEM((tc, GROUP_W), F32), pltpu.VMEM((tc, GROUP_W), F32)]
    scratch += [pltpu.VMEM((2 * N_PAIRS, LANES, 2 * LANES), F32)]
    res = pl.pallas_call(
        functools.partial(_mlstm_kernel, ctx_out=ctx_out),
        grid=(b,),
        in_specs=in_specs, out_specs=out_specs, out_shape=out_shape, scratch_shapes=scratch,
        compiler_params=_cparams(("parallel",)),
        name="mlstm",
    )(p, p, p, p, pc, pc, pc, pc, conv_w, gate_b, sel_head, sel_pair)
    return (res[0], res[1]) if ctx_out else (res[0], None)


def _sgu_kernel(u_ref, v_ref, f_ref, ws_ref, bias_ref, dft_ref, o_ref, z_ref):
    t = u_ref.shape[1]
    lane_g = _iota((1, GROUP_W), 1) // HEAD_DIM

    def body(i, carry):
        r0 = pl.multiple_of(i * SGU_CHUNK, SGU_CHUNK)
        v = v_ref[0, pl.ds(r0, SGU_CHUNK), :]
        mu = jnp.mean(v, axis=-1, keepdims=True)
        vc = v - mu
        vn = vc * lax.rsqrt(jnp.mean(vc * vc, axis=-1, keepdims=True) + EPS)
        mixed = bias_ref[...]
        for g in range(N_HEADS):
            mixed = mixed + _bdot(ws_ref[g], jnp.where(lane_g == g, vn, 0.0))
        o_ref[0, pl.ds(r0, SGU_CHUNK), :] = (u_ref[0, pl.ds(r0, SGU_CHUNK), :] * mixed).astype(o_ref.dtype)
        z = _bdot(f_ref[0, pl.ds(r0, SGU_CHUNK), :], dft_ref[...])
        z_ref[0, 0, pl.ds(r0, SGU_CHUNK), :] = z[:, :GROUP_W].astype(z_ref.dtype)
        z_ref[1, 0, pl.ds(r0, SGU_CHUNK), :] = z[:, GROUP_W:].astype(z_ref.dtype)
        return carry

    lax.fori_loop(0, t // SGU_CHUNK, body, 0)


def _sgu_fourier1(p, ws, bias, dft_c):
    b, t, _ = p.shape

    def col(idx):
        return pl.BlockSpec((1, t, GROUP_W), lambda i: (i, 0, idx))

    full = lambda shape: pl.BlockSpec(shape, lambda i: tuple(0 for _ in shape))
    return pl.pallas_call(
        _sgu_kernel,
        grid=(b,),
        in_specs=[col(P_SGU_U), col(P_SGU_V), col(P_FOUR), full(ws.shape), full(bias.shape), full(dft_c.shape)],
        out_specs=[pl.BlockSpec((1, t, GROUP_W), lambda i: (i, 0, 0)),
                   pl.BlockSpec((2, 1, t, GROUP_W), lambda i: (0, i, 0, 0))],
        out_shape=[jax.ShapeDtypeStruct((b, t, GROUP_W), BF16), jax.ShapeDtypeStruct((2, b, t, GROUP_W), BF16)],
        compiler_params=_cparams(("parallel",)),
        name="sgu_fourier1",
    )(p, p, p, ws, bias, dft_c)


def _fourier2_kernel(ct_ref, st_ref, z_ref, o_ref, *, scale):
    for i in range(z_ref.shape[1]):
        acc = jnp.dot(ct_ref[...], z_ref[0, i], preferred_element_type=F32)
        acc -= jnp.dot(st_ref[...], z_ref[1, i], preferred_element_type=F32)
        o_ref[i] = (acc * scale).astype(o_ref.dtype)


def _fourier2(z, cos_t, sin_t):
    _, b, t, w = z.shape
    tm = min(t, 256)
    return pl.pallas_call(
        functools.partial(_fourier2_kernel, scale=float((t * HEAD_DIM) ** -0.5)),
        grid=(t // tm,),
        in_specs=[pl.BlockSpec((tm, t), lambda i: (i, 0)),
                  pl.BlockSpec((tm, t), lambda i: (i, 0)),
                  pl.BlockSpec((2, b, t, w), lambda i: (0, 0, 0, 0), pipeline_mode=pl.Buffered(1))],
        out_specs=pl.BlockSpec((b, tm, w), lambda i: (0, i, 0)),
        out_shape=jax.ShapeDtypeStruct((b, t, w), BF16),
        compiler_params=_cparams(("parallel",)),
        name="fourier2",
    )(cos_t, sin_t, z)


def _dft_tables(n):
    k = np.arange(n, dtype=np.int64)
    ang = 2.0 * np.pi * ((k[:, None] * k[None, :]) % n).astype(np.float64) / n
    return np.cos(ang).astype(np.float32), np.sin(ang).astype(np.float32)


def _channel_dft():
    c, s = _dft_tables(HEAD_DIM)
    eye = np.eye(N_HEADS, dtype=np.float32)
    return np.concatenate([np.kron(eye, c), np.kron(eye, s)], axis=1)


def _route_kernel(x_ref, mod_ref, g_ref, wr_ref, h_ref, mi_ref, mf_ref, cnt_ref, cnt_sc):
    first = (pl.program_id(0) == 0) & (pl.program_id(1) == 0)

    @pl.when(first)
    def _():
        cnt_sc[...] = jnp.zeros_like(cnt_sc)

    tm = x_ref.shape[1]
    h = _norm_mod(x_ref[0], g_ref[...], mod_ref[0, 3:4, :], mod_ref[0, 4:5, :])
    for j in range(D_MODEL // LANES):
        h_ref[pl.ds(j, tm, stride=SUBLANES), :] = h[:, j * LANES:(j + 1) * LANES]
    logits = _fdot(h, wr_ref[...])
    lane = _iota((tm, LANES), 1).astype(F32)
    lg = jnp.where(lane < N_EXPERTS, logits, -jnp.inf)
    m1 = jnp.max(lg, axis=1, keepdims=True)
    i1 = jnp.min(jnp.where(lg == m1, lane, float(LANES)), axis=1, keepdims=True)
    lg2 = jnp.where(lane == i1, -jnp.inf, lg)
    m2 = jnp.max(lg2, axis=1, keepdims=True)
    i2 = jnp.min(jnp.where(lg2 == m2, lane, float(LANES)), axis=1, keepdims=True)
    e = jnp.exp(m2 - m1)
    g0 = 1.0 / (1.0 + e)
    g1 = e / (1.0 + e)
    onehot = jnp.where((lane == i1) | (lane == i2), 1.0, 0.0)
    strict = jnp.where(_iota((tm, tm), 1) < _iota((tm, tm), 0), 1.0, 0.0)
    before = _bdot(strict, onehot) + cnt_sc[...]
    r0 = jnp.sum(jnp.where(lane == i1, before, 0.0), axis=1, keepdims=True)
    r1 = jnp.sum(jnp.where(lane == i2, before, 0.0), axis=1, keepdims=True)
    cnt_sc[...] += jnp.sum(onehot, axis=0, keepdims=True)
    cnt_ref[...] = cnt_sc[...]
    l8 = _iota((tm, SUBLANES), 1)
    mi = jnp.where(l8 == 0, i1, jnp.where(l8 == 1, i2, jnp.where(l8 == 2, r0, r1)))
    mi_ref[...] = mi.astype(jnp.int32)
    mf_ref[...] = jnp.where(l8 == 0, g0, g1)


def _route(x, mod, g, w_router):
    b, t, d = x.shape
    m = b * t
    tm = 512
    nt = t // tm
    wr = jnp.zeros((d, LANES), F32).at[:, :N_EXPERTS].set(w_router)
    return pl.pallas_call(
        _route_kernel,
        grid=(b, nt),
        in_specs=[pl.BlockSpec((1, tm, d), lambda i, j: (i, j, 0)),
                  pl.BlockSpec((1, 6, d), lambda i, j: (i, 0, 0)),
                  pl.BlockSpec((1, d), lambda i, j: (0, 0)),
                  pl.BlockSpec((d, LANES), lambda i, j: (0, 0))],
        out_specs=[pl.BlockSpec((tm * SUBLANES, LANES), lambda i, j: (i * nt + j, 0)),
                   pl.BlockSpec((tm, SUBLANES), lambda i, j: (i * nt + j, 0)),
                   pl.BlockSpec((tm, SUBLANES), lambda i, j: (i * nt + j, 0)),
                   pl.BlockSpec((1, LANES), lambda i, j: (0, 0))],
        out_shape=[jax.ShapeDtypeStruct((m * SUBLANES, LANES), F32),
                   jax.ShapeDtypeStruct((m, SUBLANES), jnp.int32),
                   jax.ShapeDtypeStruct((m, SUBLANES), F32),
                   jax.ShapeDtypeStruct((1, LANES), F32)],
        scratch_shapes=[pltpu.VMEM((1, LANES), F32)],
        compiler_params=_cparams(("arbitrary", "arbitrary")),
        name="route",
    )(x, mod, g.reshape(1, d), wr)


def _moe_kernel(texp_ref, trows_ref, pos0_ref, pos1_ref, h_hbm, wg_ref, wu_ref, wd_ref, y_hbm,
                order_sm, xbuf, x_sc, acc_sc, ybuf, gsem, ssem):
    i = pl.program_id(0)
    j = pl.program_id(1)
    nf = pl.num_programs(1)
    tm = MOE_TM
    n_tok = pos0_ref.shape[0]
    rows = trows_ref[i]

    nt = pl.num_programs(0)
    buf_rows = tm * SUBLANES
    dump = TOP_K * n_tok
    slot = lax.rem(i, 2)
    prev_rows = trows_ref[jnp.maximum(i - 1, 0)]
    next_rows = trows_ref[jnp.minimum(i + 1, nt - 1)]
    n_copy_steps = tm // MOE_COPY_ROWS

    def gather_copy(tile, n_real, r, dst_slot):
        idx = jnp.where(n_real > 0, tile * tm + jnp.minimum(r, n_real - 1), 0)
        tok = jnp.right_shift(order_sm[idx], 1)
        off = pl.multiple_of(dst_slot * buf_rows + r * SUBLANES, SUBLANES)
        return pltpu.make_async_copy(h_hbm.at[tok], xbuf.at[pl.ds(off, SUBLANES), :], gsem)

    def scatter_copy(tile, n_real, r, src_slot):
        idx = jnp.where(n_real > 0, tile * tm + jnp.minimum(r, n_real - 1), 0)
        dst = jnp.where(r < n_real, order_sm[idx], dump + r)
        off = pl.multiple_of(src_slot * buf_rows + r * SUBLANES, SUBLANES)
        return pltpu.make_async_copy(ybuf.at[pl.ds(off, SUBLANES), :], y_hbm.at[dst], ssem)

    def wait_all(buf, sem):
        pltpu.make_async_copy(buf.at[pl.ds(0, buf_rows), :], buf.at[pl.ds(buf_rows, buf_rows), :], sem).wait()

    @pl.when((i == 0) & (j == 0))
    def _():
        def inv_body(t, carry):
            order_sm[pos0_ref[t]] = 2 * t
            order_sm[pos1_ref[t]] = 2 * t + 1
            return carry

        lax.fori_loop(0, n_tok, inv_body, 0, unroll=8)
        ybuf[...] = jnp.zeros_like(ybuf)

        def start(r, carry):
            gather_copy(0, rows, r, 0).start()
            return carry

        lax.fori_loop(0, tm, start, 0)

    @pl.when((j == 0) & (rows > 0))
    def _():
        wait_all(xbuf, gsem)
        base = pl.multiple_of(slot * buf_rows, SUBLANES)
        for c in range(D_MODEL // LANES):
            x_sc[:, c * LANES:(c + 1) * LANES] = xbuf[pl.ds(base + c, tm, stride=SUBLANES), :].astype(BF16)
        acc_sc[...] = jnp.zeros_like(acc_sc)

    def step(with_copies):
        if with_copies:
            r0 = j * MOE_COPY_ROWS
            for q in range(MOE_COPY_ROWS):
                gather_copy(i + 1, next_rows, r0 + q, 1 - slot).start()
                scatter_copy(i - 1, jnp.where(i > 0, prev_rows, 0), r0 + q, 1 - slot).start()
        x = x_sc[...]
        gate = jnp.dot(x, wg_ref[0].astype(BF16), preferred_element_type=F32)
        up = jnp.dot(x, wu_ref[0].astype(BF16), preferred_element_type=F32)
        acc_sc[...] += jnp.dot((_silu(gate) * up).astype(BF16), wd_ref[0].astype(BF16),
                               preferred_element_type=F32)

    pl.when((rows > 0) & (j < n_copy_steps))(functools.partial(step, True))
    pl.when((rows > 0) & (j >= n_copy_steps))(functools.partial(step, False))

    @pl.when((j == nf - 1) & (rows > 0))
    def _():
        wait_all(ybuf, ssem)
        base = pl.multiple_of(slot * buf_rows, SUBLANES)
        for c in range(D_MODEL // LANES):
            ybuf[pl.ds(base + c, tm, stride=SUBLANES), :] = acc_sc[:, c * LANES:(c + 1) * LANES]

    @pl.when((j == 0) & (rows == 0) & (i > 0) & (prev_rows > 0))
    def _():
        wait_all(xbuf, gsem)

        def start(r, carry):
            scatter_copy(i - 1, prev_rows, r, 1 - slot).start()
            return carry

        lax.fori_loop(0, tm, start, 0)
        wait_all(ybuf, ssem)


def _moe(h_tiles, pos0, pos1, tile_expert, tile_rows, wg, wu, wd):
    m = h_tiles.shape[0]
    n_exp, d, f = wg.shape
    nt = tile_expert.shape[0]
    tm, tf = MOE_TM, MOE_TF
    nf = f // tf
    assert nf >= tm // MOE_COPY_ROWS, "each tile needs enough column steps to issue its neighbours' row copies"
    assert nt * tm > TOP_K * m + n_exp * (tm - 1), "the last tile must be unused: it drains the copy pipeline"

    def col_tile(i, j, tr):
        return jnp.where(tr[i] > 0, j, nf - 1)

    grid_spec = pltpu.PrefetchScalarGridSpec(
        num_scalar_prefetch=4,
        grid=(nt, nf),
        in_specs=[pl.BlockSpec(memory_space=pl.ANY),
                  pl.BlockSpec((1, d, tf), lambda i, j, te, tr, p0, p1: (te[i], 0, col_tile(i, j, tr))),
                  pl.BlockSpec((1, d, tf), lambda i, j, te, tr, p0, p1: (te[i], 0, col_tile(i, j, tr))),
                  pl.BlockSpec((1, tf, d), lambda i, j, te, tr, p0, p1: (te[i], col_tile(i, j, tr), 0))],
        out_specs=pl.BlockSpec(memory_space=pl.ANY),
        scratch_shapes=[pltpu.SMEM((nt * tm,), jnp.int32),
                        pltpu.VMEM((2 * tm * SUBLANES, LANES), F32),
                        pltpu.VMEM((tm, d), BF16),
                        pltpu.VMEM((tm, d), F32),
                        pltpu.VMEM((2 * tm * SUBLANES, LANES), F32),
                        pltpu.SemaphoreType.DMA(()),
                        pltpu.SemaphoreType.DMA(())],
    )
    return pl.pallas_call(
        _moe_kernel,
        grid_spec=grid_spec,
        out_shape=jax.ShapeDtypeStruct((TOP_K * m + tm, SUBLANES, LANES), F32),
        compiler_params=_cparams(("arbitrary", "arbitrary")),
        name="moe",
    )(tile_expert, tile_rows, pos0, pos1, h_tiles, wg, wu, wd)


def _combine_kernel(x_ref, mod_ref, y_ref, mf_ref, fg_ref, o_ref):
    tm = x_ref.shape[1]
    stride = TOP_K * SUBLANES
    g0 = mf_ref[:, 0:1]
    g1 = mf_ref[:, 1:2]
    parts = []
    for c in range(D_MODEL // LANES):
        y0 = y_ref[pl.ds(c, tm, stride=stride), :]
        y1 = y_ref[pl.ds(SUBLANES + c, tm, stride=stride), :]
        parts.append(g0 * y0 + g1 * y1)
    moe = jnp.concatenate(parts, axis=1)
    x = x_ref[0] + mod_ref[0, 5:6, :] * moe
    ms = jnp.mean(x * x, axis=-1, keepdims=True)
    o_ref[0] = x * lax.rsqrt(ms + EPS) * fg_ref[...]


def _combine(x, mod, y, mf, final_g):
    b, t, d = x.shape
    tm = 256
    nt = t // tm
    rows = tm * TOP_K * SUBLANES
    return pl.pallas_call(
        _combine_kernel,
        grid=(b, nt),
        in_specs=[pl.BlockSpec((1, tm, d), lambda i, j: (i, j, 0)),
                  pl.BlockSpec((1, 6, d), lambda i, j: (i, 0, 0)),
                  pl.BlockSpec((rows, LANES), lambda i, j: (i * nt + j, 0)),
                  pl.BlockSpec((tm, SUBLANES), lambda i, j: (i * nt + j, 0)),
                  pl.BlockSpec((1, d), lambda i, j: (0, 0))],
        out_specs=pl.BlockSpec((1, tm, d), lambda i, j: (i, j, 0)),
        out_shape=jax.ShapeDtypeStruct((b, t, d), F32),
        compiler_params=_cparams(("parallel", "parallel")),
        name="combine",
    )(x, mod, y, mf, final_g.reshape(1, d))


def _moe_layer(x, mod, norm_g, w_router, wg, wu, wd, final_g):
    b, t, d = x.shape
    m = b * t
    h_tiles, mi, mf, cnt = _route(x, mod, norm_g, w_router)
    counts = cnt[0, :N_EXPERTS].astype(jnp.int32)
    tiles_per = (counts + MOE_TM - 1) // MOE_TM
    tile_end = jnp.cumsum(tiles_per)
    tile_start = tile_end - tiles_per
    seg_start = tile_start * MOE_TM
    nt = (TOP_K * m) // MOE_TM + N_EXPERTS
    tile_id = jnp.arange(nt, dtype=jnp.int32)
    used_id = jnp.minimum(tile_id, tile_end[-1] - 1)
    tile_expert = jnp.sum(used_id[:, None] >= tile_end[None, :], axis=1).astype(jnp.int32)
    tile_rows = jnp.clip(counts[tile_expert] - (tile_id - tile_start[tile_expert]) * MOE_TM, 0, MOE_TM)
    tile_rows = jnp.where(tile_id < tile_end[-1], tile_rows, 0).astype(jnp.int32)
    pos0 = seg_start[mi[:, 0]] + mi[:, 2]
    pos1 = seg_start[mi[:, 1]] + mi[:, 3]
    y = _moe(h_tiles.reshape(m, SUBLANES, LANES), pos0, pos1, tile_expert, tile_rows, wg, wu, wd)
    return _combine(x, mod, y.reshape(-1, LANES), mf, final_g)


def _permute_w_in(w):
    d = w.shape[0]
    return jnp.concatenate([w[:, 0:768], w[:, 1312:1568], w[:, 800:1312], w[:, 1568:1824], w[:, 1840:2608],
                            w[:, 768:800], w[:, 1824:1840], jnp.zeros((d, P_WIDTH - 2608), w.dtype)],
                           axis=1).astype(BF16)


def _mixers(p, pc, gla_w_lr, gla_b_lr, gla_norm_g, mlstm_conv, mlstm_gate_b, sgu_w, sgu_b, ctx_out, tables):
    wlr = jnp.zeros((LANES, 2 * GLA_QK_W), F32)
    wlr = wlr.at[0:GLA_RANK, 0:GLA_QK_W].set(gla_w_lr[0]).at[GLA_RANK:2 * GLA_RANK, GLA_QK_W:].set(gla_w_lr[1])
    blr = gla_b_lr.reshape(1, 2 * GLA_QK_W)
    gla, gla_c = _gla(p, pc, wlr, blr, gla_norm_g.reshape(1, GROUP_W), ctx_out)
    conv_w = mlstm_conv.reshape(9, 2 * GROUP_W)
    gate_b = jnp.zeros((1, LANES), F32).at[0, SMALL_GATE_LANE:SMALL_GATE_LANE + 4 * N_HEADS].set(
        mlstm_gate_b.reshape(-1))
    ml, ml_c = _mlstm(p, pc, conv_w, gate_b, ctx_out)
    bias = jnp.repeat(sgu_b.T, HEAD_DIM, axis=1)
    dft_c, tabs = tables
    sg, z = _sgu_fourier1(p, sgu_w, bias, dft_c)
    fo = _fourier2(z, *tabs[p.shape[1]])
    mix = (gla, ml, sg, fo)
    if not ctx_out:
        return mix, None
    sg_c, z_c = _sgu_fourier1(pc, sgu_w, bias, dft_c)
    fo_c = _fourier2(z_c, *tabs[pc.shape[1]])
    return mix, (gla_c, ml_c, sg_c, fo_c)


def kernel(x, c, ctx, c_ctx, w_ada, b_ada, norm_mix_g, norm_ffn_g, w_in, w_out, gla_w_lr, gla_b_lr, gla_norm_g,
           mlstm_conv, mlstm_gate_b, sgu_w, sgu_b, ffn_w_gate, ffn_w_up, ffn_w_down, moe_router, moe_w_gate,
           moe_w_up, moe_w_down, final_norm_g):
    depth = w_ada.shape[0]
    assert depth == 2, "layer 0 is the dense layer with context output, layer 1 the MoE layer"
    b, t, d = x.shape
    tc = ctx.shape[1]
    cvec = jnp.zeros((16, d), F32).at[:b].set(c).at[b].set(c_ctx)
    mods = _ada(cvec, w_ada, b_ada).reshape(depth, 16, 6, d)
    tables = (jnp.asarray(_channel_dft()).astype(BF16),
              {n: tuple(jnp.asarray(a).astype(BF16) for a in _dft_tables(n)) for n in {t, tc}})
    xc = ctx
    for l in range(depth):
        ctx_out = l < depth - 1
        mod, mod_c = mods[l, :b], mods[l, b:b + 1]
        w_in_l = _permute_w_in(w_in[l])
        w_out_l = w_out[l].astype(BF16)
        p = _in_proj(x, mod, norm_mix_g[l], w_in_l)
        pc = _in_proj(xc, mod_c, norm_mix_g[l], w_in_l)
        mix, mix_c = _mixers(p, pc, gla_w_lr[l], gla_b_lr[l], gla_norm_g[l], mlstm_conv[l], mlstm_gate_b[l],
                             sgu_w[l], sgu_b[l], ctx_out, tables)
        x = _out_proj(x, mod, mix, w_out_l)
        if ctx_out:
            xc = _out_proj(xc, mod_c, mix_c, w_out_l)
        i = l // 2
        if l % 2 == 0:
            wg, wu, wd = ffn_w_gate[i].astype(BF16), ffn_w_up[i].astype(BF16), ffn_w_down[i].astype(BF16)
            x = _ffn(x, mod, norm_ffn_g[l], wg, wu, wd)
            if ctx_out:
                xc = _ffn(xc, mod_c, norm_ffn_g[l], wg, wu, wd)
        else:
            x = _moe_layer(x, mod, norm_ffn_g[l], moe_router[i], moe_w_gate[i], moe_w_up[i], moe_w_down[i],
                           final_norm_g)
    return x
```

```python
import functools

import numpy as np
import jax
import jax.numpy as jnp
from jax import lax
from jax.experimental import pallas as pl
from jax.experimental.pallas import tpu as pltpu

F32 = jnp.float32
BF16 = jnp.bfloat16
HIGHEST = lax.Precision.HIGHEST

D_MODEL = 1024
GROUP_W = 256
N_HEADS = 4
HEAD_DIM = 64
GLA_DK = 32
GLA_QK_W = N_HEADS * GLA_DK
GLA_RANK = 16
GLA_NORMALIZER = 16.0
GLA_BLOCK = 128
GRID_W = 64
SGU_CHUNK = 128
N_EXPERTS = 8
TOP_K = 2
EPS = 1e-6

LANES = 128
SUBLANES = 8
VMEM_LIMIT = 52 * 1024 * 1024

P_GLA_QK, P_GLA_V, P_GLA_G, P_ML_V = 0, 1, 2, 3
P_ML_QK = 2
P_ML_OG, P_SGU_U, P_SGU_V, P_FOUR = 6, 7, 8, 9
P_SMALL = 20
P_WIDTH = 2688
SMALL_GATE_LANE = 32

MOE_TM = 1024
MOE_TF = 512
MOE_COPY_ROWS = 160


def _cparams(sem, **kw):
    return pltpu.CompilerParams(dimension_semantics=sem, vmem_limit_bytes=VMEM_LIMIT, **kw)


def _sigmoid(x):
    return 1.0 / (1.0 + jnp.exp(-x))


def _silu(x):
    return x * _sigmoid(x)


def _log_sigmoid(x):
    return jnp.minimum(x, 0.0) - jnp.log(1.0 + jnp.exp(-jnp.abs(x)))


def _norm_mod(x, g, shift, scale):
    ms = jnp.mean(x * x, axis=-1, keepdims=True)
    return (x * lax.rsqrt(ms + EPS) * g) * (1.0 + scale) + shift


def _bdot(a, b):
    return jnp.dot(a.astype(BF16), b.astype(BF16), preferred_element_type=F32)


def _bdot_nt(a, b):
    return lax.dot_general(a.astype(BF16), b.astype(BF16), (((1,), (1,)), ((), ())), preferred_element_type=F32)


def _fdot(a, b):
    return jnp.dot(a, b, precision=HIGHEST, preferred_element_type=F32)


def _split3(x):
    hi = x.astype(BF16)
    r = x - hi.astype(F32)
    mid = r.astype(BF16)
    lo = (r - mid.astype(F32)).astype(BF16)
    return hi, mid, lo


def _dot3(a, b):
    a_hi, a_lo, _ = _split3(a)
    b_hi, b_lo, _ = _split3(b)
    dot = functools.partial(jnp.dot, preferred_element_type=F32)
    return dot(a_hi, b_hi) + dot(a_lo, b_hi) + dot(a_hi, b_lo)


def _tri_dot(tri, x):
    n = x.shape[1]
    y = jnp.dot(tri.astype(BF16), jnp.concatenate(_split3(x), axis=1), preferred_element_type=F32)
    return y[:, :n] + y[:, n:2 * n] + y[:, 2 * n:]


def _ada_kernel(c_ref, w_ref, b_ref, o_ref):
    o_ref[0] = _fdot(_silu(c_ref[...]), w_ref[0]) + b_ref[0]


def _ada(cvec, w_ada, b_ada):
    depth, d, n = w_ada.shape
    rows = cvec.shape[0]
    tn = 1536
    return pl.pallas_call(
        _ada_kernel,
        grid=(depth, n // tn),
        in_specs=[pl.BlockSpec((rows, d), lambda l, j: (0, 0)),
                  pl.BlockSpec((1, d, tn), lambda l, j: (l, 0, j)),
                  pl.BlockSpec((1, 1, tn), lambda l, j: (l, 0, j))],
        out_specs=pl.BlockSpec((1, rows, tn), lambda l, j: (l, 0, j)),
        out_shape=jax.ShapeDtypeStruct((depth, rows, n), F32),
        compiler_params=_cparams(("parallel", "parallel")),
        name="ada",
    )(cvec, w_ada, b_ada.reshape(depth, 1, n))


def _in_kernel(x_ref, mod_ref, g_ref, w_ref, o_ref):
    h = _norm_mod(x_ref[0], g_ref[...], mod_ref[0, 0:1, :], mod_ref[0, 1:2, :])
    o_ref[0] = jnp.dot(h.astype(BF16), w_ref[...], preferred_element_type=F32)


def _in_proj(x, mod, g, w):
    b, t, d = x.shape
    n = w.shape[1]
    tm = min(t, 512)
    mod_map = (lambda i, j: (i, 0, 0)) if mod.shape[0] == b else (lambda i, j: (0, 0, 0))
    return pl.pallas_call(
        _in_kernel,
        grid=(b, t // tm),
        in_specs=[pl.BlockSpec((1, tm, d), lambda i, j: (i, j, 0)),
                  pl.BlockSpec((1, 6, d), mod_map),
                  pl.BlockSpec((1, d), lambda i, j: (0, 0)),
                  pl.BlockSpec((d, n), lambda i, j: (0, 0))],
        out_specs=pl.BlockSpec((1, tm, n), lambda i, j: (i, j, 0)),
        out_shape=jax.ShapeDtypeStruct((b, t, n), F32),
        compiler_params=_cparams(("parallel", "parallel")),
        name="in_proj",
    )(x, mod, g.reshape(1, d), w)


def _out_kernel(x_ref, mod_ref, a_ref, b_ref, c_ref, d_ref, w_ref, o_ref):
    acc = jnp.dot(a_ref[0], w_ref[0:GROUP_W, :], preferred_element_type=F32)
    acc += jnp.dot(b_ref[0], w_ref[GROUP_W:2 * GROUP_W, :], preferred_element_type=F32)
    acc += jnp.dot(c_ref[0], w_ref[2 * GROUP_W:3 * GROUP_W, :], preferred_element_type=F32)
    acc += jnp.dot(d_ref[0], w_ref[3 * GROUP_W:4 * GROUP_W, :], preferred_element_type=F32)
    o_ref[0] = x_ref[0] + mod_ref[0, 2:3, :] * acc


def _out_proj(x, mod, mixes, w):
    b, t, d = x.shape
    tm = min(t, 512)
    mod_map = (lambda i, j: (i, 0, 0)) if mod.shape[0] == b else (lambda i, j: (0, 0, 0))
    mix_spec = pl.BlockSpec((1, tm, GROUP_W), lambda i, j: (i, j, 0))
    return pl.pallas_call(
        _out_kernel,
        grid=(b, t // tm),
        in_specs=[pl.BlockSpec((1, tm, d), lambda i, j: (i, j, 0)),
                  pl.BlockSpec((1, 6, d), mod_map),
                  mix_spec, mix_spec, mix_spec, mix_spec,
                  pl.BlockSpec((d, d), lambda i, j: (0, 0))],
        out_specs=pl.BlockSpec((1, tm, d), lambda i, j: (i, j, 0)),
        out_shape=jax.ShapeDtypeStruct((b, t, d), F32),
        compiler_params=_cparams(("parallel", "parallel")),
        name="out_proj",
    )(x, mod, *mixes, w)


def _ffn_kernel(x_ref, mod_ref, g_ref, wg_ref, wu_ref, wd_ref, o_ref, h_sc, acc_sc):
    j = pl.program_id(2)

    @pl.when(j == 0)
    def _():
        h = _norm_mod(x_ref[0], g_ref[...], mod_ref[0, 3:4, :], mod_ref[0, 4:5, :])
        h_sc[...] = h.astype(BF16)
        acc_sc[...] = jnp.zeros_like(acc_sc)

    h = h_sc[...]
    gate = jnp.dot(h, wg_ref[...], preferred_element_type=F32)
    up = jnp.dot(h, wu_ref[...], preferred_element_type=F32)
    acc_sc[...] += jnp.dot((_silu(gate) * up).astype(BF16), wd_ref[...], preferred_element_type=F32)

    @pl.when(j == pl.num_programs(2) - 1)
    def _():
        o_ref[0] = x_ref[0] + mod_ref[0, 5:6, :] * acc_sc[...]


def _ffn(x, mod, g, wg, wu, wd):
    b, t, d = x.shape
    f = wg.shape[1]
    tm = min(t, 512)
    tf = f // 2
    mod_map = (lambda i, j, k: (i, 0, 0)) if mod.shape[0] == b else (lambda i, j, k: (0, 0, 0))
    return pl.pallas_call(
        _ffn_kernel,
        grid=(b, t // tm, f // tf),
        in_specs=[pl.BlockSpec((1, tm, d), lambda i, j, k: (i, j, 0)),
                  pl.BlockSpec((1, 6, d), mod_map),
                  pl.BlockSpec((1, d), lambda i, j, k: (0, 0)),
                  pl.BlockSpec((d, tf), lambda i, j, k: (0, k)),
                  pl.BlockSpec((d, tf), lambda i, j, k: (0, k)),
                  pl.BlockSpec((tf, d), lambda i, j, k: (k, 0))],
        out_specs=pl.BlockSpec((1, tm, d), lambda i, j, k: (i, j, 0)),
        out_shape=jax.ShapeDtypeStruct((b, t, d), F32),
        scratch_shapes=[pltpu.VMEM((tm, d), BF16), pltpu.VMEM((tm, d), F32)],
        compiler_params=_cparams(("parallel", "parallel", "arbitrary")),
        name="ffn",
    )(x, mod, g.reshape(1, d), wg, wu, wd)


def _iota(shape, dim):
    return lax.broadcasted_iota(jnp.int32, shape, dim)


def _gla_chunk(q, k, v, la, s_ref, lower, want_out):
    L = q.shape[0]
    r, c = _iota((L, L), 0), _iota((L, L), 1)
    keep = (c <= r) if lower else (c >= r)
    cum = _tri_dot(jnp.where(keep, 1.0, 0.0), la)
    tot_row = jnp.sum(la, axis=0, keepdims=True)
    s_prev = s_ref[...]
    out = None
    if want_out:
        mid = L // 2 - 1 if lower else L // 2
        ref = cum[mid:mid + 1, :]
        qr = q * jnp.exp(cum - ref)
        kc = (k * jnp.exp(ref - cum)).astype(BF16)
        lane_q = _iota((1, GLA_QK_W), 1) // GLA_DK
        lane_v = _iota((1, GROUP_W), 1) // HEAD_DIM
        out = _bdot_nt(q * jnp.exp(cum), s_prev)
        for h in range(N_HEADS):
            sc = jnp.where(keep, _bdot_nt(jnp.where(lane_q == h, qr, 0.0), kc), 0.0)
            out = out + _bdot(sc, jnp.where(lane_v == h, v, 0.0))
    ke = k * jnp.exp(tot_row - cum)
    upd = jnp.dot(v.T.astype(BF16), ke.astype(BF16), preferred_element_type=F32)
    bd = (_iota((GROUP_W, GLA_QK_W), 0) // HEAD_DIM) == (_iota((GROUP_W, GLA_QK_W), 1) // GLA_DK)
    s_ref[...] = jnp.exp(tot_row) * s_prev + jnp.where(bd, upd, 0.0)
    return out


def _gla_kernel(qk_ref, v_ref, g_ref, sm_ref, qkc_ref, vc_ref, gc_ref, smc_ref, wlr_ref, blr_ref, ng_ref,
                *rest, ctx_out):
    if ctx_out:
        o_ref, oc_ref, la_sc, lac_sc, of_sc, ob_sc, ofc_sc, obc_sc, sf_sc, sb_sc = rest
    else:
        o_ref, la_sc, lac_sc, of_sc, ob_sc, sf_sc, sb_sc = rest
        oc_ref = ofc_sc = obc_sc = None
    t = qk_ref.shape[1]
    tc = qkc_ref.shape[1]
    qscale = GLA_DK ** -0.5

    def log_decay(sm):
        z = _dot3(sm, wlr_ref[...]) + blr_ref[...]
        return _log_sigmoid(z) * (1.0 / GLA_NORMALIZER)

    lac_sc[...] = log_decay(smc_ref[0])
    rows_blk = 256

    def la_body(i, carry):
        r0 = pl.multiple_of(i * rows_blk, rows_blk)
        la_sc[pl.ds(r0, rows_blk), :] = log_decay(sm_ref[0, pl.ds(r0, rows_blk), :])
        return carry

    lax.fori_loop(0, t // rows_blk, la_body, 0)
    sf_sc[...] = jnp.zeros_like(sf_sc)
    sb_sc[...] = jnp.zeros_like(sb_sc)

    def run(n_rows, qk, v, la, of, ob, want_out):
        n = n_rows // GLA_BLOCK

        def body(i, carry):
            rf = pl.multiple_of(i * GLA_BLOCK, GLA_BLOCK)
            rb = pl.multiple_of((n - 1 - i) * GLA_BLOCK, GLA_BLOCK)
            for r0, s_ref, lower, dst, lane0 in ((rf, sf_sc, True, of, 0), (rb, sb_sc, False, ob, GLA_QK_W)):
                qkb = qk[0, pl.ds(r0, GLA_BLOCK), :]
                out = _gla_chunk(qkb[:, :GLA_QK_W] * qscale, qkb[:, GLA_QK_W:], v[0, pl.ds(r0, GLA_BLOCK), :],
                                 la[pl.ds(r0, GLA_BLOCK), lane0:lane0 + GLA_QK_W], s_ref, lower, want_out)
                if want_out:
                    dst[pl.ds(r0, GLA_BLOCK), :] = out
            return carry

        lax.fori_loop(0, n, body, 0)

    run(tc, qkc_ref, vc_ref, lac_sc, ofc_sc, obc_sc, ctx_out)
    run(t, qk_ref, v_ref, la_sc, of_sc, ob_sc, True)

    grp = jnp.where((_iota((GROUP_W, GROUP_W), 0) // HEAD_DIM) == (_iota((GROUP_W, GROUP_W), 1) // HEAD_DIM),
                    1.0 / HEAD_DIM, 0.0).astype(BF16)

    def finish(n_rows, of, ob, g, dst):
        blk = min(n_rows, 256)

        def body(i, carry):
            r0 = pl.multiple_of(i * blk, blk)
            o = of[pl.ds(r0, blk), :] + ob[pl.ds(r0, blk), :]
            sq_hi, sq_lo, _ = _split3(o * o)
            ms = (jnp.dot(sq_hi, grp, preferred_element_type=F32) + jnp.dot(sq_lo, grp, preferred_element_type=F32))
            res = o * lax.rsqrt(ms + EPS) * ng_ref[...] * _silu(g[0, pl.ds(r0, blk), :])
            dst[0, pl.ds(r0, blk), :] = res.astype(dst.dtype)
            return carry

        lax.fori_loop(0, n_rows // blk, body, 0)

    finish(t, of_sc, ob_sc, g_ref, o_ref)
    if ctx_out:
        finish(tc, ofc_sc, obc_sc, gc_ref, oc_ref)


def _gla(p, pc, wlr, blr, ng, ctx_out):
    b, t, _ = p.shape
    tc = pc.shape[1]

    def col(width, idx, rows):
        return pl.BlockSpec((1, rows, width), lambda i: (i, 0, idx))

    full = lambda shape: pl.BlockSpec(shape, lambda i: tuple(0 for _ in shape))
    in_specs = [col(GROUP_W, P_GLA_QK, t), col(GROUP_W, P_GLA_V, t), col(GROUP_W, P_GLA_G, t), col(LANES, P_SMALL, t),
                col(GROUP_W, P_GLA_QK, tc), col(GROUP_W, P_GLA_V, tc), col(GROUP_W, P_GLA_G, tc),
                col(LANES, P_SMALL, tc),
                full(wlr.shape), full(blr.shape), full(ng.shape)]
    out_specs = [pl.BlockSpec((1, t, GROUP_W), lambda i: (i, 0, 0))]
    out_shape = [jax.ShapeDtypeStruct((b, t, GROUP_W), BF16)]
    scratch = [pltpu.VMEM((t, 2 * GLA_QK_W), F32), pltpu.VMEM((tc, 2 * GLA_QK_W), F32),
               pltpu.VMEM((t, GROUP_W), F32), pltpu.VMEM((t, GROUP_W), F32)]
    if ctx_out:
        out_specs.append(pl.BlockSpec((1, tc, GROUP_W), lambda i: (i, 0, 0)))
        out_shape.append(jax.ShapeDtypeStruct((b, tc, GROUP_W), BF16))
        scratch += [pltpu.VMEM((tc, GROUP_W), F32), pltpu.VMEM((tc, GROUP_W), F32)]
    scratch += [pltpu.VMEM((GROUP_W, GLA_QK_W), F32), pltpu.VMEM((GROUP_W, GLA_QK_W), F32)]
    res = pl.pallas_call(
        functools.partial(_gla_kernel, ctx_out=ctx_out),
        grid=(b,),
        in_specs=in_specs, out_specs=out_specs, out_shape=out_shape, scratch_shapes=scratch,
        compiler_params=_cparams(("parallel",)),
        name="gla",
    )(p, p, p, p, pc, pc, pc, pc, wlr, blr, ng)
    return (res[0], res[1]) if ctx_out else (res[0], None)


ML_BLOCK = 128
N_PAIRS = N_HEADS // 2


def _gate_lane(d, h):
    return SMALL_GATE_LANE + 8 * d + h


def _mlstm_selectors():
    sel_head = np.zeros((LANES, 2 * N_HEADS * LANES), np.float32)
    sel_pair = np.zeros((LANES, 2 * N_PAIRS * LANES), np.float32)
    for d in range(2):
        for h in range(N_HEADS):
            sel_head[_gate_lane(d, h), (d * N_HEADS + h) * LANES:(d * N_HEADS + h + 1) * LANES] = 1.0
            p, hh = divmod(h, 2)
            lo = (d * N_PAIRS + p) * LANES + hh * HEAD_DIM
            sel_pair[_gate_lane(d, h), lo:lo + HEAD_DIM] = 1.0
    return sel_head, sel_pair


def _mlstm_block(blk_f, blk_b, cn_ref, m_prev, selh_ref, selp_ref, want_out):
    L = ML_BLOCK
    lane = _iota((1, LANES), 1)
    is_b = (lane >= _gate_lane(1, 0)) & (lane < _gate_lane(1, 0) + N_HEADS)
    valid = ((lane >= _gate_lane(0, 0)) & (lane < _gate_lane(0, 0) + N_HEADS)) | is_b
    r, c = _iota((L, L), 0), _iota((L, L), 1)
    keep_f, keep_b = c <= r, c >= r
    lf_f = pltpu.roll(_log_sigmoid(blk_f[2]), LANES - 4, 1)
    lf_b = pltpu.roll(_log_sigmoid(blk_b[2]), LANES - 4, 1)
    cum_f = _tri_dot(jnp.where(keep_f, 1.0, 0.0), lf_f)
    cum_b = _tri_dot(jnp.where(keep_b, 1.0, 0.0), lf_b)
    cum = jnp.where(valid, jnp.where(is_b, cum_b, cum_f), 0.0)
    ig = jnp.where(valid, jnp.where(is_b, blk_b[2], blk_f[2]), 0.0)
    tot = jnp.where(valid, jnp.where(is_b, cum_b[0:1, :], cum_f[L - 1:L, :]), 0.0)
    b = ig - cum
    row = _iota((L, 1), 0)
    pm = b
    k = 1
    while k < L:
        from_prev = jnp.where(row >= k, pltpu.roll(pm, k, 0), -jnp.inf)
        from_next = jnp.where(row < L - k, pltpu.roll(pm, L - k, 0), -jnp.inf)
        pm = jnp.maximum(pm, jnp.where(is_b, from_next, from_prev))
        k *= 2
    mm = jnp.maximum(m_prev, pm)
    w_end = tot + b
    m_chunk = jnp.max(w_end, axis=0, keepdims=True)
    m_new = jnp.maximum(tot + m_prev, m_chunk)
    a_row = jnp.exp(tot + m_prev - m_new)
    g_row = jnp.exp(m_chunk - m_new)
    cols = [jnp.exp(w_end - m_chunk)]
    if want_out:
        cols += [jnp.exp(m_prev - mm), jnp.exp(-(cum + mm))]
        mm_hi, mm_lo, _ = _split3(mm)
        mm_t = (jnp.dot(mm_hi, selh_ref[...], preferred_element_type=F32)
                + jnp.dot(mm_lo, selh_ref[...], preferred_element_type=F32))
        b_rows = b.T
    pair_t = _bdot(jnp.concatenate(cols, axis=0), selp_ref[...])
    lo_half = lane < HEAD_DIM
    ones_t = jnp.ones((L, LANES), F32)
    bd = (_iota((LANES, 2 * LANES), 0) // HEAD_DIM) == ((_iota((LANES, 2 * LANES), 1) % LANES) // HEAD_DIM)
    outs = {}
    for d, (blk, keep) in enumerate(((blk_f, keep_f), (blk_b, keep_b))):
        qk, v, _ = blk
        for p in range(N_PAIRS):
            qp = qk[:, p * LANES:(p + 1) * LANES]
            kp = qk[:, GROUP_W + p * LANES:GROUP_W + (p + 1) * LANES]
            vp = v[:, p * LANES:(p + 1) * LANES]
            cn = cn_ref[d * N_PAIRS + p]
            tcol = (d * N_PAIRS + p) * LANES
            if want_out:
                nd = None
                for hh in range(2):
                    h = 2 * p + hh
                    mine = lo_half if hh == 0 else jnp.logical_not(lo_half)
                    gl = _gate_lane(d, h)
                    hcol = (d * N_HEADS + h) * LANES
                    e = jnp.where(keep, jnp.exp(b_rows[gl:gl + 1, :] - mm_t[:, hcol:hcol + LANES]), 0.0)
                    s = e * _bdot_nt(jnp.where(mine, qp, 0.0), kp)
                    va = jnp.concatenate([jnp.where(mine, vp, 0.0), jnp.where(mine, ones_t, 0.0)], axis=1)
                    t = _bdot(s, va)
                    nd = t if nd is None else nd + t
                w_int = pair_t[L:2 * L, tcol:tcol + LANES]
                nd = nd + jnp.concatenate([w_int, w_int], axis=1) * _bdot(qp, cn)
                outs[(d, p)] = nd[:, :LANES] / jnp.maximum(jnp.abs(nd[:, LANES:]),
                                                           pair_t[2 * L:3 * L, tcol:tcol + LANES])
            kw_t = (kp * pair_t[0:L, tcol:tcol + LANES]).T
            upd = _bdot(kw_t, jnp.concatenate([vp, ones_t], axis=1))
            ga, gb = _gate_lane(d, 2 * p), _gate_lane(d, 2 * p + 1)
            half = (HEAD_DIM, 2 * LANES)
            a_t = jnp.concatenate([jnp.broadcast_to(a_row[:, ga:ga + 1], half),
                                   jnp.broadcast_to(a_row[:, gb:gb + 1], half)], axis=0)
            g_t = jnp.concatenate([jnp.broadcast_to(g_row[:, ga:ga + 1], half),
                                   jnp.broadcast_to(g_row[:, gb:gb + 1], half)], axis=0)
            cn_ref[d * N_PAIRS + p] = a_t * cn + g_t * jnp.where(bd, upd, 0.0)
    return (outs if want_out else None), m_new


def _mlstm_kernel(qk_ref, v_ref, og_ref, sm_ref, qkc_ref, vc_ref, ogc_ref, smc_ref, cw_ref, gb_ref,
                  selh_ref, selp_ref, *rest, ctx_out):
    if ctx_out:
        o_ref, oc_ref, qc_sc, qcc_sc, hf_sc, hb_sc, hfc_sc, hbc_sc, cn_sc = rest
    else:
        o_ref, qc_sc, qcc_sc, hf_sc, hb_sc, cn_sc = rest
        oc_ref = hfc_sc = hbc_sc = None
    t = qk_ref.shape[1]
    tc = qkc_ref.shape[1]
    w2 = 2 * GROUP_W
    kscale = HEAD_DIM ** -0.5
    n_rows_grid = t // GRID_W

    def tap(dr, dc):
        return cw_ref[dr * 3 + dc:dr * 3 + dc + 1, :]

    def shifted(blk, dc):
        n = blk.shape[0]
        if dc == 1:
            return blk
        ridx = _iota((n, 1), 0)
        if dc == 0:
            return jnp.where(ridx == 0, 0.0, pltpu.roll(blk, 1, 0))
        return jnp.where(ridx == n - 1, 0.0, pltpu.roll(blk, n - 1, 0))

    def finish_qk(acc):
        a = _silu(acc)
        lane = _iota((1, w2), 1)
        return jnp.where(lane >= GROUP_W, a * kscale, a)

    def conv_body(i, carry):
        r0 = pl.multiple_of(i * GRID_W, GRID_W)
        r_up = pl.multiple_of(jnp.maximum(i - 1, 0) * GRID_W, GRID_W)
        r_dn = pl.multiple_of(jnp.minimum(i + 1, n_rows_grid - 1) * GRID_W, GRID_W)
        up = qk_ref[0, pl.ds(r_up, GRID_W), :] * jnp.where(i > 0, 1.0, 0.0)
        mid = qk_ref[0, pl.ds(r0, GRID_W), :]
        dn = qk_ref[0, pl.ds(r_dn, GRID_W), :] * jnp.where(i < n_rows_grid - 1, 1.0, 0.0)
        acc = jnp.zeros((GRID_W, w2), F32)
        for dr, blk in enumerate((up, mid, dn)):
            for dc in range(3):
                acc = acc + shifted(blk, dc) * tap(dr, dc)
        qc_sc[pl.ds(r0, GRID_W), :] = finish_qk(acc)
        return carry

    lax.fori_loop(0, n_rows_grid, conv_body, 0)
    xc = qkc_ref[0]
    acc = jnp.zeros((tc, w2), F32)
    for dc in range(3):
        acc = acc + shifted(xc, dc) * tap(1, dc)
    qcc_sc[...] = finish_qk(acc)

    cn_sc[...] = jnp.zeros_like(cn_sc)

    def run(n_rows, qc, v, sm, hf, hb, want_out, m0):
        n = n_rows // ML_BLOCK

        def body(i, m):
            rf = pl.multiple_of(i * ML_BLOCK, ML_BLOCK)
            rb = pl.multiple_of((n - 1 - i) * ML_BLOCK, ML_BLOCK)
            blks = [(qc[pl.ds(r0, ML_BLOCK), :], v[0, pl.ds(r0, ML_BLOCK), :],
                     sm[0, pl.ds(r0, ML_BLOCK), :] + gb_ref[...]) for r0 in (rf, rb)]
            outs, m_new = _mlstm_block(blks[0], blks[1], cn_sc, m, selh_ref, selp_ref, want_out)
            if want_out:
                for p in range(N_PAIRS):
                    hf[pl.ds(rf, ML_BLOCK), p * LANES:(p + 1) * LANES] = outs[(0, p)]
                    hb[pl.ds(rb, ML_BLOCK), p * LANES:(p + 1) * LANES] = outs[(1, p)]
            return m_new

        return lax.fori_loop(0, n, body, m0)

    m1 = run(tc, qcc_sc, vc_ref, smc_ref, hfc_sc, hbc_sc, ctx_out, jnp.zeros((1, LANES), F32))
    run(t, qc_sc, v_ref, sm_ref, hf_sc, hb_sc, True, m1)

    def finish(n_rows, hf, hb, og, dst):
        blk = min(n_rows, 256)

        def body(i, carry):
            r0 = pl.multiple_of(i * blk, blk)
            res = _sigmoid(og[0, pl.ds(r0, blk), :]) * (hf[pl.ds(r0, blk), :] + hb[pl.ds(r0, blk), :])
            dst[0, pl.ds(r0, blk), :] = res.astype(dst.dtype)
            return carry

        lax.fori_loop(0, n_rows // blk, body, 0)

    finish(t, hf_sc, hb_sc, og_ref, o_ref)
    if ctx_out:
        finish(tc, hfc_sc, hbc_sc, ogc_ref, oc_ref)


def _mlstm(p, pc, conv_w, gate_b, ctx_out):
    b, t, _ = p.shape
    tc = pc.shape[1]
    sel_head, sel_pair = (jnp.asarray(a).astype(BF16) for a in _mlstm_selectors())

    def col(width, idx, rows):
        return pl.BlockSpec((1, rows, width), lambda i: (i, 0, idx))

    full = lambda shape: pl.BlockSpec(shape, lambda i: tuple(0 for _ in shape))
    in_specs = [col(2 * GROUP_W, P_ML_QK, t), col(GROUP_W, P_ML_V, t), col(GROUP_W, P_ML_OG, t),
                col(LANES, P_SMALL, t),
                col(2 * GROUP_W, P_ML_QK, tc), col(GROUP_W, P_ML_V, tc), col(GROUP_W, P_ML_OG, tc),
                col(LANES, P_SMALL, tc),
                full(conv_w.shape), full(gate_b.shape), full(sel_head.shape), full(sel_pair.shape)]
    out_specs = [pl.BlockSpec((1, t, GROUP_W), lambda i: (i, 0, 0))]
    out_shape = [jax.ShapeDtypeStruct((b, t, GROUP_W), BF16)]
    scratch = [pltpu.VMEM((t, 2 * GROUP_W), F32), pltpu.VMEM((tc, 2 * GROUP_W), F32),
               pltpu.VMEM((t, GROUP_W), F32), pltpu.VMEM((t, GROUP_W), F32)]
    if ctx_out:
        out_specs.append(pl.BlockSpec((1, tc, GROUP_W), lambda i: (i, 0, 0)))
        out_shape.append(jax.ShapeDtypeStruct((b, tc, GROUP_W), BF16))
        scratch += [pltpu.VMEM((tc, GROUP_W), F32), pltpu.VMEM((tc, GROUP_W), F32)]
    scratch += [pltpu.VMEM((2 * N_PAIRS, LANES, 2 * LANES), F32)]
    res = pl.pallas_call(
        functools.partial(_mlstm_kernel, ctx_out=ctx_out),
        grid=(b,),
        in_specs=in_specs, out_specs=out_specs, out_shape=out_shape, scratch_shapes=scratch,
        compiler_params=_cparams(("parallel",)),
        name="mlstm",
    )(p, p, p, p, pc, pc, pc, pc, conv_w, gate_b, sel_head, sel_pair)
    return (res[0], res[1]) if ctx_out else (res[0], None)


def _sgu_kernel(u_ref, v_ref, f_ref, ws_ref, bias_ref, dft_ref, o_ref, z_ref):
    t = u_ref.shape[1]
    lane_g = _iota((1, GROUP_W), 1) // HEAD_DIM

    def body(i, carry):
        r0 = pl.multiple_of(i * SGU_CHUNK, SGU_CHUNK)
        v = v_ref[0, pl.ds(r0, SGU_CHUNK), :]
        mu = jnp.mean(v, axis=-1, keepdims=True)
        vc = v - mu
        vn = vc * lax.rsqrt(jnp.mean(vc * vc, axis=-1, keepdims=True) + EPS)
        mixed = bias_ref[...]
        for g in range(N_HEADS):
            mixed = mixed + _bdot(ws_ref[g], jnp.where(lane_g == g, vn, 0.0))
        o_ref[0, pl.ds(r0, SGU_CHUNK), :] = (u_ref[0, pl.ds(r0, SGU_CHUNK), :] * mixed).astype(o_ref.dtype)
        z = _bdot(f_ref[0, pl.ds(r0, SGU_CHUNK), :], dft_ref[...])
        z_ref[0, 0, pl.ds(r0, SGU_CHUNK), :] = z[:, :GROUP_W].astype(z_ref.dtype)
        z_ref[1, 0, pl.ds(r0, SGU_CHUNK), :] = z[:, GROUP_W:].astype(z_ref.dtype)
        return carry

    lax.fori_loop(0, t // SGU_CHUNK, body, 0)


def _sgu_fourier1(p, ws, bias, dft_c):
    b, t, _ = p.shape

    def col(idx):
        return pl.BlockSpec((1, t, GROUP_W), lambda i: (i, 0, idx))

    full = lambda shape: pl.BlockSpec(shape, lambda i: tuple(0 for _ in shape))
    return pl.pallas_call(
        _sgu_kernel,
        grid=(b,),
        in_specs=[col(P_SGU_U), col(P_SGU_V), col(P_FOUR), full(ws.shape), full(bias.shape), full(dft_c.shape)],
        out_specs=[pl.BlockSpec((1, t, GROUP_W), lambda i: (i, 0, 0)),
                   pl.BlockSpec((2, 1, t, GROUP_W), lambda i: (0, i, 0, 0))],
        out_shape=[jax.ShapeDtypeStruct((b, t, GROUP_W), BF16), jax.ShapeDtypeStruct((2, b, t, GROUP_W), BF16)],
        compiler_params=_cparams(("parallel",)),
        name="sgu_fourier1",
    )(p, p, p, ws, bias, dft_c)


def _fourier2_kernel(ct_ref, st_ref, z_ref, o_ref, *, scale):
    for i in range(z_ref.shape[1]):
        acc = jnp.dot(ct_ref[...], z_ref[0, i], preferred_element_type=F32)
        acc -= jnp.dot(st_ref[...], z_ref[1, i], preferred_element_type=F32)
        o_ref[i] = (acc * scale).astype(o_ref.dtype)


def _fourier2(z, cos_t, sin_t):
    _, b, t, w = z.shape
    tm = min(t, 256)
    return pl.pallas_call(
        functools.partial(_fourier2_kernel, scale=float((t * HEAD_DIM) ** -0.5)),
        grid=(t // tm,),
        in_specs=[pl.BlockSpec((tm, t), lambda i: (i, 0)),
                  pl.BlockSpec((tm, t), lambda i: (i, 0)),
                  pl.BlockSpec((2, b, t, w), lambda i: (0, 0, 0, 0), pipeline_mode=pl.Buffered(1))],
        out_specs=pl.BlockSpec((b, tm, w), lambda i: (0, i, 0)),
        out_shape=jax.ShapeDtypeStruct((b, t, w), BF16),
        compiler_params=_cparams(("parallel",)),
        name="fourier2",
    )(cos_t, sin_t, z)


def _dft_tables(n):
    k = np.arange(n, dtype=np.int64)
    ang = 2.0 * np.pi * ((k[:, None] * k[None, :]) % n).astype(np.float64) / n
    return np.cos(ang).astype(np.float32), np.sin(ang).astype(np.float32)


def _channel_dft():
    c, s = _dft_tables(HEAD_DIM)
    eye = np.eye(N_HEADS, dtype=np.float32)
    return np.concatenate([np.kron(eye, c), np.kron(eye, s)], axis=1)


def _route_kernel(x_ref, mod_ref, g_ref, wr_ref, h_ref, mi_ref, mf_ref, cnt_ref, cnt_sc):
    first = (pl.program_id(0) == 0) & (pl.program_id(1) == 0)

    @pl.when(first)
    def _():
        cnt_sc[...] = jnp.zeros_like(cnt_sc)

    tm = x_ref.shape[1]
    h = _norm_mod(x_ref[0], g_ref[...], mod_ref[0, 3:4, :], mod_ref[0, 4:5, :])
    for j in range(D_MODEL // LANES):
        h_ref[pl.ds(j, tm, stride=SUBLANES), :] = h[:, j * LANES:(j + 1) * LANES]
    logits = _fdot(h, wr_ref[...])
    lane = _iota((tm, LANES), 1).astype(F32)
    lg = jnp.where(lane < N_EXPERTS, logits, -jnp.inf)
    m1 = jnp.max(lg, axis=1, keepdims=True)
    i1 = jnp.min(jnp.where(lg == m1, lane, float(LANES)), axis=1, keepdims=True)
    lg2 = jnp.where(lane == i1, -jnp.inf, lg)
    m2 = jnp.max(lg2, axis=1, keepdims=True)
    i2 = jnp.min(jnp.where(lg2 == m2, lane, float(LANES)), axis=1, keepdims=True)
    e = jnp.exp(m2 - m1)
    g0 = 1.0 / (1.0 + e)
    g1 = e / (1.0 + e)
    onehot = jnp.where((lane == i1) | (lane == i2), 1.0, 0.0)
    strict = jnp.where(_iota((tm, tm), 1) < _iota((tm, tm), 0), 1.0, 0.0)
    before = _bdot(strict, onehot) + cnt_sc[...]
    r0 = jnp.sum(jnp.where(lane == i1, before, 0.0), axis=1, keepdims=True)
    r1 = jnp.sum(jnp.where(lane == i2, before, 0.0), axis=1, keepdims=True)
    cnt_sc[...] += jnp.sum(onehot, axis=0, keepdims=True)
    cnt_ref[...] = cnt_sc[...]
    l8 = _iota((tm, SUBLANES), 1)
    mi = jnp.where(l8 == 0, i1, jnp.where(l8 == 1, i2, jnp.where(l8 == 2, r0, r1)))
    mi_ref[...] = mi.astype(jnp.int32)
    mf_ref[...] = jnp.where(l8 == 0, g0, g1)


def _route(x, mod, g, w_router):
    b, t, d = x.shape
    m = b * t
    tm = 512
    nt = t // tm
    wr = jnp.zeros((d, LANES), F32).at[:, :N_EXPERTS].set(w_router)
    return pl.pallas_call(
        _route_kernel,
        grid=(b, nt),
        in_specs=[pl.BlockSpec((1, tm, d), lambda i, j: (i, j, 0)),
                  pl.BlockSpec((1, 6, d), lambda i, j: (i, 0, 0)),
                  pl.BlockSpec((1, d), lambda i, j: (0, 0)),
                  pl.BlockSpec((d, LANES), lambda i, j: (0, 0))],
        out_specs=[pl.BlockSpec((tm * SUBLANES, LANES), lambda i, j: (i * nt + j, 0)),
                   pl.BlockSpec((tm, SUBLANES), lambda i, j: (i * nt + j, 0)),
                   pl.BlockSpec((tm, SUBLANES), lambda i, j: (i * nt + j, 0)),
                   pl.BlockSpec((1, LANES), lambda i, j: (0, 0))],
        out_shape=[jax.ShapeDtypeStruct((m * SUBLANES, LANES), F32),
                   jax.ShapeDtypeStruct((m, SUBLANES), jnp.int32),
                   jax.ShapeDtypeStruct((m, SUBLANES), F32),
                   jax.ShapeDtypeStruct((1, LANES), F32)],
        scratch_shapes=[pltpu.VMEM((1, LANES), F32)],
        compiler_params=_cparams(("arbitrary", "arbitrary")),
        name="route",
    )(x, mod, g.reshape(1, d), wr)


def _moe_kernel(texp_ref, trows_ref, pos0_ref, pos1_ref, h_hbm, wg_ref, wu_ref, wd_ref, y_hbm,
                order_sm, xbuf, x_sc, acc_sc, ybuf, gsem, ssem):
    i = pl.program_id(0)
    j = pl.program_id(1)
    nf = pl.num_programs(1)
    tm = MOE_TM
    n_tok = pos0_ref.shape[0]
    rows = trows_ref[i]

    nt = pl.num_programs(0)
    buf_rows = tm * SUBLANES
    dump = TOP_K * n_tok
    slot = lax.rem(i, 2)
    prev_rows = trows_ref[jnp.maximum(i - 1, 0)]
    next_rows = trows_ref[jnp.minimum(i + 1, nt - 1)]
    n_full, n_rest = divmod(tm, MOE_COPY_ROWS)

    def gather_copy(tile, n_real, r, dst_slot):
        idx = jnp.where(n_real > 0, tile * tm + jnp.minimum(r, n_real - 1), 0)
        tok = jnp.right_shift(order_sm[idx], 1)
        off = pl.multiple_of(dst_slot * buf_rows + r * SUBLANES, SUBLANES)
        return pltpu.make_async_copy(h_hbm.at[tok], xbuf.at[pl.ds(off, SUBLANES), :], gsem)

    def scatter_copy(tile, n_real, r, src_slot):
        idx = jnp.where(n_real > 0, tile * tm + jnp.minimum(r, n_real - 1), 0)
        dst = jnp.where(r < n_real, order_sm[idx], dump + r)
        off = pl.multiple_of(src_slot * buf_rows + r * SUBLANES, SUBLANES)
        return pltpu.make_async_copy(ybuf.at[pl.ds(off, SUBLANES), :], y_hbm.at[dst], ssem)

    def wait_all(buf, sem):
        pltpu.make_async_copy(buf.at[pl.ds(0, buf_rows), :], buf.at[pl.ds(buf_rows, buf_rows), :], sem).wait()

    @pl.when((i == 0) & (j == 0))
    def _():
        def inv_body(t, carry):
            order_sm[pos0_ref[t]] = 2 * t
            order_sm[pos1_ref[t]] = 2 * t + 1
            return carry

        lax.fori_loop(0, n_tok, inv_body, 0, unroll=8)
        ybuf[...] = jnp.zeros_like(ybuf)

        def start(r, carry):
            gather_copy(0, rows, r, 0).start()
            return carry

        lax.fori_loop(0, tm, start, 0)

    @pl.when((j == 0) & (rows > 0))
    def _():
        wait_all(xbuf, gsem)
        base = pl.multiple_of(slot * buf_rows, SUBLANES)
        for c in range(D_MODEL // LANES):
            x_sc[:, c * LANES:(c + 1) * LANES] = xbuf[pl.ds(base + c, tm, stride=SUBLANES), :].astype(BF16)
        acc_sc[...] = jnp.zeros_like(acc_sc)

    def step(n_copies):
        r0 = j * MOE_COPY_ROWS
        for q in range(n_copies):
            gather_copy(i + 1, next_rows, r0 + q, 1 - slot).start()
            scatter_copy(i - 1, jnp.where(i > 0, prev_rows, 0), r0 + q, 1 - slot).start()
        x = x_sc[...]
        gate = jnp.dot(x, wg_ref[0].astype(BF16), preferred_element_type=F32)
        up = jnp.dot(x, wu_ref[0].astype(BF16), preferred_element_type=F32)
        acc_sc[...] += jnp.dot((_silu(gate) * up).astype(BF16), wd_ref[0].astype(BF16),
                               preferred_element_type=F32)

    pl.when((rows > 0) & (j < n_full))(functools.partial(step, MOE_COPY_ROWS))
    pl.when((rows > 0) & (j == n_full))(functools.partial(step, n_rest))
    pl.when((rows > 0) & (j > n_full))(functools.partial(step, 0))

    @pl.when((j == nf - 1) & (rows > 0))
    def _():
        wait_all(ybuf, ssem)
        base = pl.multiple_of(slot * buf_rows, SUBLANES)
        for c in range(D_MODEL // LANES):
            ybuf[pl.ds(base + c, tm, stride=SUBLANES), :] = acc_sc[:, c * LANES:(c + 1) * LANES]

    @pl.when((j == 0) & (rows == 0) & (i > 0) & (prev_rows > 0))
    def _():
        wait_all(xbuf, gsem)

        def start(r, carry):
            scatter_copy(i - 1, prev_rows, r, 1 - slot).start()
            return carry

        lax.fori_loop(0, tm, start, 0)
        wait_all(ybuf, ssem)


def _moe(h_tiles, pos0, pos1, tile_expert, tile_rows, wg, wu, wd):
    m = h_tiles.shape[0]
    n_exp, d, f = wg.shape
    nt = tile_expert.shape[0]
    tm, tf = MOE_TM, MOE_TF
    nf = f // tf
    assert nf * MOE_COPY_ROWS >= tm, "each tile needs enough column steps to issue its neighbours' row copies"
    assert nt * tm > TOP_K * m + n_exp * (tm - 1), "the last tile must be unused: it drains the copy pipeline"

    def col_tile(i, j, tr):
        return jnp.where(tr[i] > 0, j, nf - 1)

    grid_spec = pltpu.PrefetchScalarGridSpec(
        num_scalar_prefetch=4,
        grid=(nt, nf),
        in_specs=[pl.BlockSpec(memory_space=pl.ANY),
                  pl.BlockSpec((1, d, tf), lambda i, j, te, tr, p0, p1: (te[i], 0, col_tile(i, j, tr))),
                  pl.BlockSpec((1, d, tf), lambda i, j, te, tr, p0, p1: (te[i], 0, col_tile(i, j, tr))),
                  pl.BlockSpec((1, tf, d), lambda i, j, te, tr, p0, p1: (te[i], col_tile(i, j, tr), 0))],
        out_specs=pl.BlockSpec(memory_space=pl.ANY),
        scratch_shapes=[pltpu.SMEM((nt * tm,), jnp.int32),
                        pltpu.VMEM((2 * tm * SUBLANES, LANES), F32),
                        pltpu.VMEM((tm, d), BF16),
                        pltpu.VMEM((tm, d), F32),
                        pltpu.VMEM((2 * tm * SUBLANES, LANES), F32),
                        pltpu.SemaphoreType.DMA(()),
                        pltpu.SemaphoreType.DMA(())],
    )
    return pl.pallas_call(
        _moe_kernel,
        grid_spec=grid_spec,
        out_shape=jax.ShapeDtypeStruct((TOP_K * m + tm, SUBLANES, LANES), F32),
        compiler_params=_cparams(("arbitrary", "arbitrary")),
        name="moe",
    )(tile_expert, tile_rows, pos0, pos1, h_tiles, wg, wu, wd)


def _combine_kernel(x_ref, mod_ref, y_ref, mf_ref, fg_ref, o_ref):
    tm = x_ref.shape[1]
    stride = TOP_K * SUBLANES
    g0 = mf_ref[:, 0:1]
    g1 = mf_ref[:, 1:2]
    parts = []
    for c in range(D_MODEL // LANES):
        y0 = y_ref[pl.ds(c, tm, stride=stride), :]
        y1 = y_ref[pl.ds(SUBLANES + c, tm, stride=stride), :]
        parts.append(g0 * y0 + g1 * y1)
    moe = jnp.concatenate(parts, axis=1)
    x = x_ref[0] + mod_ref[0, 5:6, :] * moe
    ms = jnp.mean(x * x, axis=-1, keepdims=True)
    o_ref[0] = x * lax.rsqrt(ms + EPS) * fg_ref[...]


def _combine(x, mod, y, mf, final_g):
    b, t, d = x.shape
    tm = 256
    nt = t // tm
    rows = tm * TOP_K * SUBLANES
    return pl.pallas_call(
        _combine_kernel,
        grid=(b, nt),
        in_specs=[pl.BlockSpec((1, tm, d), lambda i, j: (i, j, 0)),
                  pl.BlockSpec((1, 6, d), lambda i, j: (i, 0, 0)),
                  pl.BlockSpec((rows, LANES), lambda i, j: (i * nt + j, 0)),
                  pl.BlockSpec((tm, SUBLANES), lambda i, j: (i * nt + j, 0)),
                  pl.BlockSpec((1, d), lambda i, j: (0, 0))],
        out_specs=pl.BlockSpec((1, tm, d), lambda i, j: (i, j, 0)),
        out_shape=jax.ShapeDtypeStruct((b, t, d), F32),
        compiler_params=_cparams(("parallel", "parallel")),
        name="combine",
    )(x, mod, y, mf, final_g.reshape(1, d))


def _moe_layer(x, mod, norm_g, w_router, wg, wu, wd, final_g):
    b, t, d = x.shape
    m = b * t
    h_tiles, mi, mf, cnt = _route(x, mod, norm_g, w_router)
    counts = cnt[0, :N_EXPERTS].astype(jnp.int32)
    tiles_per = (counts + MOE_TM - 1) // MOE_TM
    tile_end = jnp.cumsum(tiles_per)
    tile_start = tile_end - tiles_per
    seg_start = tile_start * MOE_TM
    nt = (TOP_K * m) // MOE_TM + N_EXPERTS
    tile_id = jnp.arange(nt, dtype=jnp.int32)
    used_id = jnp.minimum(tile_id, tile_end[-1] - 1)
    tile_expert = jnp.sum(used_id[:, None] >= tile_end[None, :], axis=1).astype(jnp.int32)
    tile_rows = jnp.clip(counts[tile_expert] - (tile_id - tile_start[tile_expert]) * MOE_TM, 0, MOE_TM)
    tile_rows = jnp.where(tile_id < tile_end[-1], tile_rows, 0).astype(jnp.int32)
    pos0 = seg_start[mi[:, 0]] + mi[:, 2]
    pos1 = seg_start[mi[:, 1]] + mi[:, 3]
    y = _moe(h_tiles.reshape(m, SUBLANES, LANES), pos0, pos1, tile_expert, tile_rows, wg, wu, wd)
    return _combine(x, mod, y.reshape(-1, LANES), mf, final_g)


def _permute_w_in(w):
    d = w.shape[0]
    return jnp.concatenate([w[:, 0:768], w[:, 1312:1568], w[:, 800:1312], w[:, 1568:1824], w[:, 1840:2608],
                            w[:, 768:800], w[:, 1824:1840], jnp.zeros((d, P_WIDTH - 2608), w.dtype)],
                           axis=1).astype(BF16)


def _mixers(p, pc, gla_w_lr, gla_b_lr, gla_norm_g, mlstm_conv, mlstm_gate_b, sgu_w, sgu_b, ctx_out, tables):
    wlr = jnp.zeros((LANES, 2 * GLA_QK_W), F32)
    wlr = wlr.at[0:GLA_RANK, 0:GLA_QK_W].set(gla_w_lr[0]).at[GLA_RANK:2 * GLA_RANK, GLA_QK_W:].set(gla_w_lr[1])
    blr = gla_b_lr.reshape(1, 2 * GLA_QK_W)
    gla, gla_c = _gla(p, pc, wlr, blr, gla_norm_g.reshape(1, GROUP_W), ctx_out)
    conv_w = mlstm_conv.reshape(9, 2 * GROUP_W)
    gate_b = jnp.zeros((1, LANES), F32).at[0, SMALL_GATE_LANE:SMALL_GATE_LANE + 4 * N_HEADS].set(
        mlstm_gate_b.reshape(-1))
    ml, ml_c = _mlstm(p, pc, conv_w, gate_b, ctx_out)
    bias = jnp.repeat(sgu_b.T, HEAD_DIM, axis=1)
    dft_c, tabs = tables
    sg, z = _sgu_fourier1(p, sgu_w, bias, dft_c)
    fo = _fourier2(z, *tabs[p.shape[1]])
    mix = (gla, ml, sg, fo)
    if not ctx_out:
        return mix, None
    sg_c, z_c = _sgu_fourier1(pc, sgu_w, bias, dft_c)
    fo_c = _fourier2(z_c, *tabs[pc.shape[1]])
    return mix, (gla_c, ml_c, sg_c, fo_c)


def kernel(x, c, ctx, c_ctx, w_ada, b_ada, norm_mix_g, norm_ffn_g, w_in, w_out, gla_w_lr, gla_b_lr, gla_norm_g,
           mlstm_conv, mlstm_gate_b, sgu_w, sgu_b, ffn_w_gate, ffn_w_up, ffn_w_down, moe_router, moe_w_gate,
           moe_w_up, moe_w_down, final_norm_g):
    depth = w_ada.shape[0]
    assert depth == 2, "layer 0 is the dense layer with context output, layer 1 the MoE layer"
    b, t, d = x.shape
    tc = ctx.shape[1]
    cvec = jnp.zeros((16, d), F32).at[:b].set(c).at[b].set(c_ctx)
    mods = _ada(cvec, w_ada, b_ada).reshape(depth, 16, 6, d)
    tables = (jnp.asarray(_channel_dft()).astype(BF16),
              {n: tuple(jnp.asarray(a).astype(BF16) for a in _dft_tables(n)) for n in {t, tc}})
    xc = ctx
    for l in range(depth):
        ctx_out = l < depth - 1
        mod, mod_c = mods[l, :b], mods[l, b:b + 1]
        w_in_l = _permute_w_in(w_in[l])
        w_out_l = w_out[l].astype(BF16)
        p = _in_proj(x, mod, norm_mix_g[l], w_in_l)
        pc = _in_proj(xc, mod_c, norm_mix_g[l], w_in_l)
        mix, mix_c = _mixers(p, pc, gla_w_lr[l], gla_b_lr[l], gla_norm_g[l], mlstm_conv[l], mlstm_gate_b[l],
                             sgu_w[l], sgu_b[l], ctx_out, tables)
        x = _out_proj(x, mod, mix, w_out_l)
        if ctx_out:
            xc = _out_proj(xc, mod_c, mix_c, w_out_l)
        i = l // 2
        if l % 2 == 0:
            wg, wu, wd = ffn_w_gate[i].astype(BF16), ffn_w_up[i].astype(BF16), ffn_w_down[i].astype(BF16)
            x = _ffn(x, mod, norm_ffn_g[l], wg, wu, wd)
            if ctx_out:
                xc = _ffn(xc, mod_c, norm_ffn_g[l], wg, wu, wd)
        else:
            x = _moe_layer(x, mod, norm_ffn_g[l], moe_router[i], moe_w_gate[i], moe_w_up[i], moe_w_down[i],
                           final_norm_g)
    return x
```

```python
import functools

import numpy as np
import jax
import jax.numpy as jnp
from jax import lax
from jax.experimental import pallas as pl
from jax.experimental.pallas import tpu as pltpu

F32 = jnp.float32
BF16 = jnp.bfloat16
HIGHEST = lax.Precision.HIGHEST

D_MODEL = 1024
GROUP_W = 256
N_HEADS = 4
HEAD_DIM = 64
GLA_DK = 32
GLA_QK_W = N_HEADS * GLA_DK
GLA_RANK = 16
GLA_NORMALIZER = 16.0
GLA_BLOCK = 128
GRID_W = 64
SGU_CHUNK = 128
N_EXPERTS = 8
TOP_K = 2
EPS = 1e-6

LANES = 128
SUBLANES = 8
VMEM_LIMIT = 52 * 1024 * 1024

P_GLA_QK, P_GLA_V, P_GLA_G, P_ML_V = 0, 1, 2, 3
P_ML_QK = 2
P_ML_OG, P_SGU_U, P_SGU_V, P_FOUR = 6, 7, 8, 9
P_MAIN_W = 2560
P_WIDTH = 2688
SMALL_GATE_LANE = 32

MOE_TM = 1024
MOE_TF = 512
MOE_COPY_ROWS = 171


def _cparams(sem, **kw):
    return pltpu.CompilerParams(dimension_semantics=sem, vmem_limit_bytes=VMEM_LIMIT, **kw)


def _sigmoid(x):
    return 1.0 / (1.0 + jnp.exp(-x))


def _silu(x):
    return x * _sigmoid(x)


def _log_sigmoid(x):
    return jnp.minimum(x, 0.0) - jnp.log(1.0 + jnp.exp(-jnp.abs(x)))


def _norm_mod(x, g, shift, scale):
    ms = jnp.mean(x * x, axis=-1, keepdims=True)
    return (x * lax.rsqrt(ms + EPS) * g) * (1.0 + scale) + shift


def _bdot(a, b):
    return jnp.dot(a.astype(BF16), b.astype(BF16), preferred_element_type=F32)


def _bdot_nt(a, b):
    return lax.dot_general(a.astype(BF16), b.astype(BF16), (((1,), (1,)), ((), ())), preferred_element_type=F32)


def _fdot(a, b):
    return jnp.dot(a, b, precision=HIGHEST, preferred_element_type=F32)


def _split3(x):
    hi = x.astype(BF16)
    r = x - hi.astype(F32)
    mid = r.astype(BF16)
    lo = (r - mid.astype(F32)).astype(BF16)
    return hi, mid, lo


def _dot3(a, b):
    a_hi, a_lo, _ = _split3(a)
    b_hi, b_lo, _ = _split3(b)
    dot = functools.partial(jnp.dot, preferred_element_type=F32)
    return dot(a_hi, b_hi) + dot(a_lo, b_hi) + dot(a_hi, b_lo)


def _tri_dot(tri, x):
    n = x.shape[1]
    y = jnp.dot(tri.astype(BF16), jnp.concatenate(_split3(x), axis=1), preferred_element_type=F32)
    return y[:, :n] + y[:, n:2 * n] + y[:, 2 * n:]


def _ada_kernel(c_ref, w_ref, b_ref, o_ref):
    o_ref[0] = _fdot(_silu(c_ref[...]), w_ref[0]) + b_ref[0]


def _ada(cvec, w_ada, b_ada):
    depth, d, n = w_ada.shape
    rows = cvec.shape[0]
    tn = 1536
    return pl.pallas_call(
        _ada_kernel,
        grid=(depth, n // tn),
        in_specs=[pl.BlockSpec((rows, d), lambda l, j: (0, 0)),
                  pl.BlockSpec((1, d, tn), lambda l, j: (l, 0, j)),
                  pl.BlockSpec((1, 1, tn), lambda l, j: (l, 0, j))],
        out_specs=pl.BlockSpec((1, rows, tn), lambda l, j: (l, 0, j)),
        out_shape=jax.ShapeDtypeStruct((depth, rows, n), F32),
        compiler_params=_cparams(("parallel", "parallel")),
        name="ada",
    )(cvec, w_ada, b_ada.reshape(depth, 1, n))


def _in_kernel(x_ref, mod_ref, g_ref, w_ref, o_ref, os_ref):
    h = _norm_mod(x_ref[0], g_ref[...], mod_ref[0, 0:1, :], mod_ref[0, 1:2, :])
    acc = jnp.dot(h.astype(BF16), w_ref[...], preferred_element_type=F32)
    o_ref[0] = acc[:, :P_MAIN_W].astype(o_ref.dtype)
    os_ref[0] = acc[:, P_MAIN_W:]


def _in_proj(x, mod, g, w):
    b, t, d = x.shape
    n = w.shape[1]
    tm = min(t, 512)
    mod_map = (lambda i, j: (i, 0, 0)) if mod.shape[0] == b else (lambda i, j: (0, 0, 0))
    return pl.pallas_call(
        _in_kernel,
        grid=(b, t // tm),
        in_specs=[pl.BlockSpec((1, tm, d), lambda i, j: (i, j, 0)),
                  pl.BlockSpec((1, 6, d), mod_map),
                  pl.BlockSpec((1, d), lambda i, j: (0, 0)),
                  pl.BlockSpec((d, n), lambda i, j: (0, 0))],
        out_specs=[pl.BlockSpec((1, tm, P_MAIN_W), lambda i, j: (i, j, 0)),
                   pl.BlockSpec((1, tm, n - P_MAIN_W), lambda i, j: (i, j, 0))],
        out_shape=[jax.ShapeDtypeStruct((b, t, P_MAIN_W), BF16), jax.ShapeDtypeStruct((b, t, n - P_MAIN_W), F32)],
        compiler_params=_cparams(("parallel", "parallel")),
        name="in_proj",
    )(x, mod, g.reshape(1, d), w)


def _out_kernel(x_ref, mod_ref, a_ref, b_ref, c_ref, d_ref, w_ref, o_ref):
    acc = jnp.dot(a_ref[0], w_ref[0:GROUP_W, :], preferred_element_type=F32)
    acc += jnp.dot(b_ref[0], w_ref[GROUP_W:2 * GROUP_W, :], preferred_element_type=F32)
    acc += jnp.dot(c_ref[0], w_ref[2 * GROUP_W:3 * GROUP_W, :], preferred_element_type=F32)
    acc += jnp.dot(d_ref[0], w_ref[3 * GROUP_W:4 * GROUP_W, :], preferred_element_type=F32)
    o_ref[0] = x_ref[0] + mod_ref[0, 2:3, :] * acc


def _out_proj(x, mod, mixes, w):
    b, t, d = x.shape
    tm = min(t, 512)
    mod_map = (lambda i, j: (i, 0, 0)) if mod.shape[0] == b else (lambda i, j: (0, 0, 0))
    mix_spec = pl.BlockSpec((1, tm, GROUP_W), lambda i, j: (i, j, 0))
    return pl.pallas_call(
        _out_kernel,
        grid=(b, t // tm),
        in_specs=[pl.BlockSpec((1, tm, d), lambda i, j: (i, j, 0)),
                  pl.BlockSpec((1, 6, d), mod_map),
                  mix_spec, mix_spec, mix_spec, mix_spec,
                  pl.BlockSpec((d, d), lambda i, j: (0, 0))],
        out_specs=pl.BlockSpec((1, tm, d), lambda i, j: (i, j, 0)),
        out_shape=jax.ShapeDtypeStruct((b, t, d), F32),
        compiler_params=_cparams(("parallel", "parallel")),
        name="out_proj",
    )(x, mod, *mixes, w)


def _ffn_kernel(x_ref, mod_ref, g_ref, wg_ref, wu_ref, wd_ref, o_ref, h_sc, acc_sc):
    j = pl.program_id(2)

    @pl.when(j == 0)
    def _():
        h = _norm_mod(x_ref[0], g_ref[...], mod_ref[0, 3:4, :], mod_ref[0, 4:5, :])
        h_sc[...] = h.astype(BF16)
        acc_sc[...] = jnp.zeros_like(acc_sc)

    h = h_sc[...]
    gate = jnp.dot(h, wg_ref[...], preferred_element_type=F32)
    up = jnp.dot(h, wu_ref[...], preferred_element_type=F32)
    acc_sc[...] += jnp.dot((_silu(gate) * up).astype(BF16), wd_ref[...], preferred_element_type=F32)

    @pl.when(j == pl.num_programs(2) - 1)
    def _():
        o_ref[0] = x_ref[0] + mod_ref[0, 5:6, :] * acc_sc[...]


def _ffn(x, mod, g, wg, wu, wd):
    b, t, d = x.shape
    f = wg.shape[1]
    tm = min(t, 512)
    tf = f // 2
    mod_map = (lambda i, j, k: (i, 0, 0)) if mod.shape[0] == b else (lambda i, j, k: (0, 0, 0))
    return pl.pallas_call(
        _ffn_kernel,
        grid=(b, t // tm, f // tf),
        in_specs=[pl.BlockSpec((1, tm, d), lambda i, j, k: (i, j, 0)),
                  pl.BlockSpec((1, 6, d), mod_map),
                  pl.BlockSpec((1, d), lambda i, j, k: (0, 0)),
                  pl.BlockSpec((d, tf), lambda i, j, k: (0, k)),
                  pl.BlockSpec((d, tf), lambda i, j, k: (0, k)),
                  pl.BlockSpec((tf, d), lambda i, j, k: (k, 0))],
        out_specs=pl.BlockSpec((1, tm, d), lambda i, j, k: (i, j, 0)),
        out_shape=jax.ShapeDtypeStruct((b, t, d), F32),
        scratch_shapes=[pltpu.VMEM((tm, d), BF16), pltpu.VMEM((tm, d), F32)],
        compiler_params=_cparams(("parallel", "parallel", "arbitrary")),
        name="ffn",
    )(x, mod, g.reshape(1, d), wg, wu, wd)


def _iota(shape, dim):
    return lax.broadcasted_iota(jnp.int32, shape, dim)


def _gla_chunk(q, k, v, la, s_ref, lower, want_out):
    L = q.shape[0]
    r, c = _iota((L, L), 0), _iota((L, L), 1)
    keep = (c <= r) if lower else (c >= r)
    cum = _tri_dot(jnp.where(keep, 1.0, 0.0), la)
    tot_row = jnp.sum(la, axis=0, keepdims=True)
    s_prev = s_ref[...]
    out = None
    if want_out:
        mid = L // 2 - 1 if lower else L // 2
        ref = cum[mid:mid + 1, :]
        qr = q * jnp.exp(cum - ref)
        kc = (k * jnp.exp(ref - cum)).astype(BF16)
        lane_q = _iota((1, GLA_QK_W), 1) // GLA_DK
        lane_v = _iota((1, GROUP_W), 1) // HEAD_DIM
        out = _bdot_nt(q * jnp.exp(cum), s_prev)
        for h in range(N_HEADS):
            sc = jnp.where(keep, _bdot_nt(jnp.where(lane_q == h, qr, 0.0), kc), 0.0)
            out = out + _bdot(sc, jnp.where(lane_v == h, v, 0.0))
    ke = k * jnp.exp(tot_row - cum)
    upd = jnp.dot(v.T.astype(BF16), ke.astype(BF16), preferred_element_type=F32)
    bd = (_iota((GROUP_W, GLA_QK_W), 0) // HEAD_DIM) == (_iota((GROUP_W, GLA_QK_W), 1) // GLA_DK)
    s_ref[...] = jnp.exp(tot_row) * s_prev + jnp.where(bd, upd, 0.0)
    return out


def _gla_kernel(qk_ref, v_ref, g_ref, sm_ref, qkc_ref, vc_ref, gc_ref, smc_ref, wlr_ref, blr_ref, ng_ref,
                *rest, ctx_out):
    if ctx_out:
        o_ref, oc_ref, la_sc, lac_sc, of_sc, ob_sc, ofc_sc, obc_sc, sf_sc, sb_sc = rest
    else:
        o_ref, la_sc, lac_sc, of_sc, ob_sc, sf_sc, sb_sc = rest
        oc_ref = ofc_sc = obc_sc = None
    t = qk_ref.shape[1]
    tc = qkc_ref.shape[1]
    qscale = GLA_DK ** -0.5

    def log_decay(sm):
        z = _dot3(sm, wlr_ref[...]) + blr_ref[...]
        return _log_sigmoid(z) * (1.0 / GLA_NORMALIZER)

    lac_sc[...] = log_decay(smc_ref[0])
    rows_blk = 256

    def la_body(i, carry):
        r0 = pl.multiple_of(i * rows_blk, rows_blk)
        la_sc[pl.ds(r0, rows_blk), :] = log_decay(sm_ref[0, pl.ds(r0, rows_blk), :])
        return carry

    lax.fori_loop(0, t // rows_blk, la_body, 0)
    sf_sc[...] = jnp.zeros_like(sf_sc)
    sb_sc[...] = jnp.zeros_like(sb_sc)

    def run(n_rows, qk, v, la, of, ob, want_out):
        n = n_rows // GLA_BLOCK

        def body(i, carry):
            rf = pl.multiple_of(i * GLA_BLOCK, GLA_BLOCK)
            rb = pl.multiple_of((n - 1 - i) * GLA_BLOCK, GLA_BLOCK)
            for r0, s_ref, lower, dst, lane0 in ((rf, sf_sc, True, of, 0), (rb, sb_sc, False, ob, GLA_QK_W)):
                qkb = qk[0, pl.ds(r0, GLA_BLOCK), :].astype(F32)
                out = _gla_chunk(qkb[:, :GLA_QK_W] * qscale, qkb[:, GLA_QK_W:], v[0, pl.ds(r0, GLA_BLOCK), :].astype(F32),
                                 la[pl.ds(r0, GLA_BLOCK), lane0:lane0 + GLA_QK_W], s_ref, lower, want_out)
                if want_out:
                    dst[pl.ds(r0, GLA_BLOCK), :] = out
            return carry

        lax.fori_loop(0, n, body, 0)

    run(tc, qkc_ref, vc_ref, lac_sc, ofc_sc, obc_sc, ctx_out)
    run(t, qk_ref, v_ref, la_sc, of_sc, ob_sc, True)

    grp = jnp.where((_iota((GROUP_W, GROUP_W), 0) // HEAD_DIM) == (_iota((GROUP_W, GROUP_W), 1) // HEAD_DIM),
                    1.0 / HEAD_DIM, 0.0).astype(BF16)

    def finish(n_rows, of, ob, g, dst):
        blk = min(n_rows, 256)

        def body(i, carry):
            r0 = pl.multiple_of(i * blk, blk)
            o = of[pl.ds(r0, blk), :] + ob[pl.ds(r0, blk), :]
            sq_hi, sq_lo, _ = _split3(o * o)
            ms = (jnp.dot(sq_hi, grp, preferred_element_type=F32) + jnp.dot(sq_lo, grp, preferred_element_type=F32))
            res = o * lax.rsqrt(ms + EPS) * ng_ref[...] * _silu(g[0, pl.ds(r0, blk), :].astype(F32))
            dst[0, pl.ds(r0, blk), :] = res.astype(dst.dtype)
            return carry

        lax.fori_loop(0, n_rows // blk, body, 0)

    finish(t, of_sc, ob_sc, g_ref, o_ref)
    if ctx_out:
        finish(tc, ofc_sc, obc_sc, gc_ref, oc_ref)


def _gla(p, ps, pc, pcs, wlr, blr, ng, ctx_out):
    b, t, _ = p.shape
    tc = pc.shape[1]

    def col(width, idx, rows):
        return pl.BlockSpec((1, rows, width), lambda i: (i, 0, idx))

    full = lambda shape: pl.BlockSpec(shape, lambda i: tuple(0 for _ in shape))
    in_specs = [col(GROUP_W, P_GLA_QK, t), col(GROUP_W, P_GLA_V, t), col(GROUP_W, P_GLA_G, t), col(LANES, 0, t),
                col(GROUP_W, P_GLA_QK, tc), col(GROUP_W, P_GLA_V, tc), col(GROUP_W, P_GLA_G, tc),
                col(LANES, 0, tc),
                full(wlr.shape), full(blr.shape), full(ng.shape)]
    out_specs = [pl.BlockSpec((1, t, GROUP_W), lambda i: (i, 0, 0))]
    out_shape = [jax.ShapeDtypeStruct((b, t, GROUP_W), BF16)]
    scratch = [pltpu.VMEM((t, 2 * GLA_QK_W), F32), pltpu.VMEM((tc, 2 * GLA_QK_W), F32),
               pltpu.VMEM((t, GROUP_W), F32), pltpu.VMEM((t, GROUP_W), F32)]
    if ctx_out:
        out_specs.append(pl.BlockSpec((1, tc, GROUP_W), lambda i: (i, 0, 0)))
        out_shape.append(jax.ShapeDtypeStruct((b, tc, GROUP_W), BF16))
        scratch += [pltpu.VMEM((tc, GROUP_W), F32), pltpu.VMEM((tc, GROUP_W), F32)]
    scratch += [pltpu.VMEM((GROUP_W, GLA_QK_W), F32), pltpu.VMEM((GROUP_W, GLA_QK_W), F32)]
    res = pl.pallas_call(
        functools.partial(_gla_kernel, ctx_out=ctx_out),
        grid=(b,),
        in_specs=in_specs, out_specs=out_specs, out_shape=out_shape, scratch_shapes=scratch,
        compiler_params=_cparams(("parallel",)),
        name="gla",
    )(p, p, p, ps, pc, pc, pc, pcs, wlr, blr, ng)
    return (res[0], res[1]) if ctx_out else (res[0], None)


ML_BLOCK = 128
N_PAIRS = N_HEADS // 2


def _gate_lane(d, h):
    return SMALL_GATE_LANE + 8 * d + h


def _mlstm_selectors():
    sel_head = np.zeros((LANES, 2 * N_HEADS * LANES), np.float32)
    sel_pair = np.zeros((LANES, 2 * N_PAIRS * LANES), np.float32)
    for d in range(2):
        for h in range(N_HEADS):
            sel_head[_gate_lane(d, h), (d * N_HEADS + h) * LANES:(d * N_HEADS + h + 1) * LANES] = 1.0
            p, hh = divmod(h, 2)
            lo = (d * N_PAIRS + p) * LANES + hh * HEAD_DIM
            sel_pair[_gate_lane(d, h), lo:lo + HEAD_DIM] = 1.0
    return sel_head, sel_pair


def _mlstm_block(blk_f, blk_b, cn_ref, m_prev, selh_ref, selp_ref, want_out):
    L = ML_BLOCK
    lane = _iota((1, LANES), 1)
    is_b = (lane >= _gate_lane(1, 0)) & (lane < _gate_lane(1, 0) + N_HEADS)
    valid = ((lane >= _gate_lane(0, 0)) & (lane < _gate_lane(0, 0) + N_HEADS)) | is_b
    r, c = _iota((L, L), 0), _iota((L, L), 1)
    keep_f, keep_b = c <= r, c >= r
    lf_f = pltpu.roll(_log_sigmoid(blk_f[2]), LANES - 4, 1)
    lf_b = pltpu.roll(_log_sigmoid(blk_b[2]), LANES - 4, 1)
    cum_f = _tri_dot(jnp.where(keep_f, 1.0, 0.0), lf_f)
    cum_b = _tri_dot(jnp.where(keep_b, 1.0, 0.0), lf_b)
    cum = jnp.where(valid, jnp.where(is_b, cum_b, cum_f), 0.0)
    ig = jnp.where(valid, jnp.where(is_b, blk_b[2], blk_f[2]), 0.0)
    tot = jnp.where(valid, jnp.where(is_b, cum_b[0:1, :], cum_f[L - 1:L, :]), 0.0)
    b = ig - cum
    row = _iota((L, 1), 0)
    pm = b
    k = 1
    while k < L:
        from_prev = jnp.where(row >= k, pltpu.roll(pm, k, 0), -jnp.inf)
        from_next = jnp.where(row < L - k, pltpu.roll(pm, L - k, 0), -jnp.inf)
        pm = jnp.maximum(pm, jnp.where(is_b, from_next, from_prev))
        k *= 2
    mm = jnp.maximum(m_prev, pm)
    w_end = tot + b
    m_chunk = jnp.max(w_end, axis=0, keepdims=True)
    m_new = jnp.maximum(tot + m_prev, m_chunk)
    a_row = jnp.exp(tot + m_prev - m_new)
    g_row = jnp.exp(m_chunk - m_new)
    cols = [jnp.exp(w_end - m_chunk)]
    if want_out:
        cols += [jnp.exp(m_prev - mm), jnp.exp(-(cum + mm))]
        mm_hi, mm_lo, _ = _split3(mm)
        mm_t = (jnp.dot(mm_hi, selh_ref[...], preferred_element_type=F32)
                + jnp.dot(mm_lo, selh_ref[...], preferred_element_type=F32))
        b_rows = b.T
    pair_t = _bdot(jnp.concatenate(cols, axis=0), selp_ref[...])
    lo_half = lane < HEAD_DIM
    ones_t = jnp.ones((L, LANES), F32)
    bd = (_iota((LANES, 2 * LANES), 0) // HEAD_DIM) == ((_iota((LANES, 2 * LANES), 1) % LANES) // HEAD_DIM)
    outs = {}
    for d, (blk, keep) in enumerate(((blk_f, keep_f), (blk_b, keep_b))):
        qk, v, _ = blk
        for p in range(N_PAIRS):
            qp = qk[:, p * LANES:(p + 1) * LANES]
            kp = qk[:, GROUP_W + p * LANES:GROUP_W + (p + 1) * LANES]
            vp = v[:, p * LANES:(p + 1) * LANES]
            cn = cn_ref[d * N_PAIRS + p]
            tcol = (d * N_PAIRS + p) * LANES
            if want_out:
                nd = None
                for hh in range(2):
                    h = 2 * p + hh
                    mine = lo_half if hh == 0 else jnp.logical_not(lo_half)
                    gl = _gate_lane(d, h)
                    hcol = (d * N_HEADS + h) * LANES
                    e = jnp.where(keep, jnp.exp(b_rows[gl:gl + 1, :] - mm_t[:, hcol:hcol + LANES]), 0.0)
                    s = e * _bdot_nt(jnp.where(mine, qp, 0.0), kp)
                    va = jnp.concatenate([jnp.where(mine, vp, 0.0), jnp.where(mine, ones_t, 0.0)], axis=1)
                    t = _bdot(s, va)
                    nd = t if nd is None else nd + t
                w_int = pair_t[L:2 * L, tcol:tcol + LANES]
                nd = nd + jnp.concatenate([w_int, w_int], axis=1) * _bdot(qp, cn)
                outs[(d, p)] = nd[:, :LANES] / jnp.maximum(jnp.abs(nd[:, LANES:]),
                                                           pair_t[2 * L:3 * L, tcol:tcol + LANES])
            kw_t = (kp * pair_t[0:L, tcol:tcol + LANES]).T
            upd = _bdot(kw_t, jnp.concatenate([vp, ones_t], axis=1))
            ga, gb = _gate_lane(d, 2 * p), _gate_lane(d, 2 * p + 1)
            half = (HEAD_DIM, 2 * LANES)
            a_t = jnp.concatenate([jnp.broadcast_to(a_row[:, ga:ga + 1], half),
                                   jnp.broadcast_to(a_row[:, gb:gb + 1], half)], axis=0)
            g_t = jnp.concatenate([jnp.broadcast_to(g_row[:, ga:ga + 1], half),
                                   jnp.broadcast_to(g_row[:, gb:gb + 1], half)], axis=0)
            cn_ref[d * N_PAIRS + p] = a_t * cn + g_t * jnp.where(bd, upd, 0.0)
    return (outs if want_out else None), m_new


def _mlstm_kernel(qk_ref, v_ref, og_ref, sm_ref, qkc_ref, vc_ref, ogc_ref, smc_ref, cw_ref, gb_ref,
                  selh_ref, selp_ref, *rest, ctx_out):
    if ctx_out:
        o_ref, oc_ref, qc_sc, qcc_sc, hf_sc, hb_sc, hfc_sc, hbc_sc, cn_sc = rest
    else:
        o_ref, qc_sc, qcc_sc, hf_sc, hb_sc, cn_sc = rest
        oc_ref = hfc_sc = hbc_sc = None
    t = qk_ref.shape[1]
    tc = qkc_ref.shape[1]
    w2 = 2 * GROUP_W
    kscale = HEAD_DIM ** -0.5
    n_rows_grid = t // GRID_W

    def tap(dr, dc):
        return cw_ref[dr * 3 + dc:dr * 3 + dc + 1, :]

    def shifted(blk, dc):
        n = blk.shape[0]
        if dc == 1:
            return blk
        ridx = _iota((n, 1), 0)
        if dc == 0:
            return jnp.where(ridx == 0, 0.0, pltpu.roll(blk, 1, 0))
        return jnp.where(ridx == n - 1, 0.0, pltpu.roll(blk, n - 1, 0))

    def finish_qk(acc):
        a = _silu(acc)
        lane = _iota((1, w2), 1)
        return jnp.where(lane >= GROUP_W, a * kscale, a)

    def conv_body(i, carry):
        r0 = pl.multiple_of(i * GRID_W, GRID_W)
        r_up = pl.multiple_of(jnp.maximum(i - 1, 0) * GRID_W, GRID_W)
        r_dn = pl.multiple_of(jnp.minimum(i + 1, n_rows_grid - 1) * GRID_W, GRID_W)
        up = qk_ref[0, pl.ds(r_up, GRID_W), :].astype(F32) * jnp.where(i > 0, 1.0, 0.0)
        mid = qk_ref[0, pl.ds(r0, GRID_W), :].astype(F32)
        dn = qk_ref[0, pl.ds(r_dn, GRID_W), :].astype(F32) * jnp.where(i < n_rows_grid - 1, 1.0, 0.0)
        acc = jnp.zeros((GRID_W, w2), F32)
        for dr, blk in enumerate((up, mid, dn)):
            for dc in range(3):
                acc = acc + shifted(blk, dc) * tap(dr, dc)
        qc_sc[pl.ds(r0, GRID_W), :] = finish_qk(acc)
        return carry

    lax.fori_loop(0, n_rows_grid, conv_body, 0)
    xc = qkc_ref[0].astype(F32)
    acc = jnp.zeros((tc, w2), F32)
    for dc in range(3):
        acc = acc + shifted(xc, dc) * tap(1, dc)
    qcc_sc[...] = finish_qk(acc)

    cn_sc[...] = jnp.zeros_like(cn_sc)

    def run(n_rows, qc, v, sm, hf, hb, want_out, m0):
        n = n_rows // ML_BLOCK

        def body(i, m):
            rf = pl.multiple_of(i * ML_BLOCK, ML_BLOCK)
            rb = pl.multiple_of((n - 1 - i) * ML_BLOCK, ML_BLOCK)
            blks = [(qc[pl.ds(r0, ML_BLOCK), :], v[0, pl.ds(r0, ML_BLOCK), :].astype(F32),
                     sm[0, pl.ds(r0, ML_BLOCK), :] + gb_ref[...]) for r0 in (rf, rb)]
            outs, m_new = _mlstm_block(blks[0], blks[1], cn_sc, m, selh_ref, selp_ref, want_out)
            if want_out:
                for p in range(N_PAIRS):
                    hf[pl.ds(rf, ML_BLOCK), p * LANES:(p + 1) * LANES] = outs[(0, p)]
                    hb[pl.ds(rb, ML_BLOCK), p * LANES:(p + 1) * LANES] = outs[(1, p)]
            return m_new

        return lax.fori_loop(0, n, body, m0)

    m1 = run(tc, qcc_sc, vc_ref, smc_ref, hfc_sc, hbc_sc, ctx_out, jnp.zeros((1, LANES), F32))
    run(t, qc_sc, v_ref, sm_ref, hf_sc, hb_sc, True, m1)

    def finish(n_rows, hf, hb, og, dst):
        blk = min(n_rows, 256)

        def body(i, carry):
            r0 = pl.multiple_of(i * blk, blk)
            res = _sigmoid(og[0, pl.ds(r0, blk), :].astype(F32)) * (hf[pl.ds(r0, blk), :] + hb[pl.ds(r0, blk), :])
            dst[0, pl.ds(r0, blk), :] = res.astype(dst.dtype)
            return carry

        lax.fori_loop(0, n_rows // blk, body, 0)

    finish(t, hf_sc, hb_sc, og_ref, o_ref)
    if ctx_out:
        finish(tc, hfc_sc, hbc_sc, ogc_ref, oc_ref)


def _mlstm(p, ps, pc, pcs, conv_w, gate_b, ctx_out):
    b, t, _ = p.shape
    tc = pc.shape[1]
    sel_head, sel_pair = (jnp.asarray(a).astype(BF16) for a in _mlstm_selectors())

    def col(width, idx, rows):
        return pl.BlockSpec((1, rows, width), lambda i: (i, 0, idx))

    full = lambda shape: pl.BlockSpec(shape, lambda i: tuple(0 for _ in shape))
    in_specs = [col(2 * GROUP_W, P_ML_QK, t), col(GROUP_W, P_ML_V, t), col(GROUP_W, P_ML_OG, t),
                col(LANES, 0, t),
                col(2 * GROUP_W, P_ML_QK, tc), col(GROUP_W, P_ML_V, tc), col(GROUP_W, P_ML_OG, tc),
                col(LANES, 0, tc),
                full(conv_w.shape), full(gate_b.shape), full(sel_head.shape), full(sel_pair.shape)]
    out_specs = [pl.BlockSpec((1, t, GROUP_W), lambda i: (i, 0, 0))]
    out_shape = [jax.ShapeDtypeStruct((b, t, GROUP_W), BF16)]
    scratch = [pltpu.VMEM((t, 2 * GROUP_W), F32), pltpu.VMEM((tc, 2 * GROUP_W), F32),
               pltpu.VMEM((t, GROUP_W), F32), pltpu.VMEM((t, GROUP_W), F32)]
    if ctx_out:
        out_specs.append(pl.BlockSpec((1, tc, GROUP_W), lambda i: (i, 0, 0)))
        out_shape.append(jax.ShapeDtypeStruct((b, tc, GROUP_W), BF16))
        scratch += [pltpu.VMEM((tc, GROUP_W), F32), pltpu.VMEM((tc, GROUP_W), F32)]
    scratch += [pltpu.VMEM((2 * N_PAIRS, LANES, 2 * LANES), F32)]
    res = pl.pallas_call(
        functools.partial(_mlstm_kernel, ctx_out=ctx_out),
        grid=(b,),
        in_specs=in_specs, out_specs=out_specs, out_shape=out_shape, scratch_shapes=scratch,
        compiler_params=_cparams(("parallel",)),
        name="mlstm",
    )(p, p, p, ps, pc, pc, pc, pcs, conv_w, gate_b, sel_head, sel_pair)
    return (res[0], res[1]) if ctx_out else (res[0], None)


def _sgu_kernel(u_ref, v_ref, f_ref, ws_ref, bias_ref, dft_ref, o_ref, z_ref):
    t = u_ref.shape[1]
    lane_g = _iota((1, GROUP_W), 1) // HEAD_DIM

    def body(i, carry):
        r0 = pl.multiple_of(i * SGU_CHUNK, SGU_CHUNK)
        v = v_ref[0, pl.ds(r0, SGU_CHUNK), :].astype(F32)
        mu = jnp.mean(v, axis=-1, keepdims=True)
        vc = v - mu
        vn = vc * lax.rsqrt(jnp.mean(vc * vc, axis=-1, keepdims=True) + EPS)
        mixed = bias_ref[...]
        for g in range(N_HEADS):
            mixed = mixed + _bdot(ws_ref[g], jnp.where(lane_g == g, vn, 0.0))
        o_ref[0, pl.ds(r0, SGU_CHUNK), :] = (u_ref[0, pl.ds(r0, SGU_CHUNK), :].astype(F32) * mixed).astype(o_ref.dtype)
        z = _bdot(f_ref[0, pl.ds(r0, SGU_CHUNK), :], dft_ref[...])
        z_ref[0, 0, pl.ds(r0, SGU_CHUNK), :] = z[:, :GROUP_W].astype(z_ref.dtype)
        z_ref[1, 0, pl.ds(r0, SGU_CHUNK), :] = z[:, GROUP_W:].astype(z_ref.dtype)
        return carry

    lax.fori_loop(0, t // SGU_CHUNK, body, 0)


def _sgu_fourier1(p, ws, bias, dft_c):
    b, t, _ = p.shape

    def col(idx):
        return pl.BlockSpec((1, t, GROUP_W), lambda i: (i, 0, idx))

    full = lambda shape: pl.BlockSpec(shape, lambda i: tuple(0 for _ in shape))
    return pl.pallas_call(
        _sgu_kernel,
        grid=(b,),
        in_specs=[col(P_SGU_U), col(P_SGU_V), col(P_FOUR), full(ws.shape), full(bias.shape), full(dft_c.shape)],
        out_specs=[pl.BlockSpec((1, t, GROUP_W), lambda i: (i, 0, 0)),
                   pl.BlockSpec((2, 1, t, GROUP_W), lambda i: (0, i, 0, 0))],
        out_shape=[jax.ShapeDtypeStruct((b, t, GROUP_W), BF16), jax.ShapeDtypeStruct((2, b, t, GROUP_W), BF16)],
        compiler_params=_cparams(("parallel",)),
        name="sgu_fourier1",
    )(p, p, p, ws, bias, dft_c)


def _fourier2_kernel(ct_ref, st_ref, z_ref, o_ref, *, scale):
    for i in range(z_ref.shape[1]):
        acc = jnp.dot(ct_ref[...], z_ref[0, i], preferred_element_type=F32)
        acc -= jnp.dot(st_ref[...], z_ref[1, i], preferred_element_type=F32)
        o_ref[i] = (acc * scale).astype(o_ref.dtype)


def _fourier2(z, cos_t, sin_t):
    _, b, t, w = z.shape
    tm = min(t, 256)
    return pl.pallas_call(
        functools.partial(_fourier2_kernel, scale=float((t * HEAD_DIM) ** -0.5)),
        grid=(t // tm,),
        in_specs=[pl.BlockSpec((tm, t), lambda i: (i, 0)),
                  pl.BlockSpec((tm, t), lambda i: (i, 0)),
                  pl.BlockSpec((2, b, t, w), lambda i: (0, 0, 0, 0), pipeline_mode=pl.Buffered(1))],
        out_specs=pl.BlockSpec((b, tm, w), lambda i: (0, i, 0)),
        out_shape=jax.ShapeDtypeStruct((b, t, w), BF16),
        compiler_params=_cparams(("parallel",)),
        name="fourier2",
    )(cos_t, sin_t, z)


def _dft_tables(n):
    k = np.arange(n, dtype=np.int64)
    ang = 2.0 * np.pi * ((k[:, None] * k[None, :]) % n).astype(np.float64) / n
    return np.cos(ang).astype(np.float32), np.sin(ang).astype(np.float32)


def _channel_dft():
    c, s = _dft_tables(HEAD_DIM)
    eye = np.eye(N_HEADS, dtype=np.float32)
    return np.concatenate([np.kron(eye, c), np.kron(eye, s)], axis=1)


def _route_kernel(x_ref, mod_ref, g_ref, wr_ref, h_ref, mi_ref, mf_ref, cnt_ref, cnt_sc):
    first = (pl.program_id(0) == 0) & (pl.program_id(1) == 0)

    @pl.when(first)
    def _():
        cnt_sc[...] = jnp.zeros_like(cnt_sc)

    tm = x_ref.shape[1]
    h = _norm_mod(x_ref[0], g_ref[...], mod_ref[0, 3:4, :], mod_ref[0, 4:5, :])
    for j in range(D_MODEL // LANES):
        h_ref[pl.ds(j, tm, stride=SUBLANES), :] = h[:, j * LANES:(j + 1) * LANES]
    logits = _fdot(h, wr_ref[...])
    lane = _iota((tm, LANES), 1).astype(F32)
    lg = jnp.where(lane < N_EXPERTS, logits, -jnp.inf)
    m1 = jnp.max(lg, axis=1, keepdims=True)
    i1 = jnp.min(jnp.where(lg == m1, lane, float(LANES)), axis=1, keepdims=True)
    lg2 = jnp.where(lane == i1, -jnp.inf, lg)
    m2 = jnp.max(lg2, axis=1, keepdims=True)
    i2 = jnp.min(jnp.where(lg2 == m2, lane, float(LANES)), axis=1, keepdims=True)
    e = jnp.exp(m2 - m1)
    g0 = 1.0 / (1.0 + e)
    g1 = e / (1.0 + e)
    onehot = jnp.where((lane == i1) | (lane == i2), 1.0, 0.0)
    strict = jnp.where(_iota((tm, tm), 1) < _iota((tm, tm), 0), 1.0, 0.0)
    before = _bdot(strict, onehot) + cnt_sc[...]
    r0 = jnp.sum(jnp.where(lane == i1, before, 0.0), axis=1, keepdims=True)
    r1 = jnp.sum(jnp.where(lane == i2, before, 0.0), axis=1, keepdims=True)
    cnt_sc[...] += jnp.sum(onehot, axis=0, keepdims=True)
    cnt_ref[...] = cnt_sc[...]
    l8 = _iota((tm, SUBLANES), 1)
    mi = jnp.where(l8 == 0, i1, jnp.where(l8 == 1, i2, jnp.where(l8 == 2, r0, r1)))
    mi_ref[...] = mi.astype(jnp.int32)
    mf_ref[...] = jnp.where(l8 == 0, g0, g1)


def _route(x, mod, g, w_router):
    b, t, d = x.shape
    m = b * t
    tm = 512
    nt = t // tm
    wr = jnp.zeros((d, LANES), F32).at[:, :N_EXPERTS].set(w_router)
    return pl.pallas_call(
        _route_kernel,
        grid=(b, nt),
        in_specs=[pl.BlockSpec((1, tm, d), lambda i, j: (i, j, 0)),
                  pl.BlockSpec((1, 6, d), lambda i, j: (i, 0, 0)),
                  pl.BlockSpec((1, d), lambda i, j: (0, 0)),
                  pl.BlockSpec((d, LANES), lambda i, j: (0, 0))],
        out_specs=[pl.BlockSpec((tm * SUBLANES, LANES), lambda i, j: (i * nt + j, 0)),
                   pl.BlockSpec((tm, SUBLANES), lambda i, j: (i * nt + j, 0)),
                   pl.BlockSpec((tm, SUBLANES), lambda i, j: (i * nt + j, 0)),
                   pl.BlockSpec((1, LANES), lambda i, j: (0, 0))],
        out_shape=[jax.ShapeDtypeStruct((m * SUBLANES, LANES), F32),
                   jax.ShapeDtypeStruct((m, SUBLANES), jnp.int32),
                   jax.ShapeDtypeStruct((m, SUBLANES), F32),
                   jax.ShapeDtypeStruct((1, LANES), F32)],
        scratch_shapes=[pltpu.VMEM((1, LANES), F32)],
        compiler_params=_cparams(("arbitrary", "arbitrary")),
        name="route",
    )(x, mod, g.reshape(1, d), wr)


def _moe_kernel(texp_ref, trows_ref, pos0_ref, pos1_ref, h_hbm, wg_ref, wu_ref, wd_ref, y_hbm,
                order_sm, xbuf, x_sc, acc_sc, ybuf, gsem, ssem):
    i = pl.program_id(0)
    j = pl.program_id(1)
    nf = pl.num_programs(1)
    tm = MOE_TM
    n_tok = pos0_ref.shape[0]
    rows = trows_ref[i]

    nt = pl.num_programs(0)
    buf_rows = tm * SUBLANES
    dump = TOP_K * n_tok
    slot = lax.rem(i, 2)
    prev_rows = trows_ref[jnp.maximum(i - 1, 0)]
    next_rows = trows_ref[jnp.minimum(i + 1, nt - 1)]
    n_full, n_rest = divmod(tm, MOE_COPY_ROWS)

    def gather_copy(tile, n_real, r, dst_slot):
        idx = jnp.where(n_real > 0, tile * tm + jnp.minimum(r, n_real - 1), 0)
        tok = jnp.right_shift(order_sm[idx], 1)
        off = pl.multiple_of(dst_slot * buf_rows + r * SUBLANES, SUBLANES)
        return pltpu.make_async_copy(h_hbm.at[tok], xbuf.at[pl.ds(off, SUBLANES), :], gsem)

    def scatter_copy(tile, n_real, r, src_slot):
        idx = jnp.where(n_real > 0, tile * tm + jnp.minimum(r, n_real - 1), 0)
        dst = jnp.where(r < n_real, order_sm[idx], dump + r)
        off = pl.multiple_of(src_slot * buf_rows + r * SUBLANES, SUBLANES)
        return pltpu.make_async_copy(ybuf.at[pl.ds(off, SUBLANES), :], y_hbm.at[dst], ssem)

    def wait_all(buf, sem):
        pltpu.make_async_copy(buf.at[pl.ds(0, buf_rows), :], buf.at[pl.ds(buf_rows, buf_rows), :], sem).wait()

    @pl.when((i == 0) & (j == 0))
    def _():
        def inv_body(t, carry):
            order_sm[pos0_ref[t]] = 2 * t
            order_sm[pos1_ref[t]] = 2 * t + 1
            return carry

        lax.fori_loop(0, n_tok, inv_body, 0, unroll=8)
        ybuf[...] = jnp.zeros_like(ybuf)

        def start(r, carry):
            gather_copy(0, rows, r, 0).start()
            return carry

        lax.fori_loop(0, tm, start, 0)

    @pl.when((j == 0) & (rows > 0))
    def _():
        wait_all(xbuf, gsem)
        base = pl.multiple_of(slot * buf_rows, SUBLANES)
        for c in range(D_MODEL // LANES):
            x_sc[:, c * LANES:(c + 1) * LANES] = xbuf[pl.ds(base + c, tm, stride=SUBLANES), :].astype(BF16)
        acc_sc[...] = jnp.zeros_like(acc_sc)

    def step(n_copies):
        r0 = j * MOE_COPY_ROWS
        for q in range(n_copies):
            gather_copy(i + 1, next_rows, r0 + q, 1 - slot).start()
            scatter_copy(i - 1, jnp.where(i > 0, prev_rows, 0), r0 + q, 1 - slot).start()
        x = x_sc[...]
        gate = jnp.dot(x, wg_ref[0].astype(BF16), preferred_element_type=F32)
        up = jnp.dot(x, wu_ref[0].astype(BF16), preferred_element_type=F32)
        acc_sc[...] += jnp.dot((_silu(gate) * up).astype(BF16), wd_ref[0].astype(BF16),
                               preferred_element_type=F32)

    pl.when((rows > 0) & (j < n_full))(functools.partial(step, MOE_COPY_ROWS))
    pl.when((rows > 0) & (j == n_full))(functools.partial(step, n_rest))
    pl.when((rows > 0) & (j > n_full))(functools.partial(step, 0))

    @pl.when((j == nf - 1) & (rows > 0))
    def _():
        wait_all(ybuf, ssem)
        base = pl.multiple_of(slot * buf_rows, SUBLANES)
        for c in range(D_MODEL // LANES):
            ybuf[pl.ds(base + c, tm, stride=SUBLANES), :] = acc_sc[:, c * LANES:(c + 1) * LANES]

    @pl.when((j == 0) & (rows == 0) & (i > 0) & (prev_rows > 0))
    def _():
        wait_all(xbuf, gsem)

        def start(r, carry):
            scatter_copy(i - 1, prev_rows, r, 1 - slot).start()
            return carry

        lax.fori_loop(0, tm, start, 0)
        wait_all(ybuf, ssem)


def _moe(h_tiles, pos0, pos1, tile_expert, tile_rows, wg, wu, wd):
    m = h_tiles.shape[0]
    n_exp, d, f = wg.shape
    nt = tile_expert.shape[0]
    tm, tf = MOE_TM, MOE_TF
    nf = f // tf
    assert nf * MOE_COPY_ROWS >= tm, "each tile needs enough column steps to issue its neighbours' row copies"
    assert nt * tm > TOP_K * m + n_exp * (tm - 1), "the last tile must be unused: it drains the copy pipeline"

    def col_tile(i, j, tr):
        return jnp.where(tr[i] > 0, j, nf - 1)

    grid_spec = pltpu.PrefetchScalarGridSpec(
        num_scalar_prefetch=4,
        grid=(nt, nf),
        in_specs=[pl.BlockSpec(memory_space=pl.ANY),
                  pl.BlockSpec((1, d, tf), lambda i, j, te, tr, p0, p1: (te[i], 0, col_tile(i, j, tr))),
                  pl.BlockSpec((1, d, tf), lambda i, j, te, tr, p0, p1: (te[i], 0, col_tile(i, j, tr))),
                  pl.BlockSpec((1, tf, d), lambda i, j, te, tr, p0, p1: (te[i], col_tile(i, j, tr), 0))],
        out_specs=pl.BlockSpec(memory_space=pl.ANY),
        scratch_shapes=[pltpu.SMEM((nt * tm,), jnp.int32),
                        pltpu.VMEM((2 * tm * SUBLANES, LANES), F32),
                        pltpu.VMEM((tm, d), BF16),
                        pltpu.VMEM((tm, d), F32),
                        pltpu.VMEM((2 * tm * SUBLANES, LANES), F32),
                        pltpu.SemaphoreType.DMA(()),
                        pltpu.SemaphoreType.DMA(())],
    )
    return pl.pallas_call(
        _moe_kernel,
        grid_spec=grid_spec,
        out_shape=jax.ShapeDtypeStruct((TOP_K * m + tm, SUBLANES, LANES), F32),
        compiler_params=_cparams(("arbitrary", "arbitrary")),
        name="moe",
    )(tile_expert, tile_rows, pos0, pos1, h_tiles, wg, wu, wd)


def _combine_kernel(x_ref, mod_ref, y_ref, mf_ref, fg_ref, o_ref):
    tm = x_ref.shape[1]
    stride = TOP_K * SUBLANES
    g0 = mf_ref[:, 0:1]
    g1 = mf_ref[:, 1:2]
    parts = []
    for c in range(D_MODEL // LANES):
        y0 = y_ref[pl.ds(c, tm, stride=stride), :]
        y1 = y_ref[pl.ds(SUBLANES + c, tm, stride=stride), :]
        parts.append(g0 * y0 + g1 * y1)
    moe = jnp.concatenate(parts, axis=1)
    x = x_ref[0] + mod_ref[0, 5:6, :] * moe
    ms = jnp.mean(x * x, axis=-1, keepdims=True)
    o_ref[0] = x * lax.rsqrt(ms + EPS) * fg_ref[...]


def _combine(x, mod, y, mf, final_g):
    b, t, d = x.shape
    tm = 256
    nt = t // tm
    rows = tm * TOP_K * SUBLANES
    return pl.pallas_call(
        _combine_kernel,
        grid=(b, nt),
        in_specs=[pl.BlockSpec((1, tm, d), lambda i, j: (i, j, 0)),
                  pl.BlockSpec((1, 6, d), lambda i, j: (i, 0, 0)),
                  pl.BlockSpec((rows, LANES), lambda i, j: (i * nt + j, 0)),
                  pl.BlockSpec((tm, SUBLANES), lambda i, j: (i * nt + j, 0)),
                  pl.BlockSpec((1, d), lambda i, j: (0, 0))],
        out_specs=pl.BlockSpec((1, tm, d), lambda i, j: (i, j, 0)),
        out_shape=jax.ShapeDtypeStruct((b, t, d), F32),
        compiler_params=_cparams(("parallel", "parallel")),
        name="combine",
    )(x, mod, y, mf, final_g.reshape(1, d))


def _moe_layer(x, mod, norm_g, w_router, wg, wu, wd, final_g):
    b, t, d = x.shape
    m = b * t
    h_tiles, mi, mf, cnt = _route(x, mod, norm_g, w_router)
    counts = cnt[0, :N_EXPERTS].astype(jnp.int32)
    tiles_per = (counts + MOE_TM - 1) // MOE_TM
    tile_end = jnp.cumsum(tiles_per)
    tile_start = tile_end - tiles_per
    seg_start = tile_start * MOE_TM
    nt = (TOP_K * m) // MOE_TM + N_EXPERTS
    tile_id = jnp.arange(nt, dtype=jnp.int32)
    used_id = jnp.minimum(tile_id, tile_end[-1] - 1)
    tile_expert = jnp.sum(used_id[:, None] >= tile_end[None, :], axis=1).astype(jnp.int32)
    tile_rows = jnp.clip(counts[tile_expert] - (tile_id - tile_start[tile_expert]) * MOE_TM, 0, MOE_TM)
    tile_rows = jnp.where(tile_id < tile_end[-1], tile_rows, 0).astype(jnp.int32)
    pos0 = seg_start[mi[:, 0]] + mi[:, 2]
    pos1 = seg_start[mi[:, 1]] + mi[:, 3]
    y = _moe(h_tiles.reshape(m, SUBLANES, LANES), pos0, pos1, tile_expert, tile_rows, wg, wu, wd)
    return _combine(x, mod, y.reshape(-1, LANES), mf, final_g)


def _permute_w_in(w):
    d = w.shape[0]
    return jnp.concatenate([w[:, 0:768], w[:, 1312:1568], w[:, 800:1312], w[:, 1568:1824], w[:, 1840:2608],
                            w[:, 768:800], w[:, 1824:1840], jnp.zeros((d, P_WIDTH - 2608), w.dtype)],
                           axis=1).astype(BF16)


def _mixers(proj, proj_c, gla_w_lr, gla_b_lr, gla_norm_g, mlstm_conv, mlstm_gate_b, sgu_w, sgu_b, ctx_out, tables):
    (p, ps), (pc, pcs) = proj, proj_c
    wlr = jnp.zeros((LANES, 2 * GLA_QK_W), F32)
    wlr = wlr.at[0:GLA_RANK, 0:GLA_QK_W].set(gla_w_lr[0]).at[GLA_RANK:2 * GLA_RANK, GLA_QK_W:].set(gla_w_lr[1])
    blr = gla_b_lr.reshape(1, 2 * GLA_QK_W)
    gla, gla_c = _gla(p, ps, pc, pcs, wlr, blr, gla_norm_g.reshape(1, GROUP_W), ctx_out)
    conv_w = mlstm_conv.reshape(9, 2 * GROUP_W)
    gate_b = jnp.zeros((1, LANES), F32).at[0, SMALL_GATE_LANE:SMALL_GATE_LANE + 4 * N_HEADS].set(
        mlstm_gate_b.reshape(-1))
    ml, ml_c = _mlstm(p, ps, pc, pcs, conv_w, gate_b, ctx_out)
    bias = jnp.repeat(sgu_b.T, HEAD_DIM, axis=1)
    dft_c, tabs = tables
    sg, z = _sgu_fourier1(p, sgu_w, bias, dft_c)
    fo = _fourier2(z, *tabs[p.shape[1]])
    mix = (gla, ml, sg, fo)
    if not ctx_out:
        return mix, None
    sg_c, z_c = _sgu_fourier1(pc, sgu_w, bias, dft_c)
    fo_c = _fourier2(z_c, *tabs[pc.shape[1]])
    return mix, (gla_c, ml_c, sg_c, fo_c)


def kernel(x, c, ctx, c_ctx, w_ada, b_ada, norm_mix_g, norm_ffn_g, w_in, w_out, gla_w_lr, gla_b_lr, gla_norm_g,
           mlstm_conv, mlstm_gate_b, sgu_w, sgu_b, ffn_w_gate, ffn_w_up, ffn_w_down, moe_router, moe_w_gate,
           moe_w_up, moe_w_down, final_norm_g):
    depth = w_ada.shape[0]
    assert depth == 2, "layer 0 is the dense layer with context output, layer 1 the MoE layer"
    b, t, d = x.shape
    tc = ctx.shape[1]
    cvec = jnp.zeros((16, d), F32).at[:b].set(c).at[b].set(c_ctx)
    mods = _ada(cvec, w_ada, b_ada).reshape(depth, 16, 6, d)
    tables = (jnp.asarray(_channel_dft()).astype(BF16),
              {n: tuple(jnp.asarray(a).astype(BF16) for a in _dft_tables(n)) for n in {t, tc}})
    xc = ctx
    for l in range(depth):
        ctx_out = l < depth - 1
        mod, mod_c = mods[l, :b], mods[l, b:b + 1]
        w_in_l = _permute_w_in(w_in[l])
        w_out_l = w_out[l].astype(BF16)
        p = _in_proj(x, mod, norm_mix_g[l], w_in_l)
        pc = _in_proj(xc, mod_c, norm_mix_g[l], w_in_l)
        mix, mix_c = _mixers(p, pc, gla_w_lr[l], gla_b_lr[l], gla_norm_g[l], mlstm_conv[l], mlstm_gate_b[l],
                             sgu_w[l], sgu_b[l], ctx_out, tables)
        x = _out_proj(x, mod, mix, w_out_l)
        if ctx_out:
            xc = _out_proj(xc, mod_c, mix_c, w_out_l)
        i = l // 2
        if l % 2 == 0:
            wg, wu, wd = ffn_w_gate[i].astype(BF16), ffn_w_up[i].astype(BF16), ffn_w_down[i].astype(BF16)
            x = _ffn(x, mod, norm_ffn_g[l], wg, wu, wd)
            if ctx_out:
                xc = _ffn(xc, mod_c, norm_ffn_g[l], wg, wu, wd)
        else:
            x = _moe_layer(x, mod, norm_ffn_g[l], moe_router[i], moe_w_gate[i], moe_w_up[i], moe_w_down[i],
                           final_norm_g)
    return x
```

```python
import functools

import numpy as np
import jax
import jax.numpy as jnp
from jax import lax
from jax.experimental import pallas as pl
from jax.experimental.pallas import tpu as pltpu

F32 = jnp.float32
BF16 = jnp.bfloat16
HIGHEST = lax.Precision.HIGHEST

D_MODEL = 1024
GROUP_W = 256
N_HEADS = 4
HEAD_DIM = 64
GLA_DK = 32
GLA_QK_W = N_HEADS * GLA_DK
GLA_RANK = 16
GLA_NORMALIZER = 16.0
GLA_BLOCK = 128
GRID_W = 64
SGU_CHUNK = 128
N_EXPERTS = 8
TOP_K = 2
EPS = 1e-6

LANES = 128
SUBLANES = 8
VMEM_LIMIT = 52 * 1024 * 1024

P_GLA_QK, P_GLA_V, P_GLA_G, P_ML_V = 0, 1, 2, 3
P_ML_QK = 2
P_ML_OG, P_SGU_U, P_SGU_V, P_FOUR = 6, 7, 8, 9
P_MAIN_W = 2560
P_WIDTH = 2688
SMALL_GATE_LANE = 32

MOE_TM = 1024
MOE_TF = 512
MOE_COPY_ROWS = 171


def _cparams(sem, **kw):
    return pltpu.CompilerParams(dimension_semantics=sem, vmem_limit_bytes=VMEM_LIMIT, **kw)


def _sigmoid(x):
    return 1.0 / (1.0 + jnp.exp(-x))


def _silu(x):
    return x * _sigmoid(x)


def _log_sigmoid(x):
    return jnp.minimum(x, 0.0) - jnp.log(1.0 + jnp.exp(-jnp.abs(x)))


def _norm_mod(x, g, shift, scale):
    ms = jnp.mean(x * x, axis=-1, keepdims=True)
    return (x * lax.rsqrt(ms + EPS) * g) * (1.0 + scale) + shift


def _bdot(a, b):
    return jnp.dot(a.astype(BF16), b.astype(BF16), preferred_element_type=F32)


def _bdot_nt(a, b):
    return lax.dot_general(a.astype(BF16), b.astype(BF16), (((1,), (1,)), ((), ())), preferred_element_type=F32)


def _fdot(a, b):
    return jnp.dot(a, b, precision=HIGHEST, preferred_element_type=F32)


def _split3(x):
    hi = x.astype(BF16)
    r = x - hi.astype(F32)
    mid = r.astype(BF16)
    lo = (r - mid.astype(F32)).astype(BF16)
    return hi, mid, lo


def _dot3(a, b):
    a_hi, a_lo, _ = _split3(a)
    b_hi, b_lo, _ = _split3(b)
    dot = functools.partial(jnp.dot, preferred_element_type=F32)
    return dot(a_hi, b_hi) + dot(a_lo, b_hi) + dot(a_hi, b_lo)


def _tri_dot(tri, x):
    n = x.shape[1]
    y = jnp.dot(tri.astype(BF16), jnp.concatenate(_split3(x), axis=1), preferred_element_type=F32)
    return y[:, :n] + y[:, n:2 * n] + y[:, 2 * n:]


def _ada_kernel(c_ref, w_ref, b_ref, o_ref):
    o_ref[0] = _fdot(_silu(c_ref[...]), w_ref[0]) + b_ref[0]


def _ada(cvec, w_ada, b_ada):
    depth, d, n = w_ada.shape
    rows = cvec.shape[0]
    tn = 1536
    return pl.pallas_call(
        _ada_kernel,
        grid=(depth, n // tn),
        in_specs=[pl.BlockSpec((rows, d), lambda l, j: (0, 0)),
                  pl.BlockSpec((1, d, tn), lambda l, j: (l, 0, j)),
                  pl.BlockSpec((1, 1, tn), lambda l, j: (l, 0, j))],
        out_specs=pl.BlockSpec((1, rows, tn), lambda l, j: (l, 0, j)),
        out_shape=jax.ShapeDtypeStruct((depth, rows, n), F32),
        compiler_params=_cparams(("parallel", "parallel")),
        name="ada",
    )(cvec, w_ada, b_ada.reshape(depth, 1, n))


def _in_kernel(x_ref, mod_ref, g_ref, w_ref, o_ref, os_ref):
    h = _norm_mod(x_ref[0], g_ref[...], mod_ref[0, 0:1, :], mod_ref[0, 1:2, :])
    acc = jnp.dot(h.astype(BF16), w_ref[...], preferred_element_type=F32)
    o_ref[0] = acc[:, :P_MAIN_W].astype(o_ref.dtype)
    os_ref[0] = acc[:, P_MAIN_W:]


def _in_proj(x, mod, g, w):
    b, t, d = x.shape
    n = w.shape[1]
    tm = min(t, 512)
    mod_map = (lambda i, j: (i, 0, 0)) if mod.shape[0] == b else (lambda i, j: (0, 0, 0))
    return pl.pallas_call(
        _in_kernel,
        grid=(b, t // tm),
        in_specs=[pl.BlockSpec((1, tm, d), lambda i, j: (i, j, 0)),
                  pl.BlockSpec((1, 6, d), mod_map),
                  pl.BlockSpec((1, d), lambda i, j: (0, 0)),
                  pl.BlockSpec((d, n), lambda i, j: (0, 0))],
        out_specs=[pl.BlockSpec((1, tm, P_MAIN_W), lambda i, j: (i, j, 0)),
                   pl.BlockSpec((1, tm, n - P_MAIN_W), lambda i, j: (i, j, 0))],
        out_shape=[jax.ShapeDtypeStruct((b, t, P_MAIN_W), BF16), jax.ShapeDtypeStruct((b, t, n - P_MAIN_W), F32)],
        compiler_params=_cparams(("parallel", "parallel")),
        name="in_proj",
    )(x, mod, g.reshape(1, d), w)


def _out_kernel(x_ref, mod_ref, a_ref, b_ref, c_ref, d_ref, w_ref, o_ref):
    acc = jnp.dot(a_ref[0], w_ref[0:GROUP_W, :], preferred_element_type=F32)
    acc += jnp.dot(b_ref[0], w_ref[GROUP_W:2 * GROUP_W, :], preferred_element_type=F32)
    acc += jnp.dot(c_ref[0], w_ref[2 * GROUP_W:3 * GROUP_W, :], preferred_element_type=F32)
    acc += jnp.dot(d_ref[0], w_ref[3 * GROUP_W:4 * GROUP_W, :], preferred_element_type=F32)
    o_ref[0] = x_ref[0] + mod_ref[0, 2:3, :] * acc


def _out_proj(x, mod, mixes, w):
    b, t, d = x.shape
    tm = min(t, 512)
    mod_map = (lambda i, j: (i, 0, 0)) if mod.shape[0] == b else (lambda i, j: (0, 0, 0))
    mix_spec = pl.BlockSpec((1, tm, GROUP_W), lambda i, j: (i, j, 0))
    return pl.pallas_call(
        _out_kernel,
        grid=(b, t // tm),
        in_specs=[pl.BlockSpec((1, tm, d), lambda i, j: (i, j, 0)),
                  pl.BlockSpec((1, 6, d), mod_map),
                  mix_spec, mix_spec, mix_spec, mix_spec,
                  pl.BlockSpec((d, d), lambda i, j: (0, 0))],
        out_specs=pl.BlockSpec((1, tm, d), lambda i, j: (i, j, 0)),
        out_shape=jax.ShapeDtypeStruct((b, t, d), F32),
        compiler_params=_cparams(("parallel", "parallel")),
        name="out_proj",
    )(x, mod, *mixes, w)


def _ffn_kernel(x_ref, mod_ref, g_ref, wg_ref, wu_ref, wd_ref, o_ref, h_sc, acc_sc):
    j = pl.program_id(2)

    @pl.when(j == 0)
    def _():
        h = _norm_mod(x_ref[0], g_ref[...], mod_ref[0, 3:4, :], mod_ref[0, 4:5, :])
        h_sc[...] = h.astype(BF16)
        acc_sc[...] = jnp.zeros_like(acc_sc)

    h = h_sc[...]
    gate = jnp.dot(h, wg_ref[...], preferred_element_type=F32)
    up = jnp.dot(h, wu_ref[...], preferred_element_type=F32)
    acc_sc[...] += jnp.dot((_silu(gate) * up).astype(BF16), wd_ref[...], preferred_element_type=F32)

    @pl.when(j == pl.num_programs(2) - 1)
    def _():
        o_ref[0] = x_ref[0] + mod_ref[0, 5:6, :] * acc_sc[...]


def _ffn(x, mod, g, wg, wu, wd):
    b, t, d = x.shape
    f = wg.shape[1]
    tm = min(t, 512)
    tf = f // 2
    mod_map = (lambda i, j, k: (i, 0, 0)) if mod.shape[0] == b else (lambda i, j, k: (0, 0, 0))
    return pl.pallas_call(
        _ffn_kernel,
        grid=(b, t // tm, f // tf),
        in_specs=[pl.BlockSpec((1, tm, d), lambda i, j, k: (i, j, 0)),
                  pl.BlockSpec((1, 6, d), mod_map),
                  pl.BlockSpec((1, d), lambda i, j, k: (0, 0)),
                  pl.BlockSpec((d, tf), lambda i, j, k: (0, k)),
                  pl.BlockSpec((d, tf), lambda i, j, k: (0, k)),
                  pl.BlockSpec((tf, d), lambda i, j, k: (k, 0))],
        out_specs=pl.BlockSpec((1, tm, d), lambda i, j, k: (i, j, 0)),
        out_shape=jax.ShapeDtypeStruct((b, t, d), F32),
        scratch_shapes=[pltpu.VMEM((tm, d), BF16), pltpu.VMEM((tm, d), F32)],
        compiler_params=_cparams(("parallel", "parallel", "arbitrary")),
        name="ffn",
    )(x, mod, g.reshape(1, d), wg, wu, wd)


def _iota(shape, dim):
    return lax.broadcasted_iota(jnp.int32, shape, dim)


def _gla_chunk(q, k, v, la, s_ref, lower, want_out):
    L = q.shape[0]
    r, c = _iota((L, L), 0), _iota((L, L), 1)
    keep = (c <= r) if lower else (c >= r)
    cum = _tri_dot(jnp.where(keep, 1.0, 0.0), la)
    tot_row = jnp.sum(la, axis=0, keepdims=True)
    s_prev = s_ref[...]
    out = None
    if want_out:
        mid = L // 2 - 1 if lower else L // 2
        ref = cum[mid:mid + 1, :]
        qr = q * jnp.exp(cum - ref)
        kc = (k * jnp.exp(ref - cum)).astype(BF16)
        lane_q = _iota((1, GLA_QK_W), 1) // GLA_DK
        lane_v = _iota((1, GROUP_W), 1) // HEAD_DIM
        out = _bdot_nt(q * jnp.exp(cum), s_prev)
        for h in range(N_HEADS):
            sc = jnp.where(keep, _bdot_nt(jnp.where(lane_q == h, qr, 0.0), kc), 0.0)
            out = out + _bdot(sc, jnp.where(lane_v == h, v, 0.0))
    ke = k * jnp.exp(tot_row - cum)
    upd = jnp.dot(v.T.astype(BF16), ke.astype(BF16), preferred_element_type=F32)
    bd = (_iota((GROUP_W, GLA_QK_W), 0) // HEAD_DIM) == (_iota((GROUP_W, GLA_QK_W), 1) // GLA_DK)
    s_ref[...] = jnp.exp(tot_row) * s_prev + jnp.where(bd, upd, 0.0)
    return out


def _gla_kernel(qk_ref, v_ref, g_ref, sm_ref, qkc_ref, vc_ref, gc_ref, smc_ref, wlr_ref, blr_ref, ng_ref,
                *rest, ctx_out):
    if ctx_out:
        o_ref, oc_ref, la_sc, lac_sc, of_sc, ob_sc, ofc_sc, obc_sc, sf_sc, sb_sc = rest
    else:
        o_ref, la_sc, lac_sc, of_sc, ob_sc, sf_sc, sb_sc = rest
        oc_ref = ofc_sc = obc_sc = None
    t = qk_ref.shape[1]
    tc = qkc_ref.shape[1]
    qscale = GLA_DK ** -0.5

    def log_decay(sm):
        z = _dot3(sm, wlr_ref[...]) + blr_ref[...]
        return _log_sigmoid(z) * (1.0 / GLA_NORMALIZER)

    lac_sc[...] = log_decay(smc_ref[0])
    rows_blk = 256

    def la_body(i, carry):
        r0 = pl.multiple_of(i * rows_blk, rows_blk)
        la_sc[pl.ds(r0, rows_blk), :] = log_decay(sm_ref[0, pl.ds(r0, rows_blk), :])
        return carry

    lax.fori_loop(0, t // rows_blk, la_body, 0)
    sf_sc[...] = jnp.zeros_like(sf_sc)
    sb_sc[...] = jnp.zeros_like(sb_sc)

    def run(n_rows, qk, v, la, of, ob, want_out):
        n = n_rows // GLA_BLOCK

        def body(i, carry):
            rf = pl.multiple_of(i * GLA_BLOCK, GLA_BLOCK)
            rb = pl.multiple_of((n - 1 - i) * GLA_BLOCK, GLA_BLOCK)
            for r0, s_ref, lower, dst, lane0 in ((rf, sf_sc, True, of, 0), (rb, sb_sc, False, ob, GLA_QK_W)):
                qkb = qk[0, pl.ds(r0, GLA_BLOCK), :].astype(F32)
                out = _gla_chunk(qkb[:, :GLA_QK_W] * qscale, qkb[:, GLA_QK_W:], v[0, pl.ds(r0, GLA_BLOCK), :].astype(F32),
                                 la[pl.ds(r0, GLA_BLOCK), lane0:lane0 + GLA_QK_W], s_ref, lower, want_out)
                if want_out:
                    dst[pl.ds(r0, GLA_BLOCK), :] = out
            return carry

        lax.fori_loop(0, n, body, 0)

    run(tc, qkc_ref, vc_ref, lac_sc, ofc_sc, obc_sc, ctx_out)
    run(t, qk_ref, v_ref, la_sc, of_sc, ob_sc, True)

    grp = jnp.where((_iota((GROUP_W, GROUP_W), 0) // HEAD_DIM) == (_iota((GROUP_W, GROUP_W), 1) // HEAD_DIM),
                    1.0 / HEAD_DIM, 0.0).astype(BF16)

    def finish(n_rows, of, ob, g, dst):
        blk = min(n_rows, 256)

        def body(i, carry):
            r0 = pl.multiple_of(i * blk, blk)
            o = of[pl.ds(r0, blk), :] + ob[pl.ds(r0, blk), :]
            sq_hi, sq_lo, _ = _split3(o * o)
            ms = (jnp.dot(sq_hi, grp, preferred_element_type=F32) + jnp.dot(sq_lo, grp, preferred_element_type=F32))
            res = o * lax.rsqrt(ms + EPS) * ng_ref[...] * _silu(g[0, pl.ds(r0, blk), :].astype(F32))
            dst[0, pl.ds(r0, blk), :] = res.astype(dst.dtype)
            return carry

        lax.fori_loop(0, n_rows // blk, body, 0)

    finish(t, of_sc, ob_sc, g_ref, o_ref)
    if ctx_out:
        finish(tc, ofc_sc, obc_sc, gc_ref, oc_ref)


def _gla(p, ps, pc, pcs, wlr, blr, ng, ctx_out):
    b, t, _ = p.shape
    tc = pc.shape[1]

    def col(width, idx, rows):
        return pl.BlockSpec((1, rows, width), lambda i: (i, 0, idx))

    full = lambda shape: pl.BlockSpec(shape, lambda i: tuple(0 for _ in shape))
    in_specs = [col(GROUP_W, P_GLA_QK, t), col(GROUP_W, P_GLA_V, t), col(GROUP_W, P_GLA_G, t), col(LANES, 0, t),
                col(GROUP_W, P_GLA_QK, tc), col(GROUP_W, P_GLA_V, tc), col(GROUP_W, P_GLA_G, tc),
                col(LANES, 0, tc),
                full(wlr.shape), full(blr.shape), full(ng.shape)]
    out_specs = [pl.BlockSpec((1, t, GROUP_W), lambda i: (i, 0, 0))]
    out_shape = [jax.ShapeDtypeStruct((b, t, GROUP_W), BF16)]
    scratch = [pltpu.VMEM((t, 2 * GLA_QK_W), F32), pltpu.VMEM((tc, 2 * GLA_QK_W), F32),
               pltpu.VMEM((t, GROUP_W), F32), pltpu.VMEM((t, GROUP_W), F32)]
    if ctx_out:
        out_specs.append(pl.BlockSpec((1, tc, GROUP_W), lambda i: (i, 0, 0)))
        out_shape.append(jax.ShapeDtypeStruct((b, tc, GROUP_W), BF16))
        scratch += [pltpu.VMEM((tc, GROUP_W), F32), pltpu.VMEM((tc, GROUP_W), F32)]
    scratch += [pltpu.VMEM((GROUP_W, GLA_QK_W), F32), pltpu.VMEM((GROUP_W, GLA_QK_W), F32)]
    res = pl.pallas_call(
        functools.partial(_gla_kernel, ctx_out=ctx_out),
        grid=(b,),
        in_specs=in_specs, out_specs=out_specs, out_shape=out_shape, scratch_shapes=scratch,
        compiler_params=_cparams(("parallel",)),
        name="gla",
    )(p, p, p, ps, pc, pc, pc, pcs, wlr, blr, ng)
    return (res[0], res[1]) if ctx_out else (res[0], None)


ML_BLOCK = 128
N_PAIRS = N_HEADS // 2


def _gate_lane(d, h):
    return SMALL_GATE_LANE + 8 * d + h


def _mlstm_selectors():
    sel_head = np.zeros((LANES, 2 * N_HEADS * LANES), np.float32)
    sel_pair = np.zeros((LANES, 2 * N_PAIRS * LANES), np.float32)
    for d in range(2):
        for h in range(N_HEADS):
            sel_head[_gate_lane(d, h), (d * N_HEADS + h) * LANES:(d * N_HEADS + h + 1) * LANES] = 1.0
            p, hh = divmod(h, 2)
            lo = (d * N_PAIRS + p) * LANES + hh * HEAD_DIM
            sel_pair[_gate_lane(d, h), lo:lo + HEAD_DIM] = 1.0
    return sel_head, sel_pair


def _mlstm_block(blk_f, blk_b, cn_ref, m_prev, selh_ref, selp_ref, want_out):
    L = ML_BLOCK
    lane = _iota((1, LANES), 1)
    is_b = (lane >= _gate_lane(1, 0)) & (lane < _gate_lane(1, 0) + N_HEADS)
    valid = ((lane >= _gate_lane(0, 0)) & (lane < _gate_lane(0, 0) + N_HEADS)) | is_b
    r, c = _iota((L, L), 0), _iota((L, L), 1)
    keep_f, keep_b = c <= r, c >= r
    lf_f = pltpu.roll(_log_sigmoid(blk_f[2]), LANES - 4, 1)
    lf_b = pltpu.roll(_log_sigmoid(blk_b[2]), LANES - 4, 1)
    cum_f = _tri_dot(jnp.where(keep_f, 1.0, 0.0), lf_f)
    cum_b = _tri_dot(jnp.where(keep_b, 1.0, 0.0), lf_b)
    cum = jnp.where(valid, jnp.where(is_b, cum_b, cum_f), 0.0)
    ig = jnp.where(valid, jnp.where(is_b, blk_b[2], blk_f[2]), 0.0)
    tot = jnp.where(valid, jnp.where(is_b, cum_b[0:1, :], cum_f[L - 1:L, :]), 0.0)
    b = ig - cum
    row = _iota((L, 1), 0)
    pm = b
    k = 1
    while k < L:
        from_prev = jnp.where(row >= k, pltpu.roll(pm, k, 0), -jnp.inf)
        from_next = jnp.where(row < L - k, pltpu.roll(pm, L - k, 0), -jnp.inf)
        pm = jnp.maximum(pm, jnp.where(is_b, from_next, from_prev))
        k *= 2
    mm = jnp.maximum(m_prev, pm)
    w_end = tot + b
    m_chunk = jnp.max(w_end, axis=0, keepdims=True)
    m_new = jnp.maximum(tot + m_prev, m_chunk)
    a_row = jnp.exp(tot + m_prev - m_new)
    g_row = jnp.exp(m_chunk - m_new)
    cols = [jnp.exp(w_end - m_chunk)]
    if want_out:
        cols += [jnp.exp(m_prev - mm), jnp.exp(-(cum + mm))]
        mm_hi, mm_lo, _ = _split3(mm)
        mm_t = (jnp.dot(mm_hi, selh_ref[...], preferred_element_type=F32)
                + jnp.dot(mm_lo, selh_ref[...], preferred_element_type=F32))
        b_rows = b.T
    pair_t = _bdot(jnp.concatenate(cols, axis=0), selp_ref[...])
    lo_half = lane < HEAD_DIM
    ones_t = jnp.ones((L, LANES), F32)
    bd = (_iota((LANES, 2 * LANES), 0) // HEAD_DIM) == ((_iota((LANES, 2 * LANES), 1) % LANES) // HEAD_DIM)
    outs = {}
    for d, (blk, keep) in enumerate(((blk_f, keep_f), (blk_b, keep_b))):
        qk, v, _ = blk
        for p in range(N_PAIRS):
            qp = qk[:, p * LANES:(p + 1) * LANES]
            kp = qk[:, GROUP_W + p * LANES:GROUP_W + (p + 1) * LANES]
            vp = v[:, p * LANES:(p + 1) * LANES]
            cn = cn_ref[d * N_PAIRS + p]
            tcol = (d * N_PAIRS + p) * LANES
            if want_out:
                nd = None
                for hh in range(2):
                    h = 2 * p + hh
                    mine = lo_half if hh == 0 else jnp.logical_not(lo_half)
                    gl = _gate_lane(d, h)
                    hcol = (d * N_HEADS + h) * LANES
                    e = jnp.where(keep, jnp.exp(b_rows[gl:gl + 1, :] - mm_t[:, hcol:hcol + LANES]), 0.0)
                    s = e * _bdot_nt(jnp.where(mine, qp, 0.0), kp)
                    va = jnp.concatenate([jnp.where(mine, vp, 0.0), jnp.where(mine, ones_t, 0.0)], axis=1)
                    t = _bdot(s, va)
                    nd = t if nd is None else nd + t
                w_int = pair_t[L:2 * L, tcol:tcol + LANES]
                nd = nd + jnp.concatenate([w_int, w_int], axis=1) * _bdot(qp, cn)
                outs[(d, p)] = nd[:, :LANES] / jnp.maximum(jnp.abs(nd[:, LANES:]),
                                                           pair_t[2 * L:3 * L, tcol:tcol + LANES])
            kw_t = (kp * pair_t[0:L, tcol:tcol + LANES]).T
            upd = _bdot(kw_t, jnp.concatenate([vp, ones_t], axis=1))
            ga, gb = _gate_lane(d, 2 * p), _gate_lane(d, 2 * p + 1)
            half = (HEAD_DIM, 2 * LANES)
            a_t = jnp.concatenate([jnp.broadcast_to(a_row[:, ga:ga + 1], half),
                                   jnp.broadcast_to(a_row[:, gb:gb + 1], half)], axis=0)
            g_t = jnp.concatenate([jnp.broadcast_to(g_row[:, ga:ga + 1], half),
                                   jnp.broadcast_to(g_row[:, gb:gb + 1], half)], axis=0)
            cn_ref[d * N_PAIRS + p] = a_t * cn + g_t * jnp.where(bd, upd, 0.0)
    return (outs if want_out else None), m_new


def _mlstm_kernel(qk_ref, v_ref, og_ref, sm_ref, qkc_ref, vc_ref, ogc_ref, smc_ref, cw_ref, gb_ref,
                  selh_ref, selp_ref, *rest, ctx_out):
    if ctx_out:
        o_ref, oc_ref, qc_sc, qcc_sc, hf_sc, hb_sc, hfc_sc, hbc_sc, cn_sc = rest
    else:
        o_ref, qc_sc, qcc_sc, hf_sc, hb_sc, cn_sc = rest
        oc_ref = hfc_sc = hbc_sc = None
    t = qk_ref.shape[1]
    tc = qkc_ref.shape[1]
    w2 = 2 * GROUP_W
    kscale = HEAD_DIM ** -0.5
    n_rows_grid = t // GRID_W

    def tap(dr, dc):
        return cw_ref[dr * 3 + dc:dr * 3 + dc + 1, :]

    def shifted(blk, dc):
        n = blk.shape[0]
        if dc == 1:
            return blk
        ridx = _iota((n, 1), 0)
        if dc == 0:
            return jnp.where(ridx == 0, 0.0, pltpu.roll(blk, 1, 0))
        return jnp.where(ridx == n - 1, 0.0, pltpu.roll(blk, n - 1, 0))

    def finish_qk(acc):
        a = _silu(acc)
        lane = _iota((1, w2), 1)
        return jnp.where(lane >= GROUP_W, a * kscale, a)

    def conv_body(i, carry):
        r0 = pl.multiple_of(i * GRID_W, GRID_W)
        r_up = pl.multiple_of(jnp.maximum(i - 1, 0) * GRID_W, GRID_W)
        r_dn = pl.multiple_of(jnp.minimum(i + 1, n_rows_grid - 1) * GRID_W, GRID_W)
        up = qk_ref[0, pl.ds(r_up, GRID_W), :].astype(F32) * jnp.where(i > 0, 1.0, 0.0)
        mid = qk_ref[0, pl.ds(r0, GRID_W), :].astype(F32)
        dn = qk_ref[0, pl.ds(r_dn, GRID_W), :].astype(F32) * jnp.where(i < n_rows_grid - 1, 1.0, 0.0)
        acc = jnp.zeros((GRID_W, w2), F32)
        for dr, blk in enumerate((up, mid, dn)):
            for dc in range(3):
                acc = acc + shifted(blk, dc) * tap(dr, dc)
        qc_sc[pl.ds(r0, GRID_W), :] = finish_qk(acc)
        return carry

    lax.fori_loop(0, n_rows_grid, conv_body, 0)
    xc = qkc_ref[0].astype(F32)
    acc = jnp.zeros((tc, w2), F32)
    for dc in range(3):
        acc = acc + shifted(xc, dc) * tap(1, dc)
    qcc_sc[...] = finish_qk(acc)

    cn_sc[...] = jnp.zeros_like(cn_sc)

    def run(n_rows, qc, v, sm, hf, hb, want_out, m0):
        n = n_rows // ML_BLOCK

        def body(i, m):
            rf = pl.multiple_of(i * ML_BLOCK, ML_BLOCK)
            rb = pl.multiple_of((n - 1 - i) * ML_BLOCK, ML_BLOCK)
            blks = [(qc[pl.ds(r0, ML_BLOCK), :], v[0, pl.ds(r0, ML_BLOCK), :].astype(F32),
                     sm[0, pl.ds(r0, ML_BLOCK), :] + gb_ref[...]) for r0 in (rf, rb)]
            outs, m_new = _mlstm_block(blks[0], blks[1], cn_sc, m, selh_ref, selp_ref, want_out)
            if want_out:
                for p in range(N_PAIRS):
                    hf[pl.ds(rf, ML_BLOCK), p * LANES:(p + 1) * LANES] = outs[(0, p)]
                    hb[pl.ds(rb, ML_BLOCK), p * LANES:(p + 1) * LANES] = outs[(1, p)]
            return m_new

        return lax.fori_loop(0, n, body, m0)

    m1 = run(tc, qcc_sc, vc_ref, smc_ref, hfc_sc, hbc_sc, ctx_out, jnp.zeros((1, LANES), F32))
    run(t, qc_sc, v_ref, sm_ref, hf_sc, hb_sc, True, m1)

    def finish(n_rows, hf, hb, og, dst):
        blk = min(n_rows, 256)

        def body(i, carry):
            r0 = pl.multiple_of(i * blk, blk)
            res = _sigmoid(og[0, pl.ds(r0, blk), :].astype(F32)) * (hf[pl.ds(r0, blk), :] + hb[pl.ds(r0, blk), :])
            dst[0, pl.ds(r0, blk), :] = res.astype(dst.dtype)
            return carry

        lax.fori_loop(0, n_rows // blk, body, 0)

    finish(t, hf_sc, hb_sc, og_ref, o_ref)
    if ctx_out:
        finish(tc, hfc_sc, hbc_sc, ogc_ref, oc_ref)


def _mlstm(p, ps, pc, pcs, conv_w, gate_b, ctx_out):
    b, t, _ = p.shape
    tc = pc.shape[1]
    sel_head, sel_pair = (jnp.asarray(a).astype(BF16) for a in _mlstm_selectors())

    def col(width, idx, rows):
        return pl.BlockSpec((1, rows, width), lambda i: (i, 0, idx))

    full = lambda shape: pl.BlockSpec(shape, lambda i: tuple(0 for _ in shape))
    in_specs = [col(2 * GROUP_W, P_ML_QK, t), col(GROUP_W, P_ML_V, t), col(GROUP_W, P_ML_OG, t),
                col(LANES, 0, t),
                col(2 * GROUP_W, P_ML_QK, tc), col(GROUP_W, P_ML_V, tc), col(GROUP_W, P_ML_OG, tc),
                col(LANES, 0, tc),
                full(conv_w.shape), full(gate_b.shape), full(sel_head.shape), full(sel_pair.shape)]
    out_specs = [pl.BlockSpec((1, t, GROUP_W), lambda i: (i, 0, 0))]
    out_shape = [jax.ShapeDtypeStruct((b, t, GROUP_W), BF16)]
    scratch = [pltpu.VMEM((t, 2 * GROUP_W), F32), pltpu.VMEM((tc, 2 * GROUP_W), F32),
               pltpu.VMEM((t, GROUP_W), F32), pltpu.VMEM((t, GROUP_W), F32)]
    if ctx_out:
        out_specs.append(pl.BlockSpec((1, tc, GROUP_W), lambda i: (i, 0, 0)))
        out_shape.append(jax.ShapeDtypeStruct((b, tc, GROUP_W), BF16))
        scratch += [pltpu.VMEM((tc, GROUP_W), F32), pltpu.VMEM((tc, GROUP_W), F32)]
    scratch += [pltpu.VMEM((2 * N_PAIRS, LANES, 2 * LANES), F32)]
    res = pl.pallas_call(
        functools.partial(_mlstm_kernel, ctx_out=ctx_out),
        grid=(b,),
        in_specs=in_specs, out_specs=out_specs, out_shape=out_shape, scratch_shapes=scratch,
        compiler_params=_cparams(("parallel",)),
        name="mlstm",
    )(p, p, p, ps, pc, pc, pc, pcs, conv_w, gate_b, sel_head, sel_pair)
    return (res[0], res[1]) if ctx_out else (res[0], None)


def _sgu_kernel(u_ref, v_ref, f_ref, ws_ref, bias_ref, dft_ref, o_ref, z_ref):
    t = u_ref.shape[1]
    lane_g = _iota((1, GROUP_W), 1) // HEAD_DIM

    def body(i, carry):
        r0 = pl.multiple_of(i * SGU_CHUNK, SGU_CHUNK)
        v = v_ref[0, pl.ds(r0, SGU_CHUNK), :].astype(F32)
        mu = jnp.mean(v, axis=-1, keepdims=True)
        vc = v - mu
        vn = vc * lax.rsqrt(jnp.mean(vc * vc, axis=-1, keepdims=True) + EPS)
        mixed = bias_ref[...]
        for g in range(N_HEADS):
            mixed = mixed + _bdot(ws_ref[g], jnp.where(lane_g == g, vn, 0.0))
        o_ref[0, pl.ds(r0, SGU_CHUNK), :] = (u_ref[0, pl.ds(r0, SGU_CHUNK), :].astype(F32) * mixed).astype(o_ref.dtype)
        z = _bdot(f_ref[0, pl.ds(r0, SGU_CHUNK), :], dft_ref[...])
        z_ref[0, 0, pl.ds(r0, SGU_CHUNK), :] = z[:, :GROUP_W].astype(z_ref.dtype)
        z_ref[1, 0, pl.ds(r0, SGU_CHUNK), :] = z[:, GROUP_W:].astype(z_ref.dtype)
        return carry

    lax.fori_loop(0, t // SGU_CHUNK, body, 0)


def _sgu_fourier1(p, ws, bias, dft_c):
    b, t, _ = p.shape

    def col(idx):
        return pl.BlockSpec((1, t, GROUP_W), lambda i: (i, 0, idx))

    full = lambda shape: pl.BlockSpec(shape, lambda i: tuple(0 for _ in shape))
    return pl.pallas_call(
        _sgu_kernel,
        grid=(b,),
        in_specs=[col(P_SGU_U), col(P_SGU_V), col(P_FOUR), full(ws.shape), full(bias.shape), full(dft_c.shape)],
        out_specs=[pl.BlockSpec((1, t, GROUP_W), lambda i: (i, 0, 0)),
                   pl.BlockSpec((2, 1, t, GROUP_W), lambda i: (0, i, 0, 0))],
        out_shape=[jax.ShapeDtypeStruct((b, t, GROUP_W), BF16), jax.ShapeDtypeStruct((2, b, t, GROUP_W), BF16)],
        compiler_params=_cparams(("parallel",)),
        name="sgu_fourier1",
    )(p, p, p, ws, bias, dft_c)


def _fourier2_kernel(ct_ref, st_ref, z_ref, o_ref, *, scale):
    for i in range(z_ref.shape[1]):
        acc = jnp.dot(ct_ref[...], z_ref[0, i], preferred_element_type=F32)
        acc -= jnp.dot(st_ref[...], z_ref[1, i], preferred_element_type=F32)
        o_ref[i] = (acc * scale).astype(o_ref.dtype)


def _fourier2(z, cos_t, sin_t):
    _, b, t, w = z.shape
    tm = min(t, 256)
    return pl.pallas_call(
        functools.partial(_fourier2_kernel, scale=float((t * HEAD_DIM) ** -0.5)),
        grid=(t // tm,),
        in_specs=[pl.BlockSpec((tm, t), lambda i: (i, 0)),
                  pl.BlockSpec((tm, t), lambda i: (i, 0)),
                  pl.BlockSpec((2, b, t, w), lambda i: (0, 0, 0, 0), pipeline_mode=pl.Buffered(1))],
        out_specs=pl.BlockSpec((b, tm, w), lambda i: (0, i, 0)),
        out_shape=jax.ShapeDtypeStruct((b, t, w), BF16),
        compiler_params=_cparams(("parallel",)),
        name="fourier2",
    )(cos_t, sin_t, z)


def _dft_tables(n):
    k = np.arange(n, dtype=np.int64)
    ang = 2.0 * np.pi * ((k[:, None] * k[None, :]) % n).astype(np.float64) / n
    return np.cos(ang).astype(np.float32), np.sin(ang).astype(np.float32)


def _channel_dft():
    c, s = _dft_tables(HEAD_DIM)
    eye = np.eye(N_HEADS, dtype=np.float32)
    return np.concatenate([np.kron(eye, c), np.kron(eye, s)], axis=1)


def _route_kernel(x_ref, mod_ref, g_ref, wr_ref, h_ref, mi_ref, mf_ref, cnt_ref, cnt_sc):
    first = (pl.program_id(0) == 0) & (pl.program_id(1) == 0)

    @pl.when(first)
    def _():
        cnt_sc[...] = jnp.zeros_like(cnt_sc)

    tm = x_ref.shape[1]
    h = _norm_mod(x_ref[0], g_ref[...], mod_ref[0, 3:4, :], mod_ref[0, 4:5, :])
    for j in range(D_MODEL // LANES):
        h_ref[pl.ds(j, tm, stride=SUBLANES), :] = h[:, j * LANES:(j + 1) * LANES]
    logits = _fdot(h, wr_ref[...])
    lane = _iota((tm, LANES), 1).astype(F32)
    lg = jnp.where(lane < N_EXPERTS, logits, -jnp.inf)
    m1 = jnp.max(lg, axis=1, keepdims=True)
    i1 = jnp.min(jnp.where(lg == m1, lane, float(LANES)), axis=1, keepdims=True)
    lg2 = jnp.where(lane == i1, -jnp.inf, lg)
    m2 = jnp.max(lg2, axis=1, keepdims=True)
    i2 = jnp.min(jnp.where(lg2 == m2, lane, float(LANES)), axis=1, keepdims=True)
    e = jnp.exp(m2 - m1)
    g0 = 1.0 / (1.0 + e)
    g1 = e / (1.0 + e)
    onehot = jnp.where((lane == i1) | (lane == i2), 1.0, 0.0)
    strict = jnp.where(_iota((tm, tm), 1) < _iota((tm, tm), 0), 1.0, 0.0)
    before = _bdot(strict, onehot) + cnt_sc[...]
    r0 = jnp.sum(jnp.where(lane == i1, before, 0.0), axis=1, keepdims=True)
    r1 = jnp.sum(jnp.where(lane == i2, before, 0.0), axis=1, keepdims=True)
    cnt_sc[...] += jnp.sum(onehot, axis=0, keepdims=True)
    cnt_ref[...] = cnt_sc[...]
    l8 = _iota((tm, SUBLANES), 1)
    mi = jnp.where(l8 == 0, i1, jnp.where(l8 == 1, i2, jnp.where(l8 == 2, r0, r1)))
    mi_ref[...] = mi.astype(jnp.int32)
    mf_ref[...] = jnp.where(l8 == 0, g0, g1)


def _route(x, mod, g, w_router):
    b, t, d = x.shape
    m = b * t
    tm = 512
    nt = t // tm
    wr = jnp.zeros((d, LANES), F32).at[:, :N_EXPERTS].set(w_router)
    return pl.pallas_call(
        _route_kernel,
        grid=(b, nt),
        in_specs=[pl.BlockSpec((1, tm, d), lambda i, j: (i, j, 0)),
                  pl.BlockSpec((1, 6, d), lambda i, j: (i, 0, 0)),
                  pl.BlockSpec((1, d), lambda i, j: (0, 0)),
                  pl.BlockSpec((d, LANES), lambda i, j: (0, 0))],
        out_specs=[pl.BlockSpec((tm * SUBLANES, LANES), lambda i, j: (i * nt + j, 0)),
                   pl.BlockSpec((tm, SUBLANES), lambda i, j: (i * nt + j, 0)),
                   pl.BlockSpec((tm, SUBLANES), lambda i, j: (i * nt + j, 0)),
                   pl.BlockSpec((1, LANES), lambda i, j: (0, 0))],
        out_shape=[jax.ShapeDtypeStruct((m * SUBLANES, LANES), F32),
                   jax.ShapeDtypeStruct((m, SUBLANES), jnp.int32),
                   jax.ShapeDtypeStruct((m, SUBLANES), F32),
                   jax.ShapeDtypeStruct((1, LANES), F32)],
        scratch_shapes=[pltpu.VMEM((1, LANES), F32)],
        compiler_params=_cparams(("arbitrary", "arbitrary")),
        name="route",
    )(x, mod, g.reshape(1, d), wr)


def _moe_kernel(texp_ref, trows_ref, pos0_ref, pos1_ref, h_hbm, wg_ref, wu_ref, wd_ref, y_hbm,
                order_sm, xbuf, x_sc, acc_sc, ybuf, gsem, ssem):
    i = pl.program_id(0)
    j = pl.program_id(1)
    nf = pl.num_programs(1)
    tm = MOE_TM
    n_tok = pos0_ref.shape[0]
    rows = trows_ref[i]

    nt = pl.num_programs(0)
    buf_rows = tm * SUBLANES
    dump = TOP_K * n_tok
    slot = lax.rem(i, 2)
    prev_rows = trows_ref[jnp.maximum(i - 1, 0)]
    next_rows = trows_ref[jnp.minimum(i + 1, nt - 1)]
    n_full, n_rest = divmod(tm, MOE_COPY_ROWS)

    def gather_copy(tile, n_real, r, dst_slot):
        idx = jnp.where(n_real > 0, tile * tm + jnp.minimum(r, n_real - 1), 0)
        tok = jnp.right_shift(order_sm[idx], 1)
        off = pl.multiple_of(dst_slot * buf_rows + r * SUBLANES, SUBLANES)
        return pltpu.make_async_copy(h_hbm.at[tok], xbuf.at[pl.ds(off, SUBLANES), :], gsem)

    def scatter_copy(tile, n_real, r, src_slot):
        idx = jnp.where(n_real > 0, tile * tm + jnp.minimum(r, n_real - 1), 0)
        dst = jnp.where(r < n_real, order_sm[idx], dump + r)
        off = pl.multiple_of(src_slot * buf_rows + r * SUBLANES, SUBLANES)
        return pltpu.make_async_copy(ybuf.at[pl.ds(off, SUBLANES), :], y_hbm.at[dst], ssem)

    def wait_all(buf, sem):
        pltpu.make_async_copy(buf.at[pl.ds(0, buf_rows), :], buf.at[pl.ds(buf_rows, buf_rows), :], sem).wait()

    @pl.when((i == 0) & (j == 0))
    def _():
        def inv_body(t, carry):
            order_sm[pos0_ref[t]] = 2 * t
            order_sm[pos1_ref[t]] = 2 * t + 1
            return carry

        lax.fori_loop(0, n_tok, inv_body, 0, unroll=8)
        ybuf[...] = jnp.zeros_like(ybuf)

        def start(r, carry):
            gather_copy(0, rows, r, 0).start()
            return carry

        lax.fori_loop(0, tm, start, 0)

    @pl.when((j == 0) & (rows > 0))
    def _():
        wait_all(xbuf, gsem)
        base = pl.multiple_of(slot * buf_rows, SUBLANES)
        for c in range(D_MODEL // LANES):
            x_sc[:, c * LANES:(c + 1) * LANES] = xbuf[pl.ds(base + c, tm, stride=SUBLANES), :].astype(BF16)
        acc_sc[...] = jnp.zeros_like(acc_sc)

    def step(n_copies):
        r0 = j * MOE_COPY_ROWS
        for q in range(n_copies):
            gather_copy(i + 1, next_rows, r0 + q, 1 - slot).start()
            scatter_copy(i - 1, jnp.where(i > 0, prev_rows, 0), r0 + q, 1 - slot).start()
        x = x_sc[...]
        gate = jnp.dot(x, wg_ref[0].astype(BF16), preferred_element_type=F32)
        up = jnp.dot(x, wu_ref[0].astype(BF16), preferred_element_type=F32)
        acc_sc[...] += jnp.dot((_silu(gate) * up).astype(BF16), wd_ref[0].astype(BF16),
                               preferred_element_type=F32)

    pl.when((rows > 0) & (j < n_full))(functools.partial(step, MOE_COPY_ROWS))
    pl.when((rows > 0) & (j == n_full))(functools.partial(step, n_rest))
    pl.when((rows > 0) & (j > n_full))(functools.partial(step, 0))

    @pl.when((j == nf - 1) & (rows > 0))
    def _():
        wait_all(ybuf, ssem)
        base = pl.multiple_of(slot * buf_rows, SUBLANES)
        for c in range(D_MODEL // LANES):
            ybuf[pl.ds(base + c, tm, stride=SUBLANES), :] = acc_sc[:, c * LANES:(c + 1) * LANES]

    @pl.when((j == 0) & (rows == 0) & (i > 0) & (prev_rows > 0))
    def _():
        wait_all(xbuf, gsem)

        def start(r, carry):
            scatter_copy(i - 1, prev_rows, r, 1 - slot).start()
            return carry

        lax.fori_loop(0, tm, start, 0)
        wait_all(ybuf, ssem)


def _moe(h_tiles, pos0, pos1, tile_expert, tile_rows, wg, wu, wd):
    m = h_tiles.shape[0]
    n_exp, d, f = wg.shape
    nt = tile_expert.shape[0]
    tm, tf = MOE_TM, MOE_TF
    nf = f // tf
    assert nf * MOE_COPY_ROWS >= tm, "each tile needs enough column steps to issue its neighbours' row copies"
    assert nt * tm > TOP_K * m + n_exp * (tm - 1), "the last tile must be unused: it drains the copy pipeline"

    def col_tile(i, j, tr):
        return jnp.where(tr[i] > 0, j, nf - 1)

    grid_spec = pltpu.PrefetchScalarGridSpec(
        num_scalar_prefetch=4,
        grid=(nt, nf),
        in_specs=[pl.BlockSpec(memory_space=pl.ANY),
                  pl.BlockSpec((1, d, tf), lambda i, j, te, tr, p0, p1: (te[i], 0, col_tile(i, j, tr))),
                  pl.BlockSpec((1, d, tf), lambda i, j, te, tr, p0, p1: (te[i], 0, col_tile(i, j, tr))),
                  pl.BlockSpec((1, tf, d), lambda i, j, te, tr, p0, p1: (te[i], col_tile(i, j, tr), 0))],
        out_specs=pl.BlockSpec(memory_space=pl.ANY),
        scratch_shapes=[pltpu.SMEM((nt * tm,), jnp.int32),
                        pltpu.VMEM((2 * tm * SUBLANES, LANES), F32),
                        pltpu.VMEM((tm, d), BF16),
                        pltpu.VMEM((tm, d), F32),
                        pltpu.VMEM((2 * tm * SUBLANES, LANES), F32),
                        pltpu.SemaphoreType.DMA(()),
                        pltpu.SemaphoreType.DMA(())],
    )
    return pl.pallas_call(
        _moe_kernel,
        grid_spec=grid_spec,
        out_shape=jax.ShapeDtypeStruct((TOP_K * m + tm, SUBLANES, LANES), F32),
        compiler_params=_cparams(("arbitrary", "arbitrary")),
        name="moe",
    )(tile_expert, tile_rows, pos0, pos1, h_tiles, wg, wu, wd)


def _combine_kernel(x_ref, mod_ref, y_ref, mf_ref, fg_ref, o_ref):
    tm = x_ref.shape[1]
    stride = TOP_K * SUBLANES
    g0 = mf_ref[:, 0:1]
    g1 = mf_ref[:, 1:2]
    parts = []
    for c in range(D_MODEL // LANES):
        y0 = y_ref[pl.ds(c, tm, stride=stride), :]
        y1 = y_ref[pl.ds(SUBLANES + c, tm, stride=stride), :]
        parts.append(g0 * y0 + g1 * y1)
    moe = jnp.concatenate(parts, axis=1)
    x = x_ref[0] + mod_ref[0, 5:6, :] * moe
    ms = jnp.mean(x * x, axis=-1, keepdims=True)
    o_ref[0] = x * lax.rsqrt(ms + EPS) * fg_ref[...]


def _combine(x, mod, y, mf, final_g):
    b, t, d = x.shape
    tm = 256
    nt = t // tm
    rows = tm * TOP_K * SUBLANES
    return pl.pallas_call(
        _combine_kernel,
        grid=(b, nt),
        in_specs=[pl.BlockSpec((1, tm, d), lambda i, j: (i, j, 0)),
                  pl.BlockSpec((1, 6, d), lambda i, j: (i, 0, 0)),
                  pl.BlockSpec((rows, LANES), lambda i, j: (i * nt + j, 0)),
                  pl.BlockSpec((tm, SUBLANES), lambda i, j: (i * nt + j, 0)),
                  pl.BlockSpec((1, d), lambda i, j: (0, 0))],
        out_specs=pl.BlockSpec((1, tm, d), lambda i, j: (i, j, 0)),
        out_shape=jax.ShapeDtypeStruct((b, t, d), F32),
        compiler_params=_cparams(("parallel", "parallel")),
        name="combine",
    )(x, mod, y, mf, final_g.reshape(1, d))


def _moe_layer(x, mod, norm_g, w_router, wg, wu, wd, final_g):
    b, t, d = x.shape
    m = b * t
    h_tiles, mi, mf, cnt = _route(x, mod, norm_g, w_router)
    counts = cnt[0, :N_EXPERTS].astype(jnp.int32)
    tiles_per = (counts + MOE_TM - 1) // MOE_TM
    tile_end = jnp.cumsum(tiles_per)
    tile_start = tile_end - tiles_per
    seg_start = tile_start * MOE_TM
    nt = (TOP_K * m) // MOE_TM + N_EXPERTS
    tile_id = jnp.arange(nt, dtype=jnp.int32)
    used_id = jnp.minimum(tile_id, tile_end[-1] - 1)
    tile_expert = jnp.sum(used_id[:, None] >= tile_end[None, :], axis=1).astype(jnp.int32)
    tile_rows = jnp.clip(counts[tile_expert] - (tile_id - tile_start[tile_expert]) * MOE_TM, 0, MOE_TM)
    tile_rows = jnp.where(tile_id < tile_end[-1], tile_rows, 0).astype(jnp.int32)
    pos0 = seg_start[mi[:, 0]] + mi[:, 2]
    pos1 = seg_start[mi[:, 1]] + mi[:, 3]
    y = _moe(h_tiles.reshape(m, SUBLANES, LANES), pos0, pos1, tile_expert, tile_rows,
             wg.astype(BF16), wu.astype(BF16), wd.astype(BF16))
    return _combine(x, mod, y.reshape(-1, LANES), mf, final_g)


def _permute_w_in(w):
    d = w.shape[0]
    return jnp.concatenate([w[:, 0:768], w[:, 1312:1568], w[:, 800:1312], w[:, 1568:1824], w[:, 1840:2608],
                            w[:, 768:800], w[:, 1824:1840], jnp.zeros((d, P_WIDTH - 2608), w.dtype)],
                           axis=1).astype(BF16)


def _mixers(proj, proj_c, gla_w_lr, gla_b_lr, gla_norm_g, mlstm_conv, mlstm_gate_b, sgu_w, sgu_b, ctx_out, tables):
    (p, ps), (pc, pcs) = proj, proj_c
    wlr = jnp.zeros((LANES, 2 * GLA_QK_W), F32)
    wlr = wlr.at[0:GLA_RANK, 0:GLA_QK_W].set(gla_w_lr[0]).at[GLA_RANK:2 * GLA_RANK, GLA_QK_W:].set(gla_w_lr[1])
    blr = gla_b_lr.reshape(1, 2 * GLA_QK_W)
    gla, gla_c = _gla(p, ps, pc, pcs, wlr, blr, gla_norm_g.reshape(1, GROUP_W), ctx_out)
    conv_w = mlstm_conv.reshape(9, 2 * GROUP_W)
    gate_b = jnp.zeros((1, LANES), F32).at[0, SMALL_GATE_LANE:SMALL_GATE_LANE + 4 * N_HEADS].set(
        mlstm_gate_b.reshape(-1))
    ml, ml_c = _mlstm(p, ps, pc, pcs, conv_w, gate_b, ctx_out)
    bias = jnp.repeat(sgu_b.T, HEAD_DIM, axis=1)
    dft_c, tabs = tables
    sg, z = _sgu_fourier1(p, sgu_w, bias, dft_c)
    fo = _fourier2(z, *tabs[p.shape[1]])
    mix = (gla, ml, sg, fo)
    if not ctx_out:
        return mix, None
    sg_c, z_c = _sgu_fourier1(pc, sgu_w, bias, dft_c)
    fo_c = _fourier2(z_c, *tabs[pc.shape[1]])
    return mix, (gla_c, ml_c, sg_c, fo_c)


def kernel(x, c, ctx, c_ctx, w_ada, b_ada, norm_mix_g, norm_ffn_g, w_in, w_out, gla_w_lr, gla_b_lr, gla_norm_g,
           mlstm_conv, mlstm_gate_b, sgu_w, sgu_b, ffn_w_gate, ffn_w_up, ffn_w_down, moe_router, moe_w_gate,
           moe_w_up, moe_w_down, final_norm_g):
    depth = w_ada.shape[0]
    assert depth == 2, "layer 0 is the dense layer with context output, layer 1 the MoE layer"
    b, t, d = x.shape
    tc = ctx.shape[1]
    cvec = jnp.zeros((16, d), F32).at[:b].set(c).at[b].set(c_ctx)
    mods = _ada(cvec, w_ada, b_ada).reshape(depth, 16, 6, d)
    tables = (jnp.asarray(_channel_dft()).astype(BF16),
              {n: tuple(jnp.asarray(a).astype(BF16) for a in _dft_tables(n)) for n in {t, tc}})
    xc = ctx
    for l in range(depth):
        ctx_out = l < depth - 1
        mod, mod_c = mods[l, :b], mods[l, b:b + 1]
        w_in_l = _permute_w_in(w_in[l])
        w_out_l = w_out[l].astype(BF16)
        p = _in_proj(x, mod, norm_mix_g[l], w_in_l)
        pc = _in_proj(xc, mod_c, norm_mix_g[l], w_in_l)
        mix, mix_c = _mixers(p, pc, gla_w_lr[l], gla_b_lr[l], gla_norm_g[l], mlstm_conv[l], mlstm_gate_b[l],
                             sgu_w[l], sgu_b[l], ctx_out, tables)
        x = _out_proj(x, mod, mix, w_out_l)
        if ctx_out:
            xc = _out_proj(xc, mod_c, mix_c, w_out_l)
        i = l // 2
        if l % 2 == 0:
            wg, wu, wd = ffn_w_gate[i].astype(BF16), ffn_w_up[i].astype(BF16), ffn_w_down[i].astype(BF16)
            x = _ffn(x, mod, norm_ffn_g[l], wg, wu, wd)
            if ctx_out:
                xc = _ffn(xc, mod_c, norm_ffn_g[l], wg, wu, wd)
        else:
            x = _moe_layer(x, mod, norm_ffn_g[l], moe_router[i], moe_w_gate[i], moe_w_up[i], moe_w_down[i],
                           final_norm_g)
    return x
```

```python
import functools

import numpy as np
import jax
import jax.numpy as jnp
from jax import lax
from jax.experimental import pallas as pl
from jax.experimental.pallas import tpu as pltpu

F32 = jnp.float32
BF16 = jnp.bfloat16
HIGHEST = lax.Precision.HIGHEST

D_MODEL = 1024
GROUP_W = 256
N_HEADS = 4
HEAD_DIM = 64
GLA_DK = 32
GLA_QK_W = N_HEADS * GLA_DK
GLA_RANK = 16
GLA_NORMALIZER = 16.0
GLA_BLOCK = 128
GRID_W = 64
SGU_CHUNK = 128
N_EXPERTS = 8
TOP_K = 2
EPS = 1e-6

LANES = 128
SUBLANES = 8
VMEM_LIMIT = 52 * 1024 * 1024

P_GLA_QK, P_GLA_V, P_GLA_G, P_ML_V = 0, 1, 2, 3
P_ML_QK = 2
P_ML_OG, P_SGU_U, P_SGU_V, P_FOUR = 6, 7, 8, 9
P_MAIN_W = 2560
P_WIDTH = 2688
SMALL_GATE_LANE = 32

MOE_TM = 1024
MOE_TF = 512
MOE_COPY_ROWS = 171


def _cparams(sem, **kw):
    return pltpu.CompilerParams(dimension_semantics=sem, vmem_limit_bytes=VMEM_LIMIT, **kw)


def _sigmoid(x):
    return 1.0 / (1.0 + jnp.exp(-x))


def _silu(x):
    return x * _sigmoid(x)


def _log_sigmoid(x):
    return jnp.minimum(x, 0.0) - jnp.log(1.0 + jnp.exp(-jnp.abs(x)))


def _norm_mod(x, g, shift, scale):
    ms = jnp.mean(x * x, axis=-1, keepdims=True)
    return (x * lax.rsqrt(ms + EPS) * g) * (1.0 + scale) + shift


def _bdot(a, b):
    return jnp.dot(a.astype(BF16), b.astype(BF16), preferred_element_type=F32)


def _bdot_nt(a, b):
    return lax.dot_general(a.astype(BF16), b.astype(BF16), (((1,), (1,)), ((), ())), preferred_element_type=F32)


def _fdot(a, b):
    return jnp.dot(a, b, precision=HIGHEST, preferred_element_type=F32)


def _split3(x):
    hi = x.astype(BF16)
    r = x - hi.astype(F32)
    mid = r.astype(BF16)
    lo = (r - mid.astype(F32)).astype(BF16)
    return hi, mid, lo


def _dot3(a, b):
    a_hi, a_lo, _ = _split3(a)
    b_hi, b_lo, _ = _split3(b)
    dot = functools.partial(jnp.dot, preferred_element_type=F32)
    return dot(a_hi, b_hi) + dot(a_lo, b_hi) + dot(a_hi, b_lo)


def _tri_dot(tri, x):
    n = x.shape[1]
    y = jnp.dot(tri.astype(BF16), jnp.concatenate(_split3(x), axis=1), preferred_element_type=F32)
    return y[:, :n] + y[:, n:2 * n] + y[:, 2 * n:]


def _ada_kernel(c_ref, w_ref, b_ref, o_ref):
    o_ref[0] = _fdot(_silu(c_ref[...]), w_ref[0]) + b_ref[0]


def _ada(cvec, w_ada, b_ada):
    depth, d, n = w_ada.shape
    rows = cvec.shape[0]
    tn = 1536
    return pl.pallas_call(
        _ada_kernel,
        grid=(depth, n // tn),
        in_specs=[pl.BlockSpec((rows, d), lambda l, j: (0, 0)),
                  pl.BlockSpec((1, d, tn), lambda l, j: (l, 0, j)),
                  pl.BlockSpec((1, 1, tn), lambda l, j: (l, 0, j))],
        out_specs=pl.BlockSpec((1, rows, tn), lambda l, j: (l, 0, j)),
        out_shape=jax.ShapeDtypeStruct((depth, rows, n), F32),
        compiler_params=_cparams(("parallel", "parallel")),
        name="ada",
    )(cvec, w_ada, b_ada.reshape(depth, 1, n))


def _in_kernel(x_ref, mod_ref, g_ref, w_ref, o_ref, os_ref):
    h = _norm_mod(x_ref[0], g_ref[...], mod_ref[0, 0:1, :], mod_ref[0, 1:2, :])
    acc = jnp.dot(h.astype(BF16), w_ref[...], preferred_element_type=F32)
    o_ref[0] = acc[:, :P_MAIN_W].astype(o_ref.dtype)
    os_ref[0] = acc[:, P_MAIN_W:]


def _in_proj(x, mod, g, w):
    b, t, d = x.shape
    n = w.shape[1]
    tm = min(t, 512)
    mod_map = (lambda i, j: (i, 0, 0)) if mod.shape[0] == b else (lambda i, j: (0, 0, 0))
    return pl.pallas_call(
        _in_kernel,
        grid=(b, t // tm),
        in_specs=[pl.BlockSpec((1, tm, d), lambda i, j: (i, j, 0)),
                  pl.BlockSpec((1, 6, d), mod_map),
                  pl.BlockSpec((1, d), lambda i, j: (0, 0)),
                  pl.BlockSpec((d, n), lambda i, j: (0, 0))],
        out_specs=[pl.BlockSpec((1, tm, P_MAIN_W), lambda i, j: (i, j, 0)),
                   pl.BlockSpec((1, tm, n - P_MAIN_W), lambda i, j: (i, j, 0))],
        out_shape=[jax.ShapeDtypeStruct((b, t, P_MAIN_W), BF16), jax.ShapeDtypeStruct((b, t, n - P_MAIN_W), F32)],
        compiler_params=_cparams(("parallel", "parallel")),
        name="in_proj",
    )(x, mod, g.reshape(1, d), w)


def _out_kernel(x_ref, mod_ref, a_ref, b_ref, c_ref, d_ref, w_ref, o_ref):
    acc = jnp.dot(a_ref[0], w_ref[0:GROUP_W, :], preferred_element_type=F32)
    acc += jnp.dot(b_ref[0], w_ref[GROUP_W:2 * GROUP_W, :], preferred_element_type=F32)
    acc += jnp.dot(c_ref[0], w_ref[2 * GROUP_W:3 * GROUP_W, :], preferred_element_type=F32)
    acc += jnp.dot(d_ref[0], w_ref[3 * GROUP_W:4 * GROUP_W, :], preferred_element_type=F32)
    o_ref[0] = x_ref[0] + mod_ref[0, 2:3, :] * acc


def _out_proj(x, mod, mixes, w):
    b, t, d = x.shape
    tm = min(t, 512)
    mod_map = (lambda i, j: (i, 0, 0)) if mod.shape[0] == b else (lambda i, j: (0, 0, 0))
    mix_spec = pl.BlockSpec((1, tm, GROUP_W), lambda i, j: (i, j, 0))
    return pl.pallas_call(
        _out_kernel,
        grid=(b, t // tm),
        in_specs=[pl.BlockSpec((1, tm, d), lambda i, j: (i, j, 0)),
                  pl.BlockSpec((1, 6, d), mod_map),
                  mix_spec, mix_spec, mix_spec, mix_spec,
                  pl.BlockSpec((d, d), lambda i, j: (0, 0))],
        out_specs=pl.BlockSpec((1, tm, d), lambda i, j: (i, j, 0)),
        out_shape=jax.ShapeDtypeStruct((b, t, d), F32),
        compiler_params=_cparams(("parallel", "parallel")),
        name="out_proj",
    )(x, mod, *mixes, w)


def _ffn_kernel(x_ref, mod_ref, g_ref, wg_ref, wu_ref, wd_ref, o_ref, h_sc, acc_sc):
    j = pl.program_id(2)

    @pl.when(j == 0)
    def _():
        h = _norm_mod(x_ref[0], g_ref[...], mod_ref[0, 3:4, :], mod_ref[0, 4:5, :])
        h_sc[...] = h.astype(BF16)
        acc_sc[...] = jnp.zeros_like(acc_sc)

    h = h_sc[...]
    gate = jnp.dot(h, wg_ref[...], preferred_element_type=F32)
    up = jnp.dot(h, wu_ref[...], preferred_element_type=F32)
    acc_sc[...] += jnp.dot((_silu(gate) * up).astype(BF16), wd_ref[...], preferred_element_type=F32)

    @pl.when(j == pl.num_programs(2) - 1)
    def _():
        o_ref[0] = x_ref[0] + mod_ref[0, 5:6, :] * acc_sc[...]


def _ffn(x, mod, g, wg, wu, wd):
    b, t, d = x.shape
    f = wg.shape[1]
    tm = min(t, 512)
    tf = f // 2
    mod_map = (lambda i, j, k: (i, 0, 0)) if mod.shape[0] == b else (lambda i, j, k: (0, 0, 0))
    return pl.pallas_call(
        _ffn_kernel,
        grid=(b, t // tm, f // tf),
        in_specs=[pl.BlockSpec((1, tm, d), lambda i, j, k: (i, j, 0)),
                  pl.BlockSpec((1, 6, d), mod_map),
                  pl.BlockSpec((1, d), lambda i, j, k: (0, 0)),
                  pl.BlockSpec((d, tf), lambda i, j, k: (0, k)),
                  pl.BlockSpec((d, tf), lambda i, j, k: (0, k)),
                  pl.BlockSpec((tf, d), lambda i, j, k: (k, 0))],
        out_specs=pl.BlockSpec((1, tm, d), lambda i, j, k: (i, j, 0)),
        out_shape=jax.ShapeDtypeStruct((b, t, d), F32),
        scratch_shapes=[pltpu.VMEM((tm, d), BF16), pltpu.VMEM((tm, d), F32)],
        compiler_params=_cparams(("parallel", "parallel", "arbitrary")),
        name="ffn",
    )(x, mod, g.reshape(1, d), wg, wu, wd)


def _iota(shape, dim):
    return lax.broadcasted_iota(jnp.int32, shape, dim)


def _gla_chunk(q, k, v, la, s_ref, lower, want_out):
    L = q.shape[0]
    r, c = _iota((L, L), 0), _iota((L, L), 1)
    keep = (c <= r) if lower else (c >= r)
    cum = _tri_dot(jnp.where(keep, 1.0, 0.0), la)
    tot_row = jnp.sum(la, axis=0, keepdims=True)
    s_prev = s_ref[...]
    out = None
    if want_out:
        mid = L // 2 - 1 if lower else L // 2
        ref = cum[mid:mid + 1, :]
        qr = q * jnp.exp(cum - ref)
        kc = (k * jnp.exp(ref - cum)).astype(BF16)
        lane_q = _iota((1, GLA_QK_W), 1) // GLA_DK
        lane_v = _iota((1, GROUP_W), 1) // HEAD_DIM
        qs = jnp.concatenate([jnp.where(lane_q == h, qr, 0.0) for h in range(N_HEADS)], axis=0)
        sc = _bdot_nt(qs, kc)
        sc_all = jnp.concatenate([jnp.where(keep, sc[h * L:(h + 1) * L, :], 0.0).astype(BF16)
                                  for h in range(N_HEADS)], axis=1)
        v_all = jnp.concatenate([jnp.where(lane_v == h, v, 0.0).astype(BF16) for h in range(N_HEADS)], axis=0)
        out = _bdot_nt(q * jnp.exp(cum), s_prev) + jnp.dot(sc_all, v_all, preferred_element_type=F32)
    ke = k * jnp.exp(tot_row - cum)
    upd = jnp.dot(v.T.astype(BF16), ke.astype(BF16), preferred_element_type=F32)
    bd = (_iota((GROUP_W, GLA_QK_W), 0) // HEAD_DIM) == (_iota((GROUP_W, GLA_QK_W), 1) // GLA_DK)
    s_ref[...] = jnp.exp(tot_row) * s_prev + jnp.where(bd, upd, 0.0)
    return out


def _gla_kernel(qk_ref, v_ref, g_ref, sm_ref, qkc_ref, vc_ref, gc_ref, smc_ref, wlr_ref, blr_ref, ng_ref,
                *rest, ctx_out):
    if ctx_out:
        o_ref, oc_ref, la_sc, lac_sc, of_sc, ob_sc, ofc_sc, obc_sc, sf_sc, sb_sc = rest
    else:
        o_ref, la_sc, lac_sc, of_sc, ob_sc, sf_sc, sb_sc = rest
        oc_ref = ofc_sc = obc_sc = None
    t = qk_ref.shape[1]
    tc = qkc_ref.shape[1]
    qscale = GLA_DK ** -0.5

    def log_decay(sm):
        z = _dot3(sm, wlr_ref[...]) + blr_ref[...]
        return _log_sigmoid(z) * (1.0 / GLA_NORMALIZER)

    lac_sc[...] = log_decay(smc_ref[0])
    rows_blk = 256

    def la_body(i, carry):
        r0 = pl.multiple_of(i * rows_blk, rows_blk)
        la_sc[pl.ds(r0, rows_blk), :] = log_decay(sm_ref[0, pl.ds(r0, rows_blk), :])
        return carry

    lax.fori_loop(0, t // rows_blk, la_body, 0)
    sf_sc[...] = jnp.zeros_like(sf_sc)
    sb_sc[...] = jnp.zeros_like(sb_sc)

    def run(n_rows, qk, v, la, of, ob, want_out):
        n = n_rows // GLA_BLOCK

        def body(i, carry):
            rf = pl.multiple_of(i * GLA_BLOCK, GLA_BLOCK)
            rb = pl.multiple_of((n - 1 - i) * GLA_BLOCK, GLA_BLOCK)
            for r0, s_ref, lower, dst, lane0 in ((rf, sf_sc, True, of, 0), (rb, sb_sc, False, ob, GLA_QK_W)):
                qkb = qk[0, pl.ds(r0, GLA_BLOCK), :].astype(F32)
                out = _gla_chunk(qkb[:, :GLA_QK_W] * qscale, qkb[:, GLA_QK_W:], v[0, pl.ds(r0, GLA_BLOCK), :].astype(F32),
                                 la[pl.ds(r0, GLA_BLOCK), lane0:lane0 + GLA_QK_W], s_ref, lower, want_out)
                if want_out:
                    dst[pl.ds(r0, GLA_BLOCK), :] = out
            return carry

        lax.fori_loop(0, n, body, 0)

    run(tc, qkc_ref, vc_ref, lac_sc, ofc_sc, obc_sc, ctx_out)
    run(t, qk_ref, v_ref, la_sc, of_sc, ob_sc, True)

    grp = jnp.where((_iota((GROUP_W, GROUP_W), 0) // HEAD_DIM) == (_iota((GROUP_W, GROUP_W), 1) // HEAD_DIM),
                    1.0 / HEAD_DIM, 0.0).astype(BF16)

    def finish(n_rows, of, ob, g, dst):
        blk = min(n_rows, 256)

        def body(i, carry):
            r0 = pl.multiple_of(i * blk, blk)
            o = of[pl.ds(r0, blk), :] + ob[pl.ds(r0, blk), :]
            sq_hi, sq_lo, _ = _split3(o * o)
            ms = (jnp.dot(sq_hi, grp, preferred_element_type=F32) + jnp.dot(sq_lo, grp, preferred_element_type=F32))
            res = o * lax.rsqrt(ms + EPS) * ng_ref[...] * _silu(g[0, pl.ds(r0, blk), :].astype(F32))
            dst[0, pl.ds(r0, blk), :] = res.astype(dst.dtype)
            return carry

        lax.fori_loop(0, n_rows // blk, body, 0)

    finish(t, of_sc, ob_sc, g_ref, o_ref)
    if ctx_out:
        finish(tc, ofc_sc, obc_sc, gc_ref, oc_ref)


def _gla(p, ps, pc, pcs, wlr, blr, ng, ctx_out):
    b, t, _ = p.shape
    tc = pc.shape[1]

    def col(width, idx, rows):
        return pl.BlockSpec((1, rows, width), lambda i: (i, 0, idx))

    full = lambda shape: pl.BlockSpec(shape, lambda i: tuple(0 for _ in shape))
    in_specs = [col(GROUP_W, P_GLA_QK, t), col(GROUP_W, P_GLA_V, t), col(GROUP_W, P_GLA_G, t), col(LANES, 0, t),
                col(GROUP_W, P_GLA_QK, tc), col(GROUP_W, P_GLA_V, tc), col(GROUP_W, P_GLA_G, tc),
                col(LANES, 0, tc),
                full(wlr.shape), full(blr.shape), full(ng.shape)]
    out_specs = [pl.BlockSpec((1, t, GROUP_W), lambda i: (i, 0, 0))]
    out_shape = [jax.ShapeDtypeStruct((b, t, GROUP_W), BF16)]
    scratch = [pltpu.VMEM((t, 2 * GLA_QK_W), F32), pltpu.VMEM((tc, 2 * GLA_QK_W), F32),
               pltpu.VMEM((t, GROUP_W), F32), pltpu.VMEM((t, GROUP_W), F32)]
    if ctx_out:
        out_specs.append(pl.BlockSpec((1, tc, GROUP_W), lambda i: (i, 0, 0)))
        out_shape.append(jax.ShapeDtypeStruct((b, tc, GROUP_W), BF16))
        scratch += [pltpu.VMEM((tc, GROUP_W), F32), pltpu.VMEM((tc, GROUP_W), F32)]
    scratch += [pltpu.VMEM((GROUP_W, GLA_QK_W), F32), pltpu.VMEM((GROUP_W, GLA_QK_W), F32)]
    res = pl.pallas_call(
        functools.partial(_gla_kernel, ctx_out=ctx_out),
        grid=(b,),
        in_specs=in_specs, out_specs=out_specs, out_shape=out_shape, scratch_shapes=scratch,
        compiler_params=_cparams(("parallel",)),
        name="gla",
    )(p, p, p, ps, pc, pc, pc, pcs, wlr, blr, ng)
    return (res[0], res[1]) if ctx_out else (res[0], None)


ML_BLOCK = 128
N_PAIRS = N_HEADS // 2


def _gate_lane(d, h):
    return SMALL_GATE_LANE + 8 * d + h


def _mlstm_selectors():
    sel_head = np.zeros((LANES, 2 * N_HEADS * LANES), np.float32)
    sel_pair = np.zeros((LANES, 2 * N_PAIRS * LANES), np.float32)
    for d in range(2):
        for h in range(N_HEADS):
            sel_head[_gate_lane(d, h), (d * N_HEADS + h) * LANES:(d * N_HEADS + h + 1) * LANES] = 1.0
            p, hh = divmod(h, 2)
            lo = (d * N_PAIRS + p) * LANES + hh * HEAD_DIM
            sel_pair[_gate_lane(d, h), lo:lo + HEAD_DIM] = 1.0
    return sel_head, sel_pair


def _mlstm_block(blk_f, blk_b, cn_ref, m_prev, selh_ref, selp_ref, want_out):
    L = ML_BLOCK
    lane = _iota((1, LANES), 1)
    is_b = (lane >= _gate_lane(1, 0)) & (lane < _gate_lane(1, 0) + N_HEADS)
    valid = ((lane >= _gate_lane(0, 0)) & (lane < _gate_lane(0, 0) + N_HEADS)) | is_b
    r, c = _iota((L, L), 0), _iota((L, L), 1)
    keep_f, keep_b = c <= r, c >= r
    lf_f = pltpu.roll(_log_sigmoid(blk_f[2]), LANES - 4, 1)
    lf_b = pltpu.roll(_log_sigmoid(blk_b[2]), LANES - 4, 1)
    cum_f = _tri_dot(jnp.where(keep_f, 1.0, 0.0), lf_f)
    cum_b = _tri_dot(jnp.where(keep_b, 1.0, 0.0), lf_b)
    cum = jnp.where(valid, jnp.where(is_b, cum_b, cum_f), 0.0)
    ig = jnp.where(valid, jnp.where(is_b, blk_b[2], blk_f[2]), 0.0)
    tot = jnp.where(valid, jnp.where(is_b, cum_b[0:1, :], cum_f[L - 1:L, :]), 0.0)
    b = ig - cum
    row = _iota((L, 1), 0)
    pm = b
    k = 1
    while k < L:
        from_prev = jnp.where(row >= k, pltpu.roll(pm, k, 0), -jnp.inf)
        from_next = jnp.where(row < L - k, pltpu.roll(pm, L - k, 0), -jnp.inf)
        pm = jnp.maximum(pm, jnp.where(is_b, from_next, from_prev))
        k *= 2
    mm = jnp.maximum(m_prev, pm)
    w_end = tot + b
    m_chunk = jnp.max(w_end, axis=0, keepdims=True)
    m_new = jnp.maximum(tot + m_prev, m_chunk)
    a_row = jnp.exp(tot + m_prev - m_new)
    g_row = jnp.exp(m_chunk - m_new)
    cols = [jnp.exp(w_end - m_chunk)]
    if want_out:
        cols += [jnp.exp(m_prev - mm), jnp.exp(-(cum + mm))]
        mm_hi, mm_lo, _ = _split3(mm)
        mm_t = jnp.dot(jnp.concatenate([mm_hi, mm_lo], axis=1), jnp.concatenate([selh_ref[...]] * 2, axis=0),
                       preferred_element_type=F32)
        b_rows = b.T
    pair_t = _bdot(jnp.concatenate(cols, axis=0), selp_ref[...])
    lo_half = lane < HEAD_DIM
    ones_t = jnp.ones((L, LANES), F32)
    zeros_2t = jnp.zeros((L, 2 * LANES), BF16)
    lane_h = _iota((1, GROUP_W), 1) // HEAD_DIM
    bd = (_iota((LANES, 2 * LANES), 0) // HEAD_DIM) == ((_iota((LANES, 2 * LANES), 1) % LANES) // HEAD_DIM)
    outs = {}
    for d, (blk, keep) in enumerate(((blk_f, keep_f), (blk_b, keep_b))):
        qk, v, _ = blk
        q_all, k_all = qk[:, :GROUP_W], qk[:, GROUP_W:]
        cn = [cn_ref[d * N_PAIRS + p] for p in range(N_PAIRS)]
        tcol = [(d * N_PAIRS + p) * LANES for p in range(N_PAIRS)]
        if want_out:
            qs = jnp.concatenate([jnp.where(lane_h == h, q_all, 0.0) for h in range(N_HEADS)], axis=0)
            sc = _bdot_nt(qs, k_all)
            s_parts, va_rows = [], []
            for h in range(N_HEADS):
                p, hh = divmod(h, 2)
                mine = lo_half if hh == 0 else jnp.logical_not(lo_half)
                gl = _gate_lane(d, h)
                hcol = (d * N_HEADS + h) * LANES
                e = jnp.where(keep, jnp.exp(b_rows[gl:gl + 1, :] - mm_t[:, hcol:hcol + LANES]), 0.0)
                s_parts.append((e * sc[h * L:(h + 1) * L, :]).astype(BF16))
                vp = v[:, p * LANES:(p + 1) * LANES]
                va = jnp.concatenate([jnp.where(mine, vp, 0.0), jnp.where(mine, ones_t, 0.0)], axis=1).astype(BF16)
                va_rows.append(jnp.concatenate([va, zeros_2t] if p == 0 else [zeros_2t, va], axis=1))
            nd = jnp.dot(jnp.concatenate(s_parts, axis=1), jnp.concatenate(va_rows, axis=0),
                         preferred_element_type=F32)
            cn_bd = jnp.concatenate([jnp.concatenate([cn[0].astype(BF16), zeros_2t], axis=1),
                                     jnp.concatenate([zeros_2t, cn[1].astype(BF16)], axis=1)], axis=0)
            w_int = [pair_t[L:2 * L, tc0:tc0 + LANES] for tc0 in tcol]
            nd = nd + (jnp.concatenate([w_int[0], w_int[0], w_int[1], w_int[1]], axis=1)
                       * jnp.dot(q_all.astype(BF16), cn_bd, preferred_element_type=F32))
            for p in range(N_PAIRS):
                c0 = 2 * p * LANES
                outs[(d, p)] = nd[:, c0:c0 + LANES] / jnp.maximum(jnp.abs(nd[:, c0 + LANES:c0 + 2 * LANES]),
                                                                  pair_t[2 * L:3 * L, tcol[p]:tcol[p] + LANES])
        w_k = jnp.concatenate([pair_t[0:L, tc0:tc0 + LANES] for tc0 in tcol], axis=1)
        kw_t = (k_all * w_k).T
        upd_all = _bdot(kw_t, jnp.concatenate([v, ones_t], axis=1))
        for p in range(N_PAIRS):
            rows = slice(p * LANES, (p + 1) * LANES)
            upd = jnp.concatenate([upd_all[rows, p * LANES:(p + 1) * LANES], upd_all[rows, GROUP_W:]], axis=1)
            ga, gb = _gate_lane(d, 2 * p), _gate_lane(d, 2 * p + 1)
            half = (HEAD_DIM, 2 * LANES)
            a_t = jnp.concatenate([jnp.broadcast_to(a_row[:, ga:ga + 1], half),
                                   jnp.broadcast_to(a_row[:, gb:gb + 1], half)], axis=0)
            g_t = jnp.concatenate([jnp.broadcast_to(g_row[:, ga:ga + 1], half),
                                   jnp.broadcast_to(g_row[:, gb:gb + 1], half)], axis=0)
            cn_ref[d * N_PAIRS + p] = a_t * cn[p] + g_t * jnp.where(bd, upd, 0.0)
    return (outs if want_out else None), m_new


def _mlstm_kernel(qk_ref, v_ref, og_ref, sm_ref, qkc_ref, vc_ref, ogc_ref, smc_ref, cw_ref, gb_ref,
                  selh_ref, selp_ref, *rest, ctx_out):
    if ctx_out:
        o_ref, oc_ref, qc_sc, qcc_sc, hf_sc, hb_sc, hfc_sc, hbc_sc, cn_sc = rest
    else:
        o_ref, qc_sc, qcc_sc, hf_sc, hb_sc, cn_sc = rest
        oc_ref = hfc_sc = hbc_sc = None
    t = qk_ref.shape[1]
    tc = qkc_ref.shape[1]
    w2 = 2 * GROUP_W
    kscale = HEAD_DIM ** -0.5
    n_rows_grid = t // GRID_W

    def tap(dr, dc):
        return cw_ref[dr * 3 + dc:dr * 3 + dc + 1, :]

    def shifted(blk, dc):
        n = blk.shape[0]
        if dc == 1:
            return blk
        ridx = _iota((n, 1), 0)
        if dc == 0:
            return jnp.where(ridx == 0, 0.0, pltpu.roll(blk, 1, 0))
        return jnp.where(ridx == n - 1, 0.0, pltpu.roll(blk, n - 1, 0))

    def finish_qk(acc):
        a = _silu(acc)
        lane = _iota((1, w2), 1)
        return jnp.where(lane >= GROUP_W, a * kscale, a)

    def conv_body(i, carry):
        r0 = pl.multiple_of(i * GRID_W, GRID_W)
        r_up = pl.multiple_of(jnp.maximum(i - 1, 0) * GRID_W, GRID_W)
        r_dn = pl.multiple_of(jnp.minimum(i + 1, n_rows_grid - 1) * GRID_W, GRID_W)
        up = qk_ref[0, pl.ds(r_up, GRID_W), :].astype(F32) * jnp.where(i > 0, 1.0, 0.0)
        mid = qk_ref[0, pl.ds(r0, GRID_W), :].astype(F32)
        dn = qk_ref[0, pl.ds(r_dn, GRID_W), :].astype(F32) * jnp.where(i < n_rows_grid - 1, 1.0, 0.0)
        acc = jnp.zeros((GRID_W, w2), F32)
        for dr, blk in enumerate((up, mid, dn)):
            for dc in range(3):
                acc = acc + shifted(blk, dc) * tap(dr, dc)
        qc_sc[pl.ds(r0, GRID_W), :] = finish_qk(acc)
        return carry

    lax.fori_loop(0, n_rows_grid, conv_body, 0)
    xc = qkc_ref[0].astype(F32)
    acc = jnp.zeros((tc, w2), F32)
    for dc in range(3):
        acc = acc + shifted(xc, dc) * tap(1, dc)
    qcc_sc[...] = finish_qk(acc)

    cn_sc[...] = jnp.zeros_like(cn_sc)

    def run(n_rows, qc, v, sm, hf, hb, want_out, m0):
        n = n_rows // ML_BLOCK

        def body(i, m):
            rf = pl.multiple_of(i * ML_BLOCK, ML_BLOCK)
            rb = pl.multiple_of((n - 1 - i) * ML_BLOCK, ML_BLOCK)
            blks = [(qc[pl.ds(r0, ML_BLOCK), :], v[0, pl.ds(r0, ML_BLOCK), :].astype(F32),
                     sm[0, pl.ds(r0, ML_BLOCK), :] + gb_ref[...]) for r0 in (rf, rb)]
            outs, m_new = _mlstm_block(blks[0], blks[1], cn_sc, m, selh_ref, selp_ref, want_out)
            if want_out:
                for p in range(N_PAIRS):
                    hf[pl.ds(rf, ML_BLOCK), p * LANES:(p + 1) * LANES] = outs[(0, p)]
                    hb[pl.ds(rb, ML_BLOCK), p * LANES:(p + 1) * LANES] = outs[(1, p)]
            return m_new

        return lax.fori_loop(0, n, body, m0)

    m1 = run(tc, qcc_sc, vc_ref, smc_ref, hfc_sc, hbc_sc, ctx_out, jnp.zeros((1, LANES), F32))
    run(t, qc_sc, v_ref, sm_ref, hf_sc, hb_sc, True, m1)

    def finish(n_rows, hf, hb, og, dst):
        blk = min(n_rows, 256)

        def body(i, carry):
            r0 = pl.multiple_of(i * blk, blk)
            res = _sigmoid(og[0, pl.ds(r0, blk), :].astype(F32)) * (hf[pl.ds(r0, blk), :] + hb[pl.ds(r0, blk), :])
            dst[0, pl.ds(r0, blk), :] = res.astype(dst.dtype)
            return carry

        lax.fori_loop(0, n_rows // blk, body, 0)

    finish(t, hf_sc, hb_sc, og_ref, o_ref)
    if ctx_out:
        finish(tc, hfc_sc, hbc_sc, ogc_ref, oc_ref)


def _mlstm(p, ps, pc, pcs, conv_w, gate_b, ctx_out):
    b, t, _ = p.shape
    tc = pc.shape[1]
    sel_head, sel_pair = (jnp.asarray(a).astype(BF16) for a in _mlstm_selectors())

    def col(width, idx, rows):
        return pl.BlockSpec((1, rows, width), lambda i: (i, 0, idx))

    full = lambda shape: pl.BlockSpec(shape, lambda i: tuple(0 for _ in shape))
    in_specs = [col(2 * GROUP_W, P_ML_QK, t), col(GROUP_W, P_ML_V, t), col(GROUP_W, P_ML_OG, t),
                col(LANES, 0, t),
                col(2 * GROUP_W, P_ML_QK, tc), col(GROUP_W, P_ML_V, tc), col(GROUP_W, P_ML_OG, tc),
                col(LANES, 0, tc),
                full(conv_w.shape), full(gate_b.shape), full(sel_head.shape), full(sel_pair.shape)]
    out_specs = [pl.BlockSpec((1, t, GROUP_W), lambda i: (i, 0, 0))]
    out_shape = [jax.ShapeDtypeStruct((b, t, GROUP_W), BF16)]
    scratch = [pltpu.VMEM((t, 2 * GROUP_W), F32), pltpu.VMEM((tc, 2 * GROUP_W), F32),
               pltpu.VMEM((t, GROUP_W), F32), pltpu.VMEM((t, GROUP_W), F32)]
    if ctx_out:
        out_specs.append(pl.BlockSpec((1, tc, GROUP_W), lambda i: (i, 0, 0)))
        out_shape.append(jax.ShapeDtypeStruct((b, tc, GROUP_W), BF16))
        scratch += [pltpu.VMEM((tc, GROUP_W), F32), pltpu.VMEM((tc, GROUP_W), F32)]
    scratch += [pltpu.VMEM((2 * N_PAIRS, LANES, 2 * LANES), F32)]
    res = pl.pallas_call(
        functools.partial(_mlstm_kernel, ctx_out=ctx_out),
        grid=(b,),
        in_specs=in_specs, out_specs=out_specs, out_shape=out_shape, scratch_shapes=scratch,
        compiler_params=_cparams(("parallel",)),
        name="mlstm",
    )(p, p, p, ps, pc, pc, pc, pcs, conv_w, gate_b, sel_head, sel_pair)
    return (res[0], res[1]) if ctx_out else (res[0], None)


def _sgu_kernel(u_ref, v_ref, f_ref, ws_ref, bias_ref, dft_ref, o_ref, z_ref):
    t = u_ref.shape[1]
    lane_g = _iota((1, GROUP_W), 1) // HEAD_DIM

    def body(i, carry):
        r0 = pl.multiple_of(i * SGU_CHUNK, SGU_CHUNK)
        v = v_ref[0, pl.ds(r0, SGU_CHUNK), :].astype(F32)
        mu = jnp.mean(v, axis=-1, keepdims=True)
        vc = v - mu
        vn = vc * lax.rsqrt(jnp.mean(vc * vc, axis=-1, keepdims=True) + EPS)
        mixed = bias_ref[...]
        for g in range(N_HEADS):
            mixed = mixed + _bdot(ws_ref[g], jnp.where(lane_g == g, vn, 0.0))
        o_ref[0, pl.ds(r0, SGU_CHUNK), :] = (u_ref[0, pl.ds(r0, SGU_CHUNK), :].astype(F32) * mixed).astype(o_ref.dtype)
        z = _bdot(f_ref[0, pl.ds(r0, SGU_CHUNK), :], dft_ref[...])
        z_ref[0, 0, pl.ds(r0, SGU_CHUNK), :] = z[:, :GROUP_W].astype(z_ref.dtype)
        z_ref[1, 0, pl.ds(r0, SGU_CHUNK), :] = z[:, GROUP_W:].astype(z_ref.dtype)
        return carry

    lax.fori_loop(0, t // SGU_CHUNK, body, 0)


def _sgu_fourier1(p, ws, bias, dft_c):
    b, t, _ = p.shape

    def col(idx):
        return pl.BlockSpec((1, t, GROUP_W), lambda i: (i, 0, idx))

    full = lambda shape: pl.BlockSpec(shape, lambda i: tuple(0 for _ in shape))
    return pl.pallas_call(
        _sgu_kernel,
        grid=(b,),
        in_specs=[col(P_SGU_U), col(P_SGU_V), col(P_FOUR), full(ws.shape), full(bias.shape), full(dft_c.shape)],
        out_specs=[pl.BlockSpec((1, t, GROUP_W), lambda i: (i, 0, 0)),
                   pl.BlockSpec((2, 1, t, GROUP_W), lambda i: (0, i, 0, 0))],
        out_shape=[jax.ShapeDtypeStruct((b, t, GROUP_W), BF16), jax.ShapeDtypeStruct((2, b, t, GROUP_W), BF16)],
        compiler_params=_cparams(("parallel",)),
        name="sgu_fourier1",
    )(p, p, p, ws, bias, dft_c)


def _fourier2_kernel(ct_ref, st_ref, z_ref, o_ref, *, scale):
    for i in range(z_ref.shape[1]):
        acc = jnp.dot(ct_ref[...], z_ref[0, i], preferred_element_type=F32)
        acc -= jnp.dot(st_ref[...], z_ref[1, i], preferred_element_type=F32)
        o_ref[i] = (acc * scale).astype(o_ref.dtype)


def _fourier2(z, cos_t, sin_t):
    _, b, t, w = z.shape
    tm = min(t, 256)
    return pl.pallas_call(
        functools.partial(_fourier2_kernel, scale=float((t * HEAD_DIM) ** -0.5)),
        grid=(t // tm,),
        in_specs=[pl.BlockSpec((tm, t), lambda i: (i, 0)),
                  pl.BlockSpec((tm, t), lambda i: (i, 0)),
                  pl.BlockSpec((2, b, t, w), lambda i: (0, 0, 0, 0), pipeline_mode=pl.Buffered(1))],
        out_specs=pl.BlockSpec((b, tm, w), lambda i: (0, i, 0)),
        out_shape=jax.ShapeDtypeStruct((b, t, w), BF16),
        compiler_params=_cparams(("parallel",)),
        name="fourier2",
    )(cos_t, sin_t, z)


def _dft_tables(n):
    k = np.arange(n, dtype=np.int64)
    ang = 2.0 * np.pi * ((k[:, None] * k[None, :]) % n).astype(np.float64) / n
    return np.cos(ang).astype(np.float32), np.sin(ang).astype(np.float32)


def _channel_dft():
    c, s = _dft_tables(HEAD_DIM)
    eye = np.eye(N_HEADS, dtype=np.float32)
    return np.concatenate([np.kron(eye, c), np.kron(eye, s)], axis=1)


def _route_kernel(x_ref, mod_ref, g_ref, wr_ref, h_ref, mi_ref, mf_ref, cnt_ref, cnt_sc):
    first = (pl.program_id(0) == 0) & (pl.program_id(1) == 0)

    @pl.when(first)
    def _():
        cnt_sc[...] = jnp.zeros_like(cnt_sc)

    tm = x_ref.shape[1]
    h = _norm_mod(x_ref[0], g_ref[...], mod_ref[0, 3:4, :], mod_ref[0, 4:5, :])
    for j in range(D_MODEL // LANES):
        h_ref[pl.ds(j, tm, stride=SUBLANES), :] = h[:, j * LANES:(j + 1) * LANES]
    logits = _fdot(h, wr_ref[...])
    lane = _iota((tm, LANES), 1).astype(F32)
    lg = jnp.where(lane < N_EXPERTS, logits, -jnp.inf)
    m1 = jnp.max(lg, axis=1, keepdims=True)
    i1 = jnp.min(jnp.where(lg == m1, lane, float(LANES)), axis=1, keepdims=True)
    lg2 = jnp.where(lane == i1, -jnp.inf, lg)
    m2 = jnp.max(lg2, axis=1, keepdims=True)
    i2 = jnp.min(jnp.where(lg2 == m2, lane, float(LANES)), axis=1, keepdims=True)
    e = jnp.exp(m2 - m1)
    g0 = 1.0 / (1.0 + e)
    g1 = e / (1.0 + e)
    onehot = jnp.where((lane == i1) | (lane == i2), 1.0, 0.0)
    strict = jnp.where(_iota((tm, tm), 1) < _iota((tm, tm), 0), 1.0, 0.0)
    before = _bdot(strict, onehot) + cnt_sc[...]
    r0 = jnp.sum(jnp.where(lane == i1, before, 0.0), axis=1, keepdims=True)
    r1 = jnp.sum(jnp.where(lane == i2, before, 0.0), axis=1, keepdims=True)
    cnt_sc[...] += jnp.sum(onehot, axis=0, keepdims=True)
    cnt_ref[...] = cnt_sc[...]
    l8 = _iota((tm, SUBLANES), 1)
    mi = jnp.where(l8 == 0, i1, jnp.where(l8 == 1, i2, jnp.where(l8 == 2, r0, r1)))
    mi_ref[...] = mi.astype(jnp.int32)
    mf_ref[...] = jnp.where(l8 == 0, g0, g1)


def _route(x, mod, g, w_router):
    b, t, d = x.shape
    m = b * t
    tm = 512
    nt = t // tm
    wr = jnp.zeros((d, LANES), F32).at[:, :N_EXPERTS].set(w_router)
    return pl.pallas_call(
        _route_kernel,
        grid=(b, nt),
        in_specs=[pl.BlockSpec((1, tm, d), lambda i, j: (i, j, 0)),
                  pl.BlockSpec((1, 6, d), lambda i, j: (i, 0, 0)),
                  pl.BlockSpec((1, d), lambda i, j: (0, 0)),
                  pl.BlockSpec((d, LANES), lambda i, j: (0, 0))],
        out_specs=[pl.BlockSpec((tm * SUBLANES, LANES), lambda i, j: (i * nt + j, 0)),
                   pl.BlockSpec((tm, SUBLANES), lambda i, j: (i * nt + j, 0)),
                   pl.BlockSpec((tm, SUBLANES), lambda i, j: (i * nt + j, 0)),
                   pl.BlockSpec((1, LANES), lambda i, j: (0, 0))],
        out_shape=[jax.ShapeDtypeStruct((m * SUBLANES, LANES), F32),
                   jax.ShapeDtypeStruct((m, SUBLANES), jnp.int32),
                   jax.ShapeDtypeStruct((m, SUBLANES), F32),
                   jax.ShapeDtypeStruct((1, LANES), F32)],
        scratch_shapes=[pltpu.VMEM((1, LANES), F32)],
        compiler_params=_cparams(("arbitrary", "arbitrary")),
        name="route",
    )(x, mod, g.reshape(1, d), wr)


def _moe_kernel(texp_ref, trows_ref, pos0_ref, pos1_ref, h_hbm, wg_ref, wu_ref, wd_ref, y_hbm,
                order_sm, xbuf, x_sc, acc_sc, ybuf, gsem, ssem):
    i = pl.program_id(0)
    j = pl.program_id(1)
    nf = pl.num_programs(1)
    tm = MOE_TM
    n_tok = pos0_ref.shape[0]
    rows = trows_ref[i]

    nt = pl.num_programs(0)
    buf_rows = tm * SUBLANES
    dump = TOP_K * n_tok
    slot = lax.rem(i, 2)
    prev_rows = trows_ref[jnp.maximum(i - 1, 0)]
    next_rows = trows_ref[jnp.minimum(i + 1, nt - 1)]
    n_full, n_rest = divmod(tm, MOE_COPY_ROWS)

    def gather_copy(tile, n_real, r, dst_slot):
        idx = jnp.where(n_real > 0, tile * tm + jnp.minimum(r, n_real - 1), 0)
        tok = jnp.right_shift(order_sm[idx], 1)
        off = pl.multiple_of(dst_slot * buf_rows + r * SUBLANES, SUBLANES)
        return pltpu.make_async_copy(h_hbm.at[tok], xbuf.at[pl.ds(off, SUBLANES), :], gsem)

    def scatter_copy(tile, n_real, r, src_slot):
        idx = jnp.where(n_real > 0, tile * tm + jnp.minimum(r, n_real - 1), 0)
        dst = jnp.where(r < n_real, order_sm[idx], dump + r)
        off = pl.multiple_of(src_slot * buf_rows + r * SUBLANES, SUBLANES)
        return pltpu.make_async_copy(ybuf.at[pl.ds(off, SUBLANES), :], y_hbm.at[dst], ssem)

    def wait_all(buf, sem):
        pltpu.make_async_copy(buf.at[pl.ds(0, buf_rows), :], buf.at[pl.ds(buf_rows, buf_rows), :], sem).wait()

    @pl.when((i == 0) & (j == 0))
    def _():
        def inv_body(t, carry):
            order_sm[pos0_ref[t]] = 2 * t
            order_sm[pos1_ref[t]] = 2 * t + 1
            return carry

        lax.fori_loop(0, n_tok, inv_body, 0, unroll=8)
        ybuf[...] = jnp.zeros_like(ybuf)

        def start(r, carry):
            gather_copy(0, rows, r, 0).start()
            return carry

        lax.fori_loop(0, tm, start, 0)

    @pl.when((j == 0) & (rows > 0))
    def _():
        wait_all(xbuf, gsem)
        base = pl.multiple_of(slot * buf_rows, SUBLANES)
        for c in range(D_MODEL // LANES):
            x_sc[:, c * LANES:(c + 1) * LANES] = xbuf[pl.ds(base + c, tm, stride=SUBLANES), :].astype(BF16)
        acc_sc[...] = jnp.zeros_like(acc_sc)

    def step(n_copies):
        r0 = j * MOE_COPY_ROWS
        for q in range(n_copies):
            gather_copy(i + 1, next_rows, r0 + q, 1 - slot).start()
            scatter_copy(i - 1, jnp.where(i > 0, prev_rows, 0), r0 + q, 1 - slot).start()
        x = x_sc[...]
        gate = jnp.dot(x, wg_ref[0].astype(BF16), preferred_element_type=F32)
        up = jnp.dot(x, wu_ref[0].astype(BF16), preferred_element_type=F32)
        acc_sc[...] += jnp.dot((_silu(gate) * up).astype(BF16), wd_ref[0].astype(BF16),
                               preferred_element_type=F32)

    pl.when((rows > 0) & (j < n_full))(functools.partial(step, MOE_COPY_ROWS))
    pl.when((rows > 0) & (j == n_full))(functools.partial(step, n_rest))
    pl.when((rows > 0) & (j > n_full))(functools.partial(step, 0))

    @pl.when((j == nf - 1) & (rows > 0))
    def _():
        wait_all(ybuf, ssem)
        base = pl.multiple_of(slot * buf_rows, SUBLANES)
        for c in range(D_MODEL // LANES):
            ybuf[pl.ds(base + c, tm, stride=SUBLANES), :] = acc_sc[:, c * LANES:(c + 1) * LANES]

    @pl.when((j == 0) & (rows == 0) & (i > 0) & (prev_rows > 0))
    def _():
        wait_all(xbuf, gsem)

        def start(r, carry):
            scatter_copy(i - 1, prev_rows, r, 1 - slot).start()
            return carry

        lax.fori_loop(0, tm, start, 0)
        wait_all(ybuf, ssem)


def _moe(h_tiles, pos0, pos1, tile_expert, tile_rows, wg, wu, wd):
    m = h_tiles.shape[0]
    n_exp, d, f = wg.shape
    nt = tile_expert.shape[0]
    tm, tf = MOE_TM, MOE_TF
    nf = f // tf
    assert nf * MOE_COPY_ROWS >= tm, "each tile needs enough column steps to issue its neighbours' row copies"
    assert nt * tm > TOP_K * m + n_exp * (tm - 1), "the last tile must be unused: it drains the copy pipeline"

    def col_tile(i, j, tr):
        return jnp.where(tr[i] > 0, j, nf - 1)

    grid_spec = pltpu.PrefetchScalarGridSpec(
        num_scalar_prefetch=4,
        grid=(nt, nf),
        in_specs=[pl.BlockSpec(memory_space=pl.ANY),
                  pl.BlockSpec((1, d, tf), lambda i, j, te, tr, p0, p1: (te[i], 0, col_tile(i, j, tr))),
                  pl.BlockSpec((1, d, tf), lambda i, j, te, tr, p0, p1: (te[i], 0, col_tile(i, j, tr))),
                  pl.BlockSpec((1, tf, d), lambda i, j, te, tr, p0, p1: (te[i], col_tile(i, j, tr), 0))],
        out_specs=pl.BlockSpec(memory_space=pl.ANY),
        scratch_shapes=[pltpu.SMEM((nt * tm,), jnp.int32),
                        pltpu.VMEM((2 * tm * SUBLANES, LANES), F32),
                        pltpu.VMEM((tm, d), BF16),
                        pltpu.VMEM((tm, d), F32),
                        pltpu.VMEM((2 * tm * SUBLANES, LANES), F32),
                        pltpu.SemaphoreType.DMA(()),
                        pltpu.SemaphoreType.DMA(())],
    )
    return pl.pallas_call(
        _moe_kernel,
        grid_spec=grid_spec,
        out_shape=jax.ShapeDtypeStruct((TOP_K * m + tm, SUBLANES, LANES), F32),
        compiler_params=_cparams(("arbitrary", "arbitrary")),
        name="moe",
    )(tile_expert, tile_rows, pos0, pos1, h_tiles, wg, wu, wd)


def _combine_kernel(x_ref, mod_ref, y_ref, mf_ref, fg_ref, o_ref):
    tm = x_ref.shape[1]
    stride = TOP_K * SUBLANES
    g0 = mf_ref[:, 0:1]
    g1 = mf_ref[:, 1:2]
    parts = []
    for c in range(D_MODEL // LANES):
        y0 = y_ref[pl.ds(c, tm, stride=stride), :]
        y1 = y_ref[pl.ds(SUBLANES + c, tm, stride=stride), :]
        parts.append(g0 * y0 + g1 * y1)
    moe = jnp.concatenate(parts, axis=1)
    x = x_ref[0] + mod_ref[0, 5:6, :] * moe
    ms = jnp.mean(x * x, axis=-1, keepdims=True)
    o_ref[0] = x * lax.rsqrt(ms + EPS) * fg_ref[...]


def _combine(x, mod, y, mf, final_g):
    b, t, d = x.shape
    tm = 256
    nt = t // tm
    rows = tm * TOP_K * SUBLANES
    return pl.pallas_call(
        _combine_kernel,
        grid=(b, nt),
        in_specs=[pl.BlockSpec((1, tm, d), lambda i, j: (i, j, 0)),
                  pl.BlockSpec((1, 6, d), lambda i, j: (i, 0, 0)),
                  pl.BlockSpec((rows, LANES), lambda i, j: (i * nt + j, 0)),
                  pl.BlockSpec((tm, SUBLANES), lambda i, j: (i * nt + j, 0)),
                  pl.BlockSpec((1, d), lambda i, j: (0, 0))],
        out_specs=pl.BlockSpec((1, tm, d), lambda i, j: (i, j, 0)),
        out_shape=jax.ShapeDtypeStruct((b, t, d), F32),
        compiler_params=_cparams(("parallel", "parallel")),
        name="combine",
    )(x, mod, y, mf, final_g.reshape(1, d))


def _moe_layer(x, mod, norm_g, w_router, wg, wu, wd, final_g):
    b, t, d = x.shape
    m = b * t
    h_tiles, mi, mf, cnt = _route(x, mod, norm_g, w_router)
    counts = cnt[0, :N_EXPERTS].astype(jnp.int32)
    tiles_per = (counts + MOE_TM - 1) // MOE_TM
    tile_end = jnp.cumsum(tiles_per)
    tile_start = tile_end - tiles_per
    seg_start = tile_start * MOE_TM
    nt = (TOP_K * m) // MOE_TM + N_EXPERTS
    tile_id = jnp.arange(nt, dtype=jnp.int32)
    used_id = jnp.minimum(tile_id, tile_end[-1] - 1)
    tile_expert = jnp.sum(used_id[:, None] >= tile_end[None, :], axis=1).astype(jnp.int32)
    tile_rows = jnp.clip(counts[tile_expert] - (tile_id - tile_start[tile_expert]) * MOE_TM, 0, MOE_TM)
    tile_rows = jnp.where(tile_id < tile_end[-1], tile_rows, 0).astype(jnp.int32)
    pos0 = seg_start[mi[:, 0]] + mi[:, 2]
    pos1 = seg_start[mi[:, 1]] + mi[:, 3]
    y = _moe(h_tiles.reshape(m, SUBLANES, LANES), pos0, pos1, tile_expert, tile_rows, wg, wu, wd)
    return _combine(x, mod, y.reshape(-1, LANES), mf, final_g)


def _permute_w_in(w):
    d = w.shape[0]
    return jnp.concatenate([w[:, 0:768], w[:, 1312:1568], w[:, 800:1312], w[:, 1568:1824], w[:, 1840:2608],
                            w[:, 768:800], w[:, 1824:1840], jnp.zeros((d, P_WIDTH - 2608), w.dtype)],
                           axis=1).astype(BF16)


def _mixers(proj, proj_c, gla_w_lr, gla_b_lr, gla_norm_g, mlstm_conv, mlstm_gate_b, sgu_w, sgu_b, ctx_out, tables):
    (p, ps), (pc, pcs) = proj, proj_c
    wlr = jnp.zeros((LANES, 2 * GLA_QK_W), F32)
    wlr = wlr.at[0:GLA_RANK, 0:GLA_QK_W].set(gla_w_lr[0]).at[GLA_RANK:2 * GLA_RANK, GLA_QK_W:].set(gla_w_lr[1])
    blr = gla_b_lr.reshape(1, 2 * GLA_QK_W)
    gla, gla_c = _gla(p, ps, pc, pcs, wlr, blr, gla_norm_g.reshape(1, GROUP_W), ctx_out)
    conv_w = mlstm_conv.reshape(9, 2 * GROUP_W)
    gate_b = jnp.zeros((1, LANES), F32).at[0, SMALL_GATE_LANE:SMALL_GATE_LANE + 4 * N_HEADS].set(
        mlstm_gate_b.reshape(-1))
    ml, ml_c = _mlstm(p, ps, pc, pcs, conv_w, gate_b, ctx_out)
    bias = jnp.repeat(sgu_b.T, HEAD_DIM, axis=1)
    dft_c, tabs = tables
    sg, z = _sgu_fourier1(p, sgu_w, bias, dft_c)
    fo = _fourier2(z, *tabs[p.shape[1]])
    mix = (gla, ml, sg, fo)
    if not ctx_out:
        return mix, None
    sg_c, z_c = _sgu_fourier1(pc, sgu_w, bias, dft_c)
    fo_c = _fourier2(z_c, *tabs[pc.shape[1]])
    return mix, (gla_c, ml_c, sg_c, fo_c)


def kernel(x, c, ctx, c_ctx, w_ada, b_ada, norm_mix_g, norm_ffn_g, w_in, w_out, gla_w_lr, gla_b_lr, gla_norm_g,
           mlstm_conv, mlstm_gate_b, sgu_w, sgu_b, ffn_w_gate, ffn_w_up, ffn_w_down, moe_router, moe_w_gate,
           moe_w_up, moe_w_down, final_norm_g):
    depth = w_ada.shape[0]
    assert depth == 2, "layer 0 is the dense layer with context output, layer 1 the MoE layer"
    b, t, d = x.shape
    tc = ctx.shape[1]
    cvec = jnp.zeros((16, d), F32).at[:b].set(c).at[b].set(c_ctx)
    mods = _ada(cvec, w_ada, b_ada).reshape(depth, 16, 6, d)
    tables = (jnp.asarray(_channel_dft()).astype(BF16),
              {n: tuple(jnp.asarray(a).astype(BF16) for a in _dft_tables(n)) for n in {t, tc}})
    xc = ctx
    for l in range(depth):
        ctx_out = l < depth - 1
        mod, mod_c = mods[l, :b], mods[l, b:b + 1]
        w_in_l = _permute_w_in(w_in[l])
        w_out_l = w_out[l].astype(BF16)
        p = _in_proj(x, mod, norm_mix_g[l], w_in_l)
        pc = _in_proj(xc, mod_c, norm_mix_g[l], w_in_l)
        mix, mix_c = _mixers(p, pc, gla_w_lr[l], gla_b_lr[l], gla_norm_g[l], mlstm_conv[l], mlstm_gate_b[l],
                             sgu_w[l], sgu_b[l], ctx_out, tables)
        x = _out_proj(x, mod, mix, w_out_l)
        if ctx_out:
            xc = _out_proj(xc, mod_c, mix_c, w_out_l)
        i = l // 2
        if l % 2 == 0:
            wg, wu, wd = ffn_w_gate[i].astype(BF16), ffn_w_up[i].astype(BF16), ffn_w_down[i].astype(BF16)
            x = _ffn(x, mod, norm_ffn_g[l], wg, wu, wd)
            if ctx_out:
                xc = _ffn(xc, mod_c, norm_ffn_g[l], wg, wu, wd)
        else:
            x = _moe_layer(x, mod, norm_ffn_g[l], moe_router[i], moe_w_gate[i], moe_w_up[i], moe_w_down[i],
                           final_norm_g)
    return x
```

```python
import functools

import numpy as np
import jax
import jax.numpy as jnp
from jax import lax
from jax.experimental import pallas as pl
from jax.experimental.pallas import tpu as pltpu

F32 = jnp.float32
BF16 = jnp.bfloat16
HIGHEST = lax.Precision.HIGHEST

D_MODEL = 1024
GROUP_W = 256
N_HEADS = 4
HEAD_DIM = 64
GLA_DK = 32
GLA_QK_W = N_HEADS * GLA_DK
GLA_RANK = 16
GLA_NORMALIZER = 16.0
GLA_BLOCK = 128
GRID_W = 64
SGU_CHUNK = 128
N_EXPERTS = 8
TOP_K = 2
EPS = 1e-6

LANES = 128
SUBLANES = 8
VMEM_LIMIT = 52 * 1024 * 1024

P_GLA_QK, P_GLA_V, P_GLA_G, P_ML_V = 0, 1, 2, 3
P_ML_QK = 2
P_ML_OG, P_SGU_U, P_SGU_V, P_FOUR = 6, 7, 8, 9
P_MAIN_W = 2560
P_WIDTH = 2688
SMALL_GATE_LANE = 32

MOE_TM = 1024
MOE_TF = 512
MOE_COPY_ROWS = 171


def _cparams(sem, **kw):
    return pltpu.CompilerParams(dimension_semantics=sem, vmem_limit_bytes=VMEM_LIMIT, **kw)


def _sigmoid(x):
    return 1.0 / (1.0 + jnp.exp(-x))


def _silu(x):
    return x * _sigmoid(x)


def _log_sigmoid(x):
    return jnp.minimum(x, 0.0) - jnp.log(1.0 + jnp.exp(-jnp.abs(x)))


def _norm_mod(x, g, shift, scale):
    ms = jnp.mean(x * x, axis=-1, keepdims=True)
    return (x * lax.rsqrt(ms + EPS) * g) * (1.0 + scale) + shift


def _bdot(a, b):
    return jnp.dot(a.astype(BF16), b.astype(BF16), preferred_element_type=F32)


def _bdot_nt(a, b):
    return lax.dot_general(a.astype(BF16), b.astype(BF16), (((1,), (1,)), ((), ())), preferred_element_type=F32)


def _fdot(a, b):
    return jnp.dot(a, b, precision=HIGHEST, preferred_element_type=F32)


def _split3(x):
    hi = x.astype(BF16)
    r = x - hi.astype(F32)
    mid = r.astype(BF16)
    lo = (r - mid.astype(F32)).astype(BF16)
    return hi, mid, lo


def _dot3(a, b):
    a_hi, a_lo, _ = _split3(a)
    b_hi, b_lo, _ = _split3(b)
    dot = functools.partial(jnp.dot, preferred_element_type=F32)
    return dot(a_hi, b_hi) + dot(a_lo, b_hi) + dot(a_hi, b_lo)


def _tri_dot(tri, x):
    n = x.shape[1]
    y = jnp.dot(tri.astype(BF16), jnp.concatenate(_split3(x), axis=1), preferred_element_type=F32)
    return y[:, :n] + y[:, n:2 * n] + y[:, 2 * n:]


def _ada_kernel(c_ref, w_ref, b_ref, o_ref):
    o_ref[0] = _fdot(_silu(c_ref[...]), w_ref[0]) + b_ref[0]


def _ada(cvec, w_ada, b_ada):
    depth, d, n = w_ada.shape
    rows = cvec.shape[0]
    tn = 1536
    return pl.pallas_call(
        _ada_kernel,
        grid=(depth, n // tn),
        in_specs=[pl.BlockSpec((rows, d), lambda l, j: (0, 0)),
                  pl.BlockSpec((1, d, tn), lambda l, j: (l, 0, j)),
                  pl.BlockSpec((1, 1, tn), lambda l, j: (l, 0, j))],
        out_specs=pl.BlockSpec((1, rows, tn), lambda l, j: (l, 0, j)),
        out_shape=jax.ShapeDtypeStruct((depth, rows, n), F32),
        compiler_params=_cparams(("parallel", "parallel")),
        name="ada",
    )(cvec, w_ada, b_ada.reshape(depth, 1, n))


def _in_kernel(x_ref, mod_ref, g_ref, w_ref, o_ref, os_ref):
    h = _norm_mod(x_ref[0], g_ref[...], mod_ref[0, 0:1, :], mod_ref[0, 1:2, :])
    acc = jnp.dot(h.astype(BF16), w_ref[...], preferred_element_type=F32)
    o_ref[0] = acc[:, :P_MAIN_W].astype(o_ref.dtype)
    os_ref[0] = acc[:, P_MAIN_W:]


def _in_proj(x, mod, g, w):
    b, t, d = x.shape
    n = w.shape[1]
    tm = min(t, 512)
    mod_map = (lambda i, j: (i, 0, 0)) if mod.shape[0] == b else (lambda i, j: (0, 0, 0))
    return pl.pallas_call(
        _in_kernel,
        grid=(b, t // tm),
        in_specs=[pl.BlockSpec((1, tm, d), lambda i, j: (i, j, 0)),
                  pl.BlockSpec((1, 6, d), mod_map),
                  pl.BlockSpec((1, d), lambda i, j: (0, 0)),
                  pl.BlockSpec((d, n), lambda i, j: (0, 0))],
        out_specs=[pl.BlockSpec((1, tm, P_MAIN_W), lambda i, j: (i, j, 0)),
                   pl.BlockSpec((1, tm, n - P_MAIN_W), lambda i, j: (i, j, 0))],
        out_shape=[jax.ShapeDtypeStruct((b, t, P_MAIN_W), BF16), jax.ShapeDtypeStruct((b, t, n - P_MAIN_W), F32)],
        compiler_params=_cparams(("parallel", "parallel")),
        name="in_proj",
    )(x, mod, g.reshape(1, d), w)


def _out_kernel(x_ref, mod_ref, a_ref, b_ref, c_ref, d_ref, w_ref, o_ref):
    acc = jnp.dot(a_ref[0], w_ref[0:GROUP_W, :], preferred_element_type=F32)
    acc += jnp.dot(b_ref[0], w_ref[GROUP_W:2 * GROUP_W, :], preferred_element_type=F32)
    acc += jnp.dot(c_ref[0], w_ref[2 * GROUP_W:3 * GROUP_W, :], preferred_element_type=F32)
    acc += jnp.dot(d_ref[0], w_ref[3 * GROUP_W:4 * GROUP_W, :], preferred_element_type=F32)
    o_ref[0] = x_ref[0] + mod_ref[0, 2:3, :] * acc


def _out_proj(x, mod, mixes, w):
    b, t, d = x.shape
    tm = min(t, 512)
    mod_map = (lambda i, j: (i, 0, 0)) if mod.shape[0] == b else (lambda i, j: (0, 0, 0))
    mix_spec = pl.BlockSpec((1, tm, GROUP_W), lambda i, j: (i, j, 0))
    return pl.pallas_call(
        _out_kernel,
        grid=(b, t // tm),
        in_specs=[pl.BlockSpec((1, tm, d), lambda i, j: (i, j, 0)),
                  pl.BlockSpec((1, 6, d), mod_map),
                  mix_spec, mix_spec, mix_spec, mix_spec,
                  pl.BlockSpec((d, d), lambda i, j: (0, 0))],
        out_specs=pl.BlockSpec((1, tm, d), lambda i, j: (i, j, 0)),
        out_shape=jax.ShapeDtypeStruct((b, t, d), F32),
        compiler_params=_cparams(("parallel", "parallel")),
        name="out_proj",
    )(x, mod, *mixes, w)


def _ffn_kernel(x_ref, mod_ref, g_ref, wg_ref, wu_ref, wd_ref, o_ref, h_sc, acc_sc):
    j = pl.program_id(2)

    @pl.when(j == 0)
    def _():
        h = _norm_mod(x_ref[0], g_ref[...], mod_ref[0, 3:4, :], mod_ref[0, 4:5, :])
        h_sc[...] = h.astype(BF16)
        acc_sc[...] = jnp.zeros_like(acc_sc)

    h = h_sc[...]
    gate = jnp.dot(h, wg_ref[...], preferred_element_type=F32)
    up = jnp.dot(h, wu_ref[...], preferred_element_type=F32)
    acc_sc[...] += jnp.dot((_silu(gate) * up).astype(BF16), wd_ref[...], preferred_element_type=F32)

    @pl.when(j == pl.num_programs(2) - 1)
    def _():
        o_ref[0] = x_ref[0] + mod_ref[0, 5:6, :] * acc_sc[...]


def _ffn(x, mod, g, wg, wu, wd):
    b, t, d = x.shape
    f = wg.shape[1]
    tm = min(t, 512)
    tf = f // 2
    mod_map = (lambda i, j, k: (i, 0, 0)) if mod.shape[0] == b else (lambda i, j, k: (0, 0, 0))
    return pl.pallas_call(
        _ffn_kernel,
        grid=(b, t // tm, f // tf),
        in_specs=[pl.BlockSpec((1, tm, d), lambda i, j, k: (i, j, 0)),
                  pl.BlockSpec((1, 6, d), mod_map),
                  pl.BlockSpec((1, d), lambda i, j, k: (0, 0)),
                  pl.BlockSpec((d, tf), lambda i, j, k: (0, k)),
                  pl.BlockSpec((d, tf), lambda i, j, k: (0, k)),
                  pl.BlockSpec((tf, d), lambda i, j, k: (k, 0))],
        out_specs=pl.BlockSpec((1, tm, d), lambda i, j, k: (i, j, 0)),
        out_shape=jax.ShapeDtypeStruct((b, t, d), F32),
        scratch_shapes=[pltpu.VMEM((tm, d), BF16), pltpu.VMEM((tm, d), F32)],
        compiler_params=_cparams(("parallel", "parallel", "arbitrary")),
        name="ffn",
    )(x, mod, g.reshape(1, d), wg, wu, wd)


def _iota(shape, dim):
    return lax.broadcasted_iota(jnp.int32, shape, dim)


def _gla_step(fwd, bwd, sf_ref, sb_ref, want_out):
    (q_f, k_f, v_f, la_f), (q_b, k_b, v_b, la_b) = fwd, bwd
    L, n = q_f.shape
    r, c = _iota((L, L), 0), _iota((L, L), 1)
    keep_f, keep_b = c <= r, c >= r
    z3 = jnp.zeros((L, 3 * n), BF16)
    tri2 = jnp.concatenate([jnp.where(keep_f, 1.0, 0.0), jnp.where(keep_b, 1.0, 0.0)], axis=1).astype(BF16)
    terms = jnp.concatenate([jnp.concatenate(list(_split3(la_f)) + [z3], axis=1),
                             jnp.concatenate([z3] + list(_split3(la_b)), axis=1)], axis=0)
    y = jnp.dot(tri2, terms, preferred_element_type=F32)
    cum_f = y[:, :n] + y[:, n:2 * n] + y[:, 2 * n:3 * n]
    cum_b = y[:, 3 * n:4 * n] + y[:, 4 * n:5 * n] + y[:, 5 * n:]
    tot_f = jnp.sum(la_f, axis=0, keepdims=True)
    tot_b = jnp.sum(la_b, axis=0, keepdims=True)
    sp_f, sp_b = sf_ref[...], sb_ref[...]
    zq = jnp.zeros((L, n), F32)
    outs = (None, None)
    if want_out:
        ref_f = cum_f[L // 2 - 1:L // 2, :]
        ref_b = cum_b[L // 2:L // 2 + 1, :]
        lane_q = _iota((1, n), 1) // GLA_DK
        lane_v = _iota((1, GROUP_W), 1) // HEAD_DIM

        def stack_q(qr, first):
            rows = [jnp.where(lane_q == h, qr, 0.0) for h in range(N_HEADS)]
            return jnp.concatenate([jnp.concatenate([m, zq] if first else [zq, m], axis=1) for m in rows], axis=0)

        qs = jnp.concatenate([stack_q(q_f * jnp.exp(cum_f - ref_f), True),
                              stack_q(q_b * jnp.exp(cum_b - ref_b), False)], axis=0)
        kc = jnp.concatenate([k_f * jnp.exp(ref_f - cum_f), k_b * jnp.exp(ref_b - cum_b)], axis=1)
        sc = _bdot_nt(qs, kc)

        def heads_along_lanes(part, keep):
            return jnp.concatenate([jnp.where(keep, part[h * L:(h + 1) * L, :], 0.0).astype(BF16)
                                    for h in range(N_HEADS)], axis=1)

        def heads_along_rows(v):
            return jnp.concatenate([jnp.where(lane_v == h, v, 0.0).astype(BF16) for h in range(N_HEADS)], axis=0)

        z4 = jnp.zeros((L, N_HEADS * L), BF16)
        lhs = jnp.concatenate([jnp.concatenate([heads_along_lanes(sc[:N_HEADS * L], keep_f), z4], axis=1),
                               jnp.concatenate([z4, heads_along_lanes(sc[N_HEADS * L:], keep_b)], axis=1)], axis=0)
        pv = jnp.dot(lhs, jnp.concatenate([heads_along_rows(v_f), heads_along_rows(v_b)], axis=0),
                     preferred_element_type=F32)
        qd = jnp.concatenate([jnp.concatenate([q_f * jnp.exp(cum_f), zq], axis=1),
                              jnp.concatenate([zq, q_b * jnp.exp(cum_b)], axis=1)], axis=0)
        out = pv + _bdot_nt(qd, jnp.concatenate([sp_f, sp_b], axis=1))
        outs = (out[:L], out[L:])
    ke = jnp.concatenate([jnp.concatenate([k_f * jnp.exp(tot_f - cum_f), zq], axis=1),
                          jnp.concatenate([zq, k_b * jnp.exp(tot_b - cum_b)], axis=1)], axis=0)
    v_t = jnp.concatenate([v_f, v_b], axis=0).T
    upd = _bdot(v_t, ke)
    bd = (_iota((GROUP_W, n), 0) // HEAD_DIM) == (_iota((GROUP_W, n), 1) // GLA_DK)
    sf_ref[...] = jnp.exp(tot_f) * sp_f + jnp.where(bd, upd[:, :n], 0.0)
    sb_ref[...] = jnp.exp(tot_b) * sp_b + jnp.where(bd, upd[:, n:], 0.0)
    return outs


def _gla_kernel(qk_ref, v_ref, g_ref, sm_ref, qkc_ref, vc_ref, gc_ref, smc_ref, wlr_ref, blr_ref, ng_ref,
                *rest, ctx_out):
    if ctx_out:
        o_ref, oc_ref, la_sc, lac_sc, of_sc, ob_sc, ofc_sc, obc_sc, sf_sc, sb_sc = rest
    else:
        o_ref, la_sc, lac_sc, of_sc, ob_sc, sf_sc, sb_sc = rest
        oc_ref = ofc_sc = obc_sc = None
    t = qk_ref.shape[1]
    tc = qkc_ref.shape[1]
    qscale = GLA_DK ** -0.5

    def log_decay(sm):
        z = _dot3(sm, wlr_ref[...]) + blr_ref[...]
        return _log_sigmoid(z) * (1.0 / GLA_NORMALIZER)

    lac_sc[...] = log_decay(smc_ref[0])
    rows_blk = 256

    def la_body(i, carry):
        r0 = pl.multiple_of(i * rows_blk, rows_blk)
        la_sc[pl.ds(r0, rows_blk), :] = log_decay(sm_ref[0, pl.ds(r0, rows_blk), :])
        return carry

    lax.fori_loop(0, t // rows_blk, la_body, 0)
    sf_sc[...] = jnp.zeros_like(sf_sc)
    sb_sc[...] = jnp.zeros_like(sb_sc)

    def run(n_rows, qk, v, la, of, ob, want_out):
        n = n_rows // GLA_BLOCK

        def body(i, carry):
            rf = pl.multiple_of(i * GLA_BLOCK, GLA_BLOCK)
            rb = pl.multiple_of((n - 1 - i) * GLA_BLOCK, GLA_BLOCK)
            blks = []
            for r0, lane0 in ((rf, 0), (rb, GLA_QK_W)):
                qkb = qk[0, pl.ds(r0, GLA_BLOCK), :].astype(F32)
                blks.append((qkb[:, :GLA_QK_W] * qscale, qkb[:, GLA_QK_W:], v[0, pl.ds(r0, GLA_BLOCK), :].astype(F32),
                             la[pl.ds(r0, GLA_BLOCK), lane0:lane0 + GLA_QK_W]))
            out_f, out_b = _gla_step(blks[0], blks[1], sf_sc, sb_sc, want_out)
            if want_out:
                of[pl.ds(rf, GLA_BLOCK), :] = out_f
                ob[pl.ds(rb, GLA_BLOCK), :] = out_b
            return carry

        lax.fori_loop(0, n, body, 0)

    run(tc, qkc_ref, vc_ref, lac_sc, ofc_sc, obc_sc, ctx_out)
    run(t, qk_ref, v_ref, la_sc, of_sc, ob_sc, True)

    grp = jnp.where((_iota((GROUP_W, GROUP_W), 0) // HEAD_DIM) == (_iota((GROUP_W, GROUP_W), 1) // HEAD_DIM),
                    1.0 / HEAD_DIM, 0.0).astype(BF16)

    def finish(n_rows, of, ob, g, dst):
        blk = min(n_rows, 256)

        def body(i, carry):
            r0 = pl.multiple_of(i * blk, blk)
            o = of[pl.ds(r0, blk), :] + ob[pl.ds(r0, blk), :]
            sq_hi, sq_lo, _ = _split3(o * o)
            ms = (jnp.dot(sq_hi, grp, preferred_element_type=F32) + jnp.dot(sq_lo, grp, preferred_element_type=F32))
            res = o * lax.rsqrt(ms + EPS) * ng_ref[...] * _silu(g[0, pl.ds(r0, blk), :].astype(F32))
            dst[0, pl.ds(r0, blk), :] = res.astype(dst.dtype)
            return carry

        lax.fori_loop(0, n_rows // blk, body, 0)

    finish(t, of_sc, ob_sc, g_ref, o_ref)
    if ctx_out:
        finish(tc, ofc_sc, obc_sc, gc_ref, oc_ref)


def _gla(p, ps, pc, pcs, wlr, blr, ng, ctx_out):
    b, t, _ = p.shape
    tc = pc.shape[1]

    def col(width, idx, rows):
        return pl.BlockSpec((1, rows, width), lambda i: (i, 0, idx))

    full = lambda shape: pl.BlockSpec(shape, lambda i: tuple(0 for _ in shape))
    in_specs = [col(GROUP_W, P_GLA_QK, t), col(GROUP_W, P_GLA_V, t), col(GROUP_W, P_GLA_G, t), col(LANES, 0, t),
                col(GROUP_W, P_GLA_QK, tc), col(GROUP_W, P_GLA_V, tc), col(GROUP_W, P_GLA_G, tc),
                col(LANES, 0, tc),
                full(wlr.shape), full(blr.shape), full(ng.shape)]
    out_specs = [pl.BlockSpec((1, t, GROUP_W), lambda i: (i, 0, 0))]
    out_shape = [jax.ShapeDtypeStruct((b, t, GROUP_W), BF16)]
    scratch = [pltpu.VMEM((t, 2 * GLA_QK_W), F32), pltpu.VMEM((tc, 2 * GLA_QK_W), F32),
               pltpu.VMEM((t, GROUP_W), F32), pltpu.VMEM((t, GROUP_W), F32)]
    if ctx_out:
        out_specs.append(pl.BlockSpec((1, tc, GROUP_W), lambda i: (i, 0, 0)))
        out_shape.append(jax.ShapeDtypeStruct((b, tc, GROUP_W), BF16))
        scratch += [pltpu.VMEM((tc, GROUP_W), F32), pltpu.VMEM((tc, GROUP_W), F32)]
    scratch += [pltpu.VMEM((GROUP_W, GLA_QK_W), F32), pltpu.VMEM((GROUP_W, GLA_QK_W), F32)]
    res = pl.pallas_call(
        functools.partial(_gla_kernel, ctx_out=ctx_out),
        grid=(b,),
        in_specs=in_specs, out_specs=out_specs, out_shape=out_shape, scratch_shapes=scratch,
        compiler_params=_cparams(("parallel",)),
        name="gla",
    )(p, p, p, ps, pc, pc, pc, pcs, wlr, blr, ng)
    return (res[0], res[1]) if ctx_out else (res[0], None)


ML_BLOCK = 128
N_PAIRS = N_HEADS // 2


def _gate_lane(d, h):
    return SMALL_GATE_LANE + 8 * d + h


def _mlstm_selectors():
    sel_head = np.zeros((LANES, 2 * N_HEADS * LANES), np.float32)
    sel_pair = np.zeros((LANES, 2 * N_PAIRS * LANES), np.float32)
    for d in range(2):
        for h in range(N_HEADS):
            sel_head[_gate_lane(d, h), (d * N_HEADS + h) * LANES:(d * N_HEADS + h + 1) * LANES] = 1.0
            p, hh = divmod(h, 2)
            lo = (d * N_PAIRS + p) * LANES + hh * HEAD_DIM
            sel_pair[_gate_lane(d, h), lo:lo + HEAD_DIM] = 1.0
    return sel_head, sel_pair


def _mlstm_block(blk_f, blk_b, cn_ref, m_prev, selh_ref, selp_ref, want_out):
    L = ML_BLOCK
    lane = _iota((1, LANES), 1)
    is_b = (lane >= _gate_lane(1, 0)) & (lane < _gate_lane(1, 0) + N_HEADS)
    valid = ((lane >= _gate_lane(0, 0)) & (lane < _gate_lane(0, 0) + N_HEADS)) | is_b
    r, c = _iota((L, L), 0), _iota((L, L), 1)
    keep_f, keep_b = c <= r, c >= r
    lf_f = pltpu.roll(_log_sigmoid(blk_f[2]), LANES - 4, 1)
    lf_b = pltpu.roll(_log_sigmoid(blk_b[2]), LANES - 4, 1)
    cum_f = _tri_dot(jnp.where(keep_f, 1.0, 0.0), lf_f)
    cum_b = _tri_dot(jnp.where(keep_b, 1.0, 0.0), lf_b)
    cum = jnp.where(valid, jnp.where(is_b, cum_b, cum_f), 0.0)
    ig = jnp.where(valid, jnp.where(is_b, blk_b[2], blk_f[2]), 0.0)
    tot = jnp.where(valid, jnp.where(is_b, cum_b[0:1, :], cum_f[L - 1:L, :]), 0.0)
    b = ig - cum
    row = _iota((L, 1), 0)
    pm = b
    k = 1
    while k < L:
        from_prev = jnp.where(row >= k, pltpu.roll(pm, k, 0), -jnp.inf)
        from_next = jnp.where(row < L - k, pltpu.roll(pm, L - k, 0), -jnp.inf)
        pm = jnp.maximum(pm, jnp.where(is_b, from_next, from_prev))
        k *= 2
    mm = jnp.maximum(m_prev, pm)
    w_end = tot + b
    m_chunk = jnp.max(w_end, axis=0, keepdims=True)
    m_new = jnp.maximum(tot + m_prev, m_chunk)
    a_row = jnp.exp(tot + m_prev - m_new)
    g_row = jnp.exp(m_chunk - m_new)
    cols = [jnp.exp(w_end - m_chunk)]
    if want_out:
        cols += [jnp.exp(m_prev - mm), jnp.exp(-(cum + mm))]
        mm_hi, mm_lo, _ = _split3(mm)
        mm_t = jnp.dot(jnp.concatenate([mm_hi, mm_lo], axis=1), jnp.concatenate([selh_ref[...]] * 2, axis=0),
                       preferred_element_type=F32)
        b_rows = b.T
    pair_t = _bdot(jnp.concatenate(cols, axis=0), selp_ref[...])
    lo_half = lane < HEAD_DIM
    ones_t = jnp.ones((L, LANES), F32)
    zeros_2t = jnp.zeros((L, 2 * LANES), BF16)
    lane_h = _iota((1, GROUP_W), 1) // HEAD_DIM
    bd = (_iota((LANES, 2 * LANES), 0) // HEAD_DIM) == ((_iota((LANES, 2 * LANES), 1) % LANES) // HEAD_DIM)
    outs = {}
    for d, (blk, keep) in enumerate(((blk_f, keep_f), (blk_b, keep_b))):
        qk, v, _ = blk
        q_all, k_all = qk[:, :GROUP_W], qk[:, GROUP_W:]
        cn = [cn_ref[d * N_PAIRS + p] for p in range(N_PAIRS)]
        tcol = [(d * N_PAIRS + p) * LANES for p in range(N_PAIRS)]
        if want_out:
            qs = jnp.concatenate([jnp.where(lane_h == h, q_all, 0.0) for h in range(N_HEADS)], axis=0)
            sc = _bdot_nt(qs, k_all)
            s_parts, va_rows = [], []
            for h in range(N_HEADS):
                p, hh = divmod(h, 2)
                mine = lo_half if hh == 0 else jnp.logical_not(lo_half)
                gl = _gate_lane(d, h)
                hcol = (d * N_HEADS + h) * LANES
                e = jnp.where(keep, jnp.exp(b_rows[gl:gl + 1, :] - mm_t[:, hcol:hcol + LANES]), 0.0)
                s_parts.append((e * sc[h * L:(h + 1) * L, :]).astype(BF16))
                vp = v[:, p * LANES:(p + 1) * LANES]
                va = jnp.concatenate([jnp.where(mine, vp, 0.0), jnp.where(mine, ones_t, 0.0)], axis=1).astype(BF16)
                va_rows.append(jnp.concatenate([va, zeros_2t] if p == 0 else [zeros_2t, va], axis=1))
            nd = jnp.dot(jnp.concatenate(s_parts, axis=1), jnp.concatenate(va_rows, axis=0),
                         preferred_element_type=F32)
            cn_bd = jnp.concatenate([jnp.concatenate([cn[0].astype(BF16), zeros_2t], axis=1),
                                     jnp.concatenate([zeros_2t, cn[1].astype(BF16)], axis=1)], axis=0)
            w_int = [pair_t[L:2 * L, tc0:tc0 + LANES] for tc0 in tcol]
            nd = nd + (jnp.concatenate([w_int[0], w_int[0], w_int[1], w_int[1]], axis=1)
                       * jnp.dot(q_all.astype(BF16), cn_bd, preferred_element_type=F32))
            for p in range(N_PAIRS):
                c0 = 2 * p * LANES
                outs[(d, p)] = nd[:, c0:c0 + LANES] / jnp.maximum(jnp.abs(nd[:, c0 + LANES:c0 + 2 * LANES]),
                                                                  pair_t[2 * L:3 * L, tcol[p]:tcol[p] + LANES])
        w_k = jnp.concatenate([pair_t[0:L, tc0:tc0 + LANES] for tc0 in tcol], axis=1)
        kw_t = (k_all * w_k).T
        upd_all = _bdot(kw_t, jnp.concatenate([v, ones_t], axis=1))
        for p in range(N_PAIRS):
            rows = slice(p * LANES, (p + 1) * LANES)
            upd = jnp.concatenate([upd_all[rows, p * LANES:(p + 1) * LANES], upd_all[rows, GROUP_W:]], axis=1)
            ga, gb = _gate_lane(d, 2 * p), _gate_lane(d, 2 * p + 1)
            half = (HEAD_DIM, 2 * LANES)
            a_t = jnp.concatenate([jnp.broadcast_to(a_row[:, ga:ga + 1], half),
                                   jnp.broadcast_to(a_row[:, gb:gb + 1], half)], axis=0)
            g_t = jnp.concatenate([jnp.broadcast_to(g_row[:, ga:ga + 1], half),
                                   jnp.broadcast_to(g_row[:, gb:gb + 1], half)], axis=0)
            cn_ref[d * N_PAIRS + p] = a_t * cn[p] + g_t * jnp.where(bd, upd, 0.0)
    return (outs if want_out else None), m_new


def _mlstm_kernel(qk_ref, v_ref, og_ref, sm_ref, qkc_ref, vc_ref, ogc_ref, smc_ref, cw_ref, gb_ref,
                  selh_ref, selp_ref, *rest, ctx_out):
    if ctx_out:
        o_ref, oc_ref, qc_sc, qcc_sc, hf_sc, hb_sc, hfc_sc, hbc_sc, cn_sc = rest
    else:
        o_ref, qc_sc, qcc_sc, hf_sc, hb_sc, cn_sc = rest
        oc_ref = hfc_sc = hbc_sc = None
    t = qk_ref.shape[1]
    tc = qkc_ref.shape[1]
    w2 = 2 * GROUP_W
    kscale = HEAD_DIM ** -0.5
    n_rows_grid = t // GRID_W

    def tap(dr, dc):
        return cw_ref[dr * 3 + dc:dr * 3 + dc + 1, :]

    def shifted(blk, dc):
        n = blk.shape[0]
        if dc == 1:
            return blk
        ridx = _iota((n, 1), 0)
        if dc == 0:
            return jnp.where(ridx == 0, 0.0, pltpu.roll(blk, 1, 0))
        return jnp.where(ridx == n - 1, 0.0, pltpu.roll(blk, n - 1, 0))

    def finish_qk(acc):
        a = _silu(acc)
        lane = _iota((1, w2), 1)
        return jnp.where(lane >= GROUP_W, a * kscale, a)

    def conv_body(i, carry):
        r0 = pl.multiple_of(i * GRID_W, GRID_W)
        r_up = pl.multiple_of(jnp.maximum(i - 1, 0) * GRID_W, GRID_W)
        r_dn = pl.multiple_of(jnp.minimum(i + 1, n_rows_grid - 1) * GRID_W, GRID_W)
        up = qk_ref[0, pl.ds(r_up, GRID_W), :].astype(F32) * jnp.where(i > 0, 1.0, 0.0)
        mid = qk_ref[0, pl.ds(r0, GRID_W), :].astype(F32)
        dn = qk_ref[0, pl.ds(r_dn, GRID_W), :].astype(F32) * jnp.where(i < n_rows_grid - 1, 1.0, 0.0)
        acc = jnp.zeros((GRID_W, w2), F32)
        for dr, blk in enumerate((up, mid, dn)):
            for dc in range(3):
                acc = acc + shifted(blk, dc) * tap(dr, dc)
        qc_sc[pl.ds(r0, GRID_W), :] = finish_qk(acc)
        return carry

    lax.fori_loop(0, n_rows_grid, conv_body, 0)
    xc = qkc_ref[0].astype(F32)
    acc = jnp.zeros((tc, w2), F32)
    for dc in range(3):
        acc = acc + shifted(xc, dc) * tap(1, dc)
    qcc_sc[...] = finish_qk(acc)

    cn_sc[...] = jnp.zeros_like(cn_sc)

    def run(n_rows, qc, v, sm, hf, hb, want_out, m0):
        n = n_rows // ML_BLOCK

        def body(i, m):
            rf = pl.multiple_of(i * ML_BLOCK, ML_BLOCK)
            rb = pl.multiple_of((n - 1 - i) * ML_BLOCK, ML_BLOCK)
            blks = [(qc[pl.ds(r0, ML_BLOCK), :], v[0, pl.ds(r0, ML_BLOCK), :].astype(F32),
                     sm[0, pl.ds(r0, ML_BLOCK), :] + gb_ref[...]) for r0 in (rf, rb)]
            outs, m_new = _mlstm_block(blks[0], blks[1], cn_sc, m, selh_ref, selp_ref, want_out)
            if want_out:
                for p in range(N_PAIRS):
                    hf[pl.ds(rf, ML_BLOCK), p * LANES:(p + 1) * LANES] = outs[(0, p)]
                    hb[pl.ds(rb, ML_BLOCK), p * LANES:(p + 1) * LANES] = outs[(1, p)]
            return m_new

        return lax.fori_loop(0, n, body, m0)

    m1 = run(tc, qcc_sc, vc_ref, smc_ref, hfc_sc, hbc_sc, ctx_out, jnp.zeros((1, LANES), F32))
    run(t, qc_sc, v_ref, sm_ref, hf_sc, hb_sc, True, m1)

    def finish(n_rows, hf, hb, og, dst):
        blk = min(n_rows, 256)

        def body(i, carry):
            r0 = pl.multiple_of(i * blk, blk)
            res = _sigmoid(og[0, pl.ds(r0, blk), :].astype(F32)) * (hf[pl.ds(r0, blk), :] + hb[pl.ds(r0, blk), :])
            dst[0, pl.ds(r0, blk), :] = res.astype(dst.dtype)
            return carry

        lax.fori_loop(0, n_rows // blk, body, 0)

    finish(t, hf_sc, hb_sc, og_ref, o_ref)
    if ctx_out:
        finish(tc, hfc_sc, hbc_sc, ogc_ref, oc_ref)


def _mlstm(p, ps, pc, pcs, conv_w, gate_b, ctx_out):
    b, t, _ = p.shape
    tc = pc.shape[1]
    sel_head, sel_pair = (jnp.asarray(a).astype(BF16) for a in _mlstm_selectors())

    def col(width, idx, rows):
        return pl.BlockSpec((1, rows, width), lambda i: (i, 0, idx))

    full = lambda shape: pl.BlockSpec(shape, lambda i: tuple(0 for _ in shape))
    in_specs = [col(2 * GROUP_W, P_ML_QK, t), col(GROUP_W, P_ML_V, t), col(GROUP_W, P_ML_OG, t),
                col(LANES, 0, t),
                col(2 * GROUP_W, P_ML_QK, tc), col(GROUP_W, P_ML_V, tc), col(GROUP_W, P_ML_OG, tc),
                col(LANES, 0, tc),
                full(conv_w.shape), full(gate_b.shape), full(sel_head.shape), full(sel_pair.shape)]
    out_specs = [pl.BlockSpec((1, t, GROUP_W), lambda i: (i, 0, 0))]
    out_shape = [jax.ShapeDtypeStruct((b, t, GROUP_W), BF16)]
    scratch = [pltpu.VMEM((t, 2 * GROUP_W), F32), pltpu.VMEM((tc, 2 * GROUP_W), F32),
               pltpu.VMEM((t, GROUP_W), F32), pltpu.VMEM((t, GROUP_W), F32)]
    if ctx_out:
        out_specs.append(pl.BlockSpec((1, tc, GROUP_W), lambda i: (i, 0, 0)))
        out_shape.append(jax.ShapeDtypeStruct((b, tc, GROUP_W), BF16))
        scratch += [pltpu.VMEM((tc, GROUP_W), F32), pltpu.VMEM((tc, GROUP_W), F32)]
    scratch += [pltpu.VMEM((2 * N_PAIRS, LANES, 2 * LANES), F32)]
    res = pl.pallas_call(
        functools.partial(_mlstm_kernel, ctx_out=ctx_out),
        grid=(b,),
        in_specs=in_specs, out_specs=out_specs, out_shape=out_shape, scratch_shapes=scratch,
        compiler_params=_cparams(("parallel",)),
        name="mlstm",
    )(p, p, p, ps, pc, pc, pc, pcs, conv_w, gate_b, sel_head, sel_pair)
    return (res[0], res[1]) if ctx_out else (res[0], None)


def _sgu_kernel(u_ref, v_ref, f_ref, ws_ref, bias_ref, dft_ref, o_ref, z_ref):
    t = u_ref.shape[1]
    lane_g = _iota((1, GROUP_W), 1) // HEAD_DIM

    def body(i, carry):
        r0 = pl.multiple_of(i * SGU_CHUNK, SGU_CHUNK)
        v = v_ref[0, pl.ds(r0, SGU_CHUNK), :].astype(F32)
        mu = jnp.mean(v, axis=-1, keepdims=True)
        vc = v - mu
        vn = vc * lax.rsqrt(jnp.mean(vc * vc, axis=-1, keepdims=True) + EPS)
        mixed = bias_ref[...]
        for g in range(N_HEADS):
            mixed = mixed + _bdot(ws_ref[g], jnp.where(lane_g == g, vn, 0.0))
        o_ref[0, pl.ds(r0, SGU_CHUNK), :] = (u_ref[0, pl.ds(r0, SGU_CHUNK), :].astype(F32) * mixed).astype(o_ref.dtype)
        z = _bdot(f_ref[0, pl.ds(r0, SGU_CHUNK), :], dft_ref[...])
        z_ref[0, 0, pl.ds(r0, SGU_CHUNK), :] = z[:, :GROUP_W].astype(z_ref.dtype)
        z_ref[1, 0, pl.ds(r0, SGU_CHUNK), :] = z[:, GROUP_W:].astype(z_ref.dtype)
        return carry

    lax.fori_loop(0, t // SGU_CHUNK, body, 0)


def _sgu_fourier1(p, ws, bias, dft_c):
    b, t, _ = p.shape

    def col(idx):
        return pl.BlockSpec((1, t, GROUP_W), lambda i: (i, 0, idx))

    full = lambda shape: pl.BlockSpec(shape, lambda i: tuple(0 for _ in shape))
    return pl.pallas_call(
        _sgu_kernel,
        grid=(b,),
        in_specs=[col(P_SGU_U), col(P_SGU_V), col(P_FOUR), full(ws.shape), full(bias.shape), full(dft_c.shape)],
        out_specs=[pl.BlockSpec((1, t, GROUP_W), lambda i: (i, 0, 0)),
                   pl.BlockSpec((2, 1, t, GROUP_W), lambda i: (0, i, 0, 0))],
        out_shape=[jax.ShapeDtypeStruct((b, t, GROUP_W), BF16), jax.ShapeDtypeStruct((2, b, t, GROUP_W), BF16)],
        compiler_params=_cparams(("parallel",)),
        name="sgu_fourier1",
    )(p, p, p, ws, bias, dft_c)


def _fourier2_kernel(ct_ref, st_ref, z_ref, o_ref, *, scale):
    for i in range(z_ref.shape[1]):
        acc = jnp.dot(ct_ref[...], z_ref[0, i], preferred_element_type=F32)
        acc -= jnp.dot(st_ref[...], z_ref[1, i], preferred_element_type=F32)
        o_ref[i] = (acc * scale).astype(o_ref.dtype)


def _fourier2(z, cos_t, sin_t):
    _, b, t, w = z.shape
    tm = min(t, 256)
    return pl.pallas_call(
        functools.partial(_fourier2_kernel, scale=float((t * HEAD_DIM) ** -0.5)),
        grid=(t // tm,),
        in_specs=[pl.BlockSpec((tm, t), lambda i: (i, 0)),
                  pl.BlockSpec((tm, t), lambda i: (i, 0)),
                  pl.BlockSpec((2, b, t, w), lambda i: (0, 0, 0, 0), pipeline_mode=pl.Buffered(1))],
        out_specs=pl.BlockSpec((b, tm, w), lambda i: (0, i, 0)),
        out_shape=jax.ShapeDtypeStruct((b, t, w), BF16),
        compiler_params=_cparams(("parallel",)),
        name="fourier2",
    )(cos_t, sin_t, z)


def _dft_tables(n):
    k = np.arange(n, dtype=np.int64)
    ang = 2.0 * np.pi * ((k[:, None] * k[None, :]) % n).astype(np.float64) / n
    return np.cos(ang).astype(np.float32), np.sin(ang).astype(np.float32)


def _channel_dft():
    c, s = _dft_tables(HEAD_DIM)
    eye = np.eye(N_HEADS, dtype=np.float32)
    return np.concatenate([np.kron(eye, c), np.kron(eye, s)], axis=1)


def _route_kernel(x_ref, mod_ref, g_ref, wr_ref, h_ref, mi_ref, mf_ref, cnt_ref, cnt_sc):
    first = (pl.program_id(0) == 0) & (pl.program_id(1) == 0)

    @pl.when(first)
    def _():
        cnt_sc[...] = jnp.zeros_like(cnt_sc)

    tm = x_ref.shape[1]
    h = _norm_mod(x_ref[0], g_ref[...], mod_ref[0, 3:4, :], mod_ref[0, 4:5, :])
    for j in range(D_MODEL // LANES):
        h_ref[pl.ds(j, tm, stride=SUBLANES), :] = h[:, j * LANES:(j + 1) * LANES]
    logits = _fdot(h, wr_ref[...])
    lane = _iota((tm, LANES), 1).astype(F32)
    lg = jnp.where(lane < N_EXPERTS, logits, -jnp.inf)
    m1 = jnp.max(lg, axis=1, keepdims=True)
    i1 = jnp.min(jnp.where(lg == m1, lane, float(LANES)), axis=1, keepdims=True)
    lg2 = jnp.where(lane == i1, -jnp.inf, lg)
    m2 = jnp.max(lg2, axis=1, keepdims=True)
    i2 = jnp.min(jnp.where(lg2 == m2, lane, float(LANES)), axis=1, keepdims=True)
    e = jnp.exp(m2 - m1)
    g0 = 1.0 / (1.0 + e)
    g1 = e / (1.0 + e)
    onehot = jnp.where((lane == i1) | (lane == i2), 1.0, 0.0)
    strict = jnp.where(_iota((tm, tm), 1) < _iota((tm, tm), 0), 1.0, 0.0)
    before = _bdot(strict, onehot) + cnt_sc[...]
    r0 = jnp.sum(jnp.where(lane == i1, before, 0.0), axis=1, keepdims=True)
    r1 = jnp.sum(jnp.where(lane == i2, before, 0.0), axis=1, keepdims=True)
    cnt_sc[...] += jnp.sum(onehot, axis=0, keepdims=True)
    cnt_ref[...] = cnt_sc[...]
    l8 = _iota((tm, SUBLANES), 1)
    mi = jnp.where(l8 == 0, i1, jnp.where(l8 == 1, i2, jnp.where(l8 == 2, r0, r1)))
    mi_ref[...] = mi.astype(jnp.int32)
    mf_ref[...] = jnp.where(l8 == 0, g0, g1)


def _route(x, mod, g, w_router):
    b, t, d = x.shape
    m = b * t
    tm = 512
    nt = t // tm
    wr = jnp.zeros((d, LANES), F32).at[:, :N_EXPERTS].set(w_router)
    return pl.pallas_call(
        _route_kernel,
        grid=(b, nt),
        in_specs=[pl.BlockSpec((1, tm, d), lambda i, j: (i, j, 0)),
                  pl.BlockSpec((1, 6, d), lambda i, j: (i, 0, 0)),
                  pl.BlockSpec((1, d), lambda i, j: (0, 0)),
                  pl.BlockSpec((d, LANES), lambda i, j: (0, 0))],
        out_specs=[pl.BlockSpec((tm * SUBLANES, LANES), lambda i, j: (i * nt + j, 0)),
                   pl.BlockSpec((tm, SUBLANES), lambda i, j: (i * nt + j, 0)),
                   pl.BlockSpec((tm, SUBLANES), lambda i, j: (i * nt + j, 0)),
                   pl.BlockSpec((1, LANES), lambda i, j: (0, 0))],
        out_shape=[jax.ShapeDtypeStruct((m * SUBLANES, LANES), F32),
                   jax.ShapeDtypeStruct((m, SUBLANES), jnp.int32),
                   jax.ShapeDtypeStruct((m, SUBLANES), F32),
                   jax.ShapeDtypeStruct((1, LANES), F32)],
        scratch_shapes=[pltpu.VMEM((1, LANES), F32)],
        compiler_params=_cparams(("arbitrary", "arbitrary")),
        name="route",
    )(x, mod, g.reshape(1, d), wr)


def _moe_kernel(texp_ref, trows_ref, pos0_ref, pos1_ref, h_hbm, wg_ref, wu_ref, wd_ref, y_hbm,
                order_sm, xbuf, x_sc, acc_sc, ybuf, gsem, ssem):
    i = pl.program_id(0)
    j = pl.program_id(1)
    nf = pl.num_programs(1)
    tm = MOE_TM
    n_tok = pos0_ref.shape[0]
    rows = trows_ref[i]

    nt = pl.num_programs(0)
    buf_rows = tm * SUBLANES
    dump = TOP_K * n_tok
    slot = lax.rem(i, 2)
    prev_rows = trows_ref[jnp.maximum(i - 1, 0)]
    next_rows = trows_ref[jnp.minimum(i + 1, nt - 1)]
    n_full, n_rest = divmod(tm, MOE_COPY_ROWS)

    def gather_copy(tile, n_real, r, dst_slot):
        idx = jnp.where(n_real > 0, tile * tm + jnp.minimum(r, n_real - 1), 0)
        tok = jnp.right_shift(order_sm[idx], 1)
        off = pl.multiple_of(dst_slot * buf_rows + r * SUBLANES, SUBLANES)
        return pltpu.make_async_copy(h_hbm.at[tok], xbuf.at[pl.ds(off, SUBLANES), :], gsem)

    def scatter_copy(tile, n_real, r, src_slot):
        idx = jnp.where(n_real > 0, tile * tm + jnp.minimum(r, n_real - 1), 0)
        dst = jnp.where(r < n_real, order_sm[idx], dump + r)
        off = pl.multiple_of(src_slot * buf_rows + r * SUBLANES, SUBLANES)
        return pltpu.make_async_copy(ybuf.at[pl.ds(off, SUBLANES), :], y_hbm.at[dst], ssem)

    def wait_all(buf, sem):
        pltpu.make_async_copy(buf.at[pl.ds(0, buf_rows), :], buf.at[pl.ds(buf_rows, buf_rows), :], sem).wait()

    @pl.when((i == 0) & (j == 0))
    def _():
        def inv_body(t, carry):
            order_sm[pos0_ref[t]] = 2 * t
            order_sm[pos1_ref[t]] = 2 * t + 1
            return carry

        lax.fori_loop(0, n_tok, inv_body, 0, unroll=8)
        ybuf[...] = jnp.zeros_like(ybuf)

        def start(r, carry):
            gather_copy(0, rows, r, 0).start()
            return carry

        lax.fori_loop(0, tm, start, 0)

    @pl.when((j == 0) & (rows > 0))
    def _():
        wait_all(xbuf, gsem)
        base = pl.multiple_of(slot * buf_rows, SUBLANES)
        for c in range(D_MODEL // LANES):
            x_sc[:, c * LANES:(c + 1) * LANES] = xbuf[pl.ds(base + c, tm, stride=SUBLANES), :].astype(BF16)
        acc_sc[...] = jnp.zeros_like(acc_sc)

    def step(n_copies):
        r0 = j * MOE_COPY_ROWS
        for q in range(n_copies):
            gather_copy(i + 1, next_rows, r0 + q, 1 - slot).start()
            scatter_copy(i - 1, jnp.where(i > 0, prev_rows, 0), r0 + q, 1 - slot).start()
        x = x_sc[...]
        gate = jnp.dot(x, wg_ref[0].astype(BF16), preferred_element_type=F32)
        up = jnp.dot(x, wu_ref[0].astype(BF16), preferred_element_type=F32)
        acc_sc[...] += jnp.dot((_silu(gate) * up).astype(BF16), wd_ref[0].astype(BF16),
                               preferred_element_type=F32)

    pl.when((rows > 0) & (j < n_full))(functools.partial(step, MOE_COPY_ROWS))
    pl.when((rows > 0) & (j == n_full))(functools.partial(step, n_rest))
    pl.when((rows > 0) & (j > n_full))(functools.partial(step, 0))

    @pl.when((j == nf - 1) & (rows > 0))
    def _():
        wait_all(ybuf, ssem)
        base = pl.multiple_of(slot * buf_rows, SUBLANES)
        for c in range(D_MODEL // LANES):
            ybuf[pl.ds(base + c, tm, stride=SUBLANES), :] = acc_sc[:, c * LANES:(c + 1) * LANES]

    @pl.when((j == 0) & (rows == 0) & (i > 0) & (prev_rows > 0))
    def _():
        wait_all(xbuf, gsem)

        def start(r, carry):
            scatter_copy(i - 1, prev_rows, r, 1 - slot).start()
            return carry

        lax.fori_loop(0, tm, start, 0)
        wait_all(ybuf, ssem)


def _moe(h_tiles, pos0, pos1, tile_expert, tile_rows, wg, wu, wd):
    m = h_tiles.shape[0]
    n_exp, d, f = wg.shape
    nt = tile_expert.shape[0]
    tm, tf = MOE_TM, MOE_TF
    nf = f // tf
    assert nf * MOE_COPY_ROWS >= tm, "each tile needs enough column steps to issue its neighbours' row copies"
    assert nt * tm > TOP_K * m + n_exp * (tm - 1), "the last tile must be unused: it drains the copy pipeline"

    def col_tile(i, j, tr):
        return jnp.where(tr[i] > 0, j, nf - 1)

    grid_spec = pltpu.PrefetchScalarGridSpec(
        num_scalar_prefetch=4,
        grid=(nt, nf),
        in_specs=[pl.BlockSpec(memory_space=pl.ANY),
                  pl.BlockSpec((1, d, tf), lambda i, j, te, tr, p0, p1: (te[i], 0, col_tile(i, j, tr))),
                  pl.BlockSpec((1, d, tf), lambda i, j, te, tr, p0, p1: (te[i], 0, col_tile(i, j, tr))),
                  pl.BlockSpec((1, tf, d), lambda i, j, te, tr, p0, p1: (te[i], col_tile(i, j, tr), 0))],
        out_specs=pl.BlockSpec(memory_space=pl.ANY),
        scratch_shapes=[pltpu.SMEM((nt * tm,), jnp.int32),
                        pltpu.VMEM((2 * tm * SUBLANES, LANES), F32),
                        pltpu.VMEM((tm, d), BF16),
                        pltpu.VMEM((tm, d), F32),
                        pltpu.VMEM((2 * tm * SUBLANES, LANES), F32),
                        pltpu.SemaphoreType.DMA(()),
                        pltpu.SemaphoreType.DMA(())],
    )
    return pl.pallas_call(
        _moe_kernel,
        grid_spec=grid_spec,
        out_shape=jax.ShapeDtypeStruct((TOP_K * m + tm, SUBLANES, LANES), F32),
        compiler_params=_cparams(("arbitrary", "arbitrary")),
        name="moe",
    )(tile_expert, tile_rows, pos0, pos1, h_tiles, wg, wu, wd)


def _combine_kernel(x_ref, mod_ref, y_ref, mf_ref, fg_ref, o_ref):
    tm = x_ref.shape[1]
    stride = TOP_K * SUBLANES
    g0 = mf_ref[:, 0:1]
    g1 = mf_ref[:, 1:2]
    parts = []
    for c in range(D_MODEL // LANES):
        y0 = y_ref[pl.ds(c, tm, stride=stride), :]
        y1 = y_ref[pl.ds(SUBLANES + c, tm, stride=stride), :]
        parts.append(g0 * y0 + g1 * y1)
    moe = jnp.concatenate(parts, axis=1)
    x = x_ref[0] + mod_ref[0, 5:6, :] * moe
    ms = jnp.mean(x * x, axis=-1, keepdims=True)
    o_ref[0] = x * lax.rsqrt(ms + EPS) * fg_ref[...]


def _combine(x, mod, y, mf, final_g):
    b, t, d = x.shape
    tm = 256
    nt = t // tm
    rows = tm * TOP_K * SUBLANES
    return pl.pallas_call(
        _combine_kernel,
        grid=(b, nt),
        in_specs=[pl.BlockSpec((1, tm, d), lambda i, j: (i, j, 0)),
                  pl.BlockSpec((1, 6, d), lambda i, j: (i, 0, 0)),
                  pl.BlockSpec((rows, LANES), lambda i, j: (i * nt + j, 0)),
                  pl.BlockSpec((tm, SUBLANES), lambda i, j: (i * nt + j, 0)),
                  pl.BlockSpec((1, d), lambda i, j: (0, 0))],
        out_specs=pl.BlockSpec((1, tm, d), lambda i, j: (i, j, 0)),
        out_shape=jax.ShapeDtypeStruct((b, t, d), F32),
        compiler_params=_cparams(("parallel", "parallel")),
        name="combine",
    )(x, mod, y, mf, final_g.reshape(1, d))


def _moe_layer(x, mod, norm_g, w_router, wg, wu, wd, final_g):
    b, t, d = x.shape
    m = b * t
    h_tiles, mi, mf, cnt = _route(x, mod, norm_g, w_router)
    counts = cnt[0, :N_EXPERTS].astype(jnp.int32)
    tiles_per = (counts + MOE_TM - 1) // MOE_TM
    tile_end = jnp.cumsum(tiles_per)
    tile_start = tile_end - tiles_per
    seg_start = tile_start * MOE_TM
    nt = (TOP_K * m) // MOE_TM + N_EXPERTS
    tile_id = jnp.arange(nt, dtype=jnp.int32)
    used_id = jnp.minimum(tile_id, tile_end[-1] - 1)
    tile_expert = jnp.sum(used_id[:, None] >= tile_end[None, :], axis=1).astype(jnp.int32)
    tile_rows = jnp.clip(counts[tile_expert] - (tile_id - tile_start[tile_expert]) * MOE_TM, 0, MOE_TM)
    tile_rows = jnp.where(tile_id < tile_end[-1], tile_rows, 0).astype(jnp.int32)
    pos0 = seg_start[mi[:, 0]] + mi[:, 2]
    pos1 = seg_start[mi[:, 1]] + mi[:, 3]
    y = _moe(h_tiles.reshape(m, SUBLANES, LANES), pos0, pos1, tile_expert, tile_rows, wg, wu, wd)
    return _combine(x, mod, y.reshape(-1, LANES), mf, final_g)


def _permute_w_in(w):
    d = w.shape[0]
    return jnp.concatenate([w[:, 0:768], w[:, 1312:1568], w[:, 800:1312], w[:, 1568:1824], w[:, 1840:2608],
                            w[:, 768:800], w[:, 1824:1840], jnp.zeros((d, P_WIDTH - 2608), w.dtype)],
                           axis=1).astype(BF16)


def _mixers(proj, proj_c, gla_w_lr, gla_b_lr, gla_norm_g, mlstm_conv, mlstm_gate_b, sgu_w, sgu_b, ctx_out, tables):
    (p, ps), (pc, pcs) = proj, proj_c
    wlr = jnp.zeros((LANES, 2 * GLA_QK_W), F32)
    wlr = wlr.at[0:GLA_RANK, 0:GLA_QK_W].set(gla_w_lr[0]).at[GLA_RANK:2 * GLA_RANK, GLA_QK_W:].set(gla_w_lr[1])
    blr = gla_b_lr.reshape(1, 2 * GLA_QK_W)
    gla, gla_c = _gla(p, ps, pc, pcs, wlr, blr, gla_norm_g.reshape(1, GROUP_W), ctx_out)
    conv_w = mlstm_conv.reshape(9, 2 * GROUP_W)
    gate_b = jnp.zeros((1, LANES), F32).at[0, SMALL_GATE_LANE:SMALL_GATE_LANE + 4 * N_HEADS].set(
        mlstm_gate_b.reshape(-1))
    ml, ml_c = _mlstm(p, ps, pc, pcs, conv_w, gate_b, ctx_out)
    bias = jnp.repeat(sgu_b.T, HEAD_DIM, axis=1)
    dft_c, tabs = tables
    sg, z = _sgu_fourier1(p, sgu_w, bias, dft_c)
    fo = _fourier2(z, *tabs[p.shape[1]])
    mix = (gla, ml, sg, fo)
    if not ctx_out:
        return mix, None
    sg_c, z_c = _sgu_fourier1(pc, sgu_w, bias, dft_c)
    fo_c = _fourier2(z_c, *tabs[pc.shape[1]])
    return mix, (gla_c, ml_c, sg_c, fo_c)


def kernel(x, c, ctx, c_ctx, w_ada, b_ada, norm_mix_g, norm_ffn_g, w_in, w_out, gla_w_lr, gla_b_lr, gla_norm_g,
           mlstm_conv, mlstm_gate_b, sgu_w, sgu_b, ffn_w_gate, ffn_w_up, ffn_w_down, moe_router, moe_w_gate,
           moe_w_up, moe_w_down, final_norm_g):
    depth = w_ada.shape[0]
    assert depth == 2, "layer 0 is the dense layer with context output, layer 1 the MoE layer"
    b, t, d = x.shape
    tc = ctx.shape[1]
    cvec = jnp.zeros((16, d), F32).at[:b].set(c).at[b].set(c_ctx)
    mods = _ada(cvec, w_ada, b_ada).reshape(depth, 16, 6, d)
    tables = (jnp.asarray(_channel_dft()).astype(BF16),
              {n: tuple(jnp.asarray(a).astype(BF16) for a in _dft_tables(n)) for n in {t, tc}})
    xc = ctx
    for l in range(depth):
        ctx_out = l < depth - 1
        mod, mod_c = mods[l, :b], mods[l, b:b + 1]
        w_in_l = _permute_w_in(w_in[l])
        w_out_l = w_out[l].astype(BF16)
        p = _in_proj(x, mod, norm_mix_g[l], w_in_l)
        pc = _in_proj(xc, mod_c, norm_mix_g[l], w_in_l)
        mix, mix_c = _mixers(p, pc, gla_w_lr[l], gla_b_lr[l], gla_norm_g[l], mlstm_conv[l], mlstm_gate_b[l],
                             sgu_w[l], sgu_b[l], ctx_out, tables)
        x = _out_proj(x, mod, mix, w_out_l)
        if ctx_out:
            xc = _out_proj(xc, mod_c, mix_c, w_out_l)
        i = l // 2
        if l % 2 == 0:
            wg, wu, wd = ffn_w_gate[i].astype(BF16), ffn_w_up[i].astype(BF16), ffn_w_down[i].astype(BF16)
            x = _ffn(x, mod, norm_ffn_g[l], wg, wu, wd)
            if ctx_out:
                xc = _ffn(xc, mod_c, norm_ffn_g[l], wg, wu, wd)
        else:
            x = _moe_layer(x, mod, norm_ffn_g[l], moe_router[i], moe_w_gate[i], moe_w_up[i], moe_w_down[i],
                           final_norm_g)
    return x
```

```python
import functools

import numpy as np
import jax
import jax.numpy as jnp
from jax import lax
from jax.experimental import pallas as pl
from jax.experimental.pallas import tpu as pltpu

F32 = jnp.float32
BF16 = jnp.bfloat16
HIGHEST = lax.Precision.HIGHEST

D_MODEL = 1024
GROUP_W = 256
N_HEADS = 4
HEAD_DIM = 64
GLA_DK = 32
GLA_QK_W = N_HEADS * GLA_DK
GLA_RANK = 16
GLA_NORMALIZER = 16.0
GLA_BLOCK = 128
GRID_W = 64
SGU_CHUNK = 128
N_EXPERTS = 8
TOP_K = 2
EPS = 1e-6

LANES = 128
SUBLANES = 8
VMEM_LIMIT = 52 * 1024 * 1024

P_GLA_QK, P_GLA_V, P_GLA_G, P_ML_V = 0, 1, 2, 3
P_ML_QK = 2
P_ML_OG, P_SGU_U, P_SGU_V, P_FOUR = 6, 7, 8, 9
P_MAIN_W = 2560
P_WIDTH = 2688
SMALL_GATE_LANE = 32

MOE_TM = 1024
MOE_TF = 512
MOE_COPY_ROWS = 171


def _cparams(sem, **kw):
    return pltpu.CompilerParams(dimension_semantics=sem, vmem_limit_bytes=VMEM_LIMIT, **kw)


def _sigmoid(x):
    return 1.0 / (1.0 + jnp.exp(-x))


def _silu(x):
    return x * _sigmoid(x)


def _log_sigmoid(x):
    return jnp.minimum(x, 0.0) - jnp.log(1.0 + jnp.exp(-jnp.abs(x)))


def _norm_mod(x, g, shift, scale):
    ms = jnp.mean(x * x, axis=-1, keepdims=True)
    return (x * lax.rsqrt(ms + EPS) * g) * (1.0 + scale) + shift


def _bdot(a, b):
    return jnp.dot(a.astype(BF16), b.astype(BF16), preferred_element_type=F32)


def _bdot_nt(a, b):
    return lax.dot_general(a.astype(BF16), b.astype(BF16), (((1,), (1,)), ((), ())), preferred_element_type=F32)


def _fdot(a, b):
    return jnp.dot(a, b, precision=HIGHEST, preferred_element_type=F32)


def _split3(x):
    hi = x.astype(BF16)
    r = x - hi.astype(F32)
    mid = r.astype(BF16)
    lo = (r - mid.astype(F32)).astype(BF16)
    return hi, mid, lo


def _dot3(a, b):
    a_hi, a_lo, _ = _split3(a)
    b_hi, b_lo, _ = _split3(b)
    dot = functools.partial(jnp.dot, preferred_element_type=F32)
    return dot(a_hi, b_hi) + dot(a_lo, b_hi) + dot(a_hi, b_lo)


def _tri_dot(tri, x):
    n = x.shape[1]
    y = jnp.dot(tri.astype(BF16), jnp.concatenate(_split3(x), axis=1), preferred_element_type=F32)
    return y[:, :n] + y[:, n:2 * n] + y[:, 2 * n:]


def _ada_kernel(c_ref, w_ref, b_ref, o_ref):
    o_ref[0] = _fdot(_silu(c_ref[...]), w_ref[0]) + b_ref[0]


def _ada(cvec, w_ada, b_ada):
    depth, d, n = w_ada.shape
    rows = cvec.shape[0]
    tn = 1536
    return pl.pallas_call(
        _ada_kernel,
        grid=(depth, n // tn),
        in_specs=[pl.BlockSpec((rows, d), lambda l, j: (0, 0)),
                  pl.BlockSpec((1, d, tn), lambda l, j: (l, 0, j)),
                  pl.BlockSpec((1, 1, tn), lambda l, j: (l, 0, j))],
        out_specs=pl.BlockSpec((1, rows, tn), lambda l, j: (l, 0, j)),
        out_shape=jax.ShapeDtypeStruct((depth, rows, n), F32),
        compiler_params=_cparams(("parallel", "parallel")),
        name="ada",
    )(cvec, w_ada, b_ada.reshape(depth, 1, n))


def _in_kernel(x_ref, mod_ref, g_ref, w_ref, o_ref, os_ref):
    h = _norm_mod(x_ref[0], g_ref[...], mod_ref[0, 0:1, :], mod_ref[0, 1:2, :])
    acc = jnp.dot(h.astype(BF16), w_ref[...], preferred_element_type=F32)
    o_ref[0] = acc[:, :P_MAIN_W].astype(o_ref.dtype)
    os_ref[0] = acc[:, P_MAIN_W:]


def _in_proj(x, mod, g, w):
    b, t, d = x.shape
    n = w.shape[1]
    tm = min(t, 512)
    mod_map = (lambda i, j: (i, 0, 0)) if mod.shape[0] == b else (lambda i, j: (0, 0, 0))
    return pl.pallas_call(
        _in_kernel,
        grid=(b, t // tm),
        in_specs=[pl.BlockSpec((1, tm, d), lambda i, j: (i, j, 0)),
                  pl.BlockSpec((1, 6, d), mod_map),
                  pl.BlockSpec((1, d), lambda i, j: (0, 0)),
                  pl.BlockSpec((d, n), lambda i, j: (0, 0))],
        out_specs=[pl.BlockSpec((1, tm, P_MAIN_W), lambda i, j: (i, j, 0)),
                   pl.BlockSpec((1, tm, n - P_MAIN_W), lambda i, j: (i, j, 0))],
        out_shape=[jax.ShapeDtypeStruct((b, t, P_MAIN_W), BF16), jax.ShapeDtypeStruct((b, t, n - P_MAIN_W), F32)],
        compiler_params=_cparams(("parallel", "parallel")),
        name="in_proj",
    )(x, mod, g.reshape(1, d), w)


def _out_kernel(x_ref, mod_ref, a_ref, b_ref, c_ref, d_ref, w_ref, o_ref):
    acc = jnp.dot(a_ref[0], w_ref[0:GROUP_W, :], preferred_element_type=F32)
    acc += jnp.dot(b_ref[0], w_ref[GROUP_W:2 * GROUP_W, :], preferred_element_type=F32)
    acc += jnp.dot(c_ref[0], w_ref[2 * GROUP_W:3 * GROUP_W, :], preferred_element_type=F32)
    acc += jnp.dot(d_ref[0], w_ref[3 * GROUP_W:4 * GROUP_W, :], preferred_element_type=F32)
    o_ref[0] = x_ref[0] + mod_ref[0, 2:3, :] * acc


def _out_proj(x, mod, mixes, w):
    b, t, d = x.shape
    tm = min(t, 512)
    mod_map = (lambda i, j: (i, 0, 0)) if mod.shape[0] == b else (lambda i, j: (0, 0, 0))
    mix_spec = pl.BlockSpec((1, tm, GROUP_W), lambda i, j: (i, j, 0))
    return pl.pallas_call(
        _out_kernel,
        grid=(b, t // tm),
        in_specs=[pl.BlockSpec((1, tm, d), lambda i, j: (i, j, 0)),
                  pl.BlockSpec((1, 6, d), mod_map),
                  mix_spec, mix_spec, mix_spec, mix_spec,
                  pl.BlockSpec((d, d), lambda i, j: (0, 0))],
        out_specs=pl.BlockSpec((1, tm, d), lambda i, j: (i, j, 0)),
        out_shape=jax.ShapeDtypeStruct((b, t, d), F32),
        compiler_params=_cparams(("parallel", "parallel")),
        name="out_proj",
    )(x, mod, *mixes, w)


def _ffn_kernel(x_ref, mod_ref, g_ref, wg_ref, wu_ref, wd_ref, o_ref, h_sc, acc_sc):
    j = pl.program_id(2)

    @pl.when(j == 0)
    def _():
        h = _norm_mod(x_ref[0], g_ref[...], mod_ref[0, 3:4, :], mod_ref[0, 4:5, :])
        h_sc[...] = h.astype(BF16)
        acc_sc[...] = jnp.zeros_like(acc_sc)

    h = h_sc[...]
    gate = jnp.dot(h, wg_ref[...], preferred_element_type=F32)
    up = jnp.dot(h, wu_ref[...], preferred_element_type=F32)
    acc_sc[...] += jnp.dot((_silu(gate) * up).astype(BF16), wd_ref[...], preferred_element_type=F32)

    @pl.when(j == pl.num_programs(2) - 1)
    def _():
        o_ref[0] = x_ref[0] + mod_ref[0, 5:6, :] * acc_sc[...]


def _ffn(x, mod, g, wg, wu, wd):
    b, t, d = x.shape
    f = wg.shape[1]
    tm = min(t, 512)
    tf = f // 2
    mod_map = (lambda i, j, k: (i, 0, 0)) if mod.shape[0] == b else (lambda i, j, k: (0, 0, 0))
    return pl.pallas_call(
        _ffn_kernel,
        grid=(b, t // tm, f // tf),
        in_specs=[pl.BlockSpec((1, tm, d), lambda i, j, k: (i, j, 0)),
                  pl.BlockSpec((1, 6, d), mod_map),
                  pl.BlockSpec((1, d), lambda i, j, k: (0, 0)),
                  pl.BlockSpec((d, tf), lambda i, j, k: (0, k)),
                  pl.BlockSpec((d, tf), lambda i, j, k: (0, k)),
                  pl.BlockSpec((tf, d), lambda i, j, k: (k, 0))],
        out_specs=pl.BlockSpec((1, tm, d), lambda i, j, k: (i, j, 0)),
        out_shape=jax.ShapeDtypeStruct((b, t, d), F32),
        scratch_shapes=[pltpu.VMEM((tm, d), BF16), pltpu.VMEM((tm, d), F32)],
        compiler_params=_cparams(("parallel", "parallel", "arbitrary")),
        name="ffn",
    )(x, mod, g.reshape(1, d), wg, wu, wd)


def _iota(shape, dim):
    return lax.broadcasted_iota(jnp.int32, shape, dim)


def _gla_step(fwd, bwd, sf_ref, sb_ref, want_out):
    (q_f, k_f, v_f, la_f), (q_b, k_b, v_b, la_b) = fwd, bwd
    L, n = q_f.shape
    r, c = _iota((L, L), 0), _iota((L, L), 1)
    keep_f, keep_b = c <= r, c >= r
    z3 = jnp.zeros((L, 3 * n), BF16)
    tri2 = jnp.concatenate([jnp.where(keep_f, 1.0, 0.0), jnp.where(keep_b, 1.0, 0.0)], axis=1).astype(BF16)
    terms = jnp.concatenate([jnp.concatenate(list(_split3(la_f)) + [z3], axis=1),
                             jnp.concatenate([z3] + list(_split3(la_b)), axis=1)], axis=0)
    y = jnp.dot(tri2, terms, preferred_element_type=F32)
    cum_f = y[:, :n] + y[:, n:2 * n] + y[:, 2 * n:3 * n]
    cum_b = y[:, 3 * n:4 * n] + y[:, 4 * n:5 * n] + y[:, 5 * n:]
    tot_f = jnp.sum(la_f, axis=0, keepdims=True)
    tot_b = jnp.sum(la_b, axis=0, keepdims=True)
    sp_f, sp_b = sf_ref[...], sb_ref[...]
    zq = jnp.zeros((L, n), F32)
    outs = (None, None)
    if want_out:
        ref_f = cum_f[L // 2 - 1:L // 2, :]
        ref_b = cum_b[L // 2:L // 2 + 1, :]
        lane_q = _iota((1, n), 1) // GLA_DK
        lane_v = _iota((1, GROUP_W), 1) // HEAD_DIM

        def stack_q(qr, first):
            rows = [jnp.where(lane_q == h, qr, 0.0) for h in range(N_HEADS)]
            return jnp.concatenate([jnp.concatenate([m, zq] if first else [zq, m], axis=1) for m in rows], axis=0)

        qs = jnp.concatenate([stack_q(q_f * jnp.exp(cum_f - ref_f), True),
                              stack_q(q_b * jnp.exp(cum_b - ref_b), False)], axis=0)
        kc = jnp.concatenate([k_f * jnp.exp(ref_f - cum_f), k_b * jnp.exp(ref_b - cum_b)], axis=1)
        sc = _bdot_nt(qs, kc)

        def heads_along_lanes(part, keep):
            return jnp.concatenate([jnp.where(keep, part[h * L:(h + 1) * L, :], 0.0).astype(BF16)
                                    for h in range(N_HEADS)], axis=1)

        def heads_along_rows(v):
            return jnp.concatenate([jnp.where(lane_v == h, v, 0.0).astype(BF16) for h in range(N_HEADS)], axis=0)

        z4 = jnp.zeros((L, N_HEADS * L), BF16)
        lhs = jnp.concatenate([jnp.concatenate([heads_along_lanes(sc[:N_HEADS * L], keep_f), z4], axis=1),
                               jnp.concatenate([z4, heads_along_lanes(sc[N_HEADS * L:], keep_b)], axis=1)], axis=0)
        pv = jnp.dot(lhs, jnp.concatenate([heads_along_rows(v_f), heads_along_rows(v_b)], axis=0),
                     preferred_element_type=F32)
        qd = jnp.concatenate([jnp.concatenate([q_f * jnp.exp(cum_f), zq], axis=1),
                              jnp.concatenate([zq, q_b * jnp.exp(cum_b)], axis=1)], axis=0)
        out = pv + _bdot_nt(qd, jnp.concatenate([sp_f, sp_b], axis=1))
        outs = (out[:L], out[L:])
    ke = jnp.concatenate([jnp.concatenate([k_f * jnp.exp(tot_f - cum_f), zq], axis=1),
                          jnp.concatenate([zq, k_b * jnp.exp(tot_b - cum_b)], axis=1)], axis=0)
    v_t = jnp.concatenate([v_f, v_b], axis=0).T
    upd = _bdot(v_t, ke)
    bd = (_iota((GROUP_W, n), 0) // HEAD_DIM) == (_iota((GROUP_W, n), 1) // GLA_DK)
    sf_ref[...] = jnp.exp(tot_f) * sp_f + jnp.where(bd, upd[:, :n], 0.0)
    sb_ref[...] = jnp.exp(tot_b) * sp_b + jnp.where(bd, upd[:, n:], 0.0)
    return outs


def _gla_kernel(qk_ref, v_ref, g_ref, sm_ref, qkc_ref, vc_ref, gc_ref, smc_ref, wlr_ref, blr_ref, ng_ref,
                *rest, ctx_out):
    if ctx_out:
        o_ref, oc_ref, la_sc, lac_sc, of_sc, ob_sc, ofc_sc, obc_sc, sf_sc, sb_sc = rest
    else:
        o_ref, la_sc, lac_sc, of_sc, ob_sc, sf_sc, sb_sc = rest
        oc_ref = ofc_sc = obc_sc = None
    t = qk_ref.shape[1]
    tc = qkc_ref.shape[1]
    qscale = GLA_DK ** -0.5

    def log_decay(sm):
        z = _dot3(sm, wlr_ref[...]) + blr_ref[...]
        return _log_sigmoid(z) * (1.0 / GLA_NORMALIZER)

    lac_sc[...] = log_decay(smc_ref[0])
    rows_blk = 256

    def la_body(i, carry):
        r0 = pl.multiple_of(i * rows_blk, rows_blk)
        la_sc[pl.ds(r0, rows_blk), :] = log_decay(sm_ref[0, pl.ds(r0, rows_blk), :])
        return carry

    lax.fori_loop(0, t // rows_blk, la_body, 0)
    sf_sc[...] = jnp.zeros_like(sf_sc)
    sb_sc[...] = jnp.zeros_like(sb_sc)

    def run(n_rows, qk, v, la, of, ob, want_out):
        n = n_rows // GLA_BLOCK

        def body(i, carry):
            rf = pl.multiple_of(i * GLA_BLOCK, GLA_BLOCK)
            rb = pl.multiple_of((n - 1 - i) * GLA_BLOCK, GLA_BLOCK)
            blks = []
            for r0, lane0 in ((rf, 0), (rb, GLA_QK_W)):
                qkb = qk[0, pl.ds(r0, GLA_BLOCK), :].astype(F32)
                blks.append((qkb[:, :GLA_QK_W] * qscale, qkb[:, GLA_QK_W:], v[0, pl.ds(r0, GLA_BLOCK), :].astype(F32),
                             la[pl.ds(r0, GLA_BLOCK), lane0:lane0 + GLA_QK_W]))
            out_f, out_b = _gla_step(blks[0], blks[1], sf_sc, sb_sc, want_out)
            if want_out:
                of[pl.ds(rf, GLA_BLOCK), :] = out_f
                ob[pl.ds(rb, GLA_BLOCK), :] = out_b
            return carry

        lax.fori_loop(0, n, body, 0)

    run(tc, qkc_ref, vc_ref, lac_sc, ofc_sc, obc_sc, ctx_out)
    run(t, qk_ref, v_ref, la_sc, of_sc, ob_sc, True)

    grp = jnp.where((_iota((GROUP_W, GROUP_W), 0) // HEAD_DIM) == (_iota((GROUP_W, GROUP_W), 1) // HEAD_DIM),
                    1.0 / HEAD_DIM, 0.0).astype(BF16)

    def finish(n_rows, of, ob, g, dst):
        blk = min(n_rows, 256)

        def body(i, carry):
            r0 = pl.multiple_of(i * blk, blk)
            o = of[pl.ds(r0, blk), :] + ob[pl.ds(r0, blk), :]
            sq_hi, sq_lo, _ = _split3(o * o)
            ms = (jnp.dot(sq_hi, grp, preferred_element_type=F32) + jnp.dot(sq_lo, grp, preferred_element_type=F32))
            res = o * lax.rsqrt(ms + EPS) * ng_ref[...] * _silu(g[0, pl.ds(r0, blk), :].astype(F32))
            dst[0, pl.ds(r0, blk), :] = res.astype(dst.dtype)
            return carry

        lax.fori_loop(0, n_rows // blk, body, 0)

    finish(t, of_sc, ob_sc, g_ref, o_ref)
    if ctx_out:
        finish(tc, ofc_sc, obc_sc, gc_ref, oc_ref)


def _gla(p, ps, pc, pcs, wlr, blr, ng, ctx_out):
    b, t, _ = p.shape
    tc = pc.shape[1]

    def col(width, idx, rows):
        return pl.BlockSpec((1, rows, width), lambda i: (i, 0, idx))

    full = lambda shape: pl.BlockSpec(shape, lambda i: tuple(0 for _ in shape))
    in_specs = [col(GROUP_W, P_GLA_QK, t), col(GROUP_W, P_GLA_V, t), col(GROUP_W, P_GLA_G, t), col(LANES, 0, t),
                col(GROUP_W, P_GLA_QK, tc), col(GROUP_W, P_GLA_V, tc), col(GROUP_W, P_GLA_G, tc),
                col(LANES, 0, tc),
                full(wlr.shape), full(blr.shape), full(ng.shape)]
    out_specs = [pl.BlockSpec((1, t, GROUP_W), lambda i: (i, 0, 0))]
    out_shape = [jax.ShapeDtypeStruct((b, t, GROUP_W), BF16)]
    scratch = [pltpu.VMEM((t, 2 * GLA_QK_W), F32), pltpu.VMEM((tc, 2 * GLA_QK_W), F32),
               pltpu.VMEM((t, GROUP_W), F32), pltpu.VMEM((t, GROUP_W), F32)]
    if ctx_out:
        out_specs.append(pl.BlockSpec((1, tc, GROUP_W), lambda i: (i, 0, 0)))
        out_shape.append(jax.ShapeDtypeStruct((b, tc, GROUP_W), BF16))
        scratch += [pltpu.VMEM((tc, GROUP_W), F32), pltpu.VMEM((tc, GROUP_W), F32)]
    scratch += [pltpu.VMEM((GROUP_W, GLA_QK_W), F32), pltpu.VMEM((GROUP_W, GLA_QK_W), F32)]
    res = pl.pallas_call(
        functools.partial(_gla_kernel, ctx_out=ctx_out),
        grid=(b,),
        in_specs=in_specs, out_specs=out_specs, out_shape=out_shape, scratch_shapes=scratch,
        compiler_params=_cparams(("parallel",)),
        name="gla",
    )(p, p, p, ps, pc, pc, pc, pcs, wlr, blr, ng)
    return (res[0], res[1]) if ctx_out else (res[0], None)


ML_BLOCK = 128
N_PAIRS = N_HEADS // 2


def _gate_lane(d, h):
    return SMALL_GATE_LANE + 8 * d + h


def _mlstm_selectors():
    sel_head = np.zeros((LANES, 2 * N_HEADS * LANES), np.float32)
    sel_pair = np.zeros((LANES, 2 * N_PAIRS * LANES), np.float32)
    for d in range(2):
        for h in range(N_HEADS):
            sel_head[_gate_lane(d, h), (d * N_HEADS + h) * LANES:(d * N_HEADS + h + 1) * LANES] = 1.0
            p, hh = divmod(h, 2)
            lo = (d * N_PAIRS + p) * LANES + hh * HEAD_DIM
            sel_pair[_gate_lane(d, h), lo:lo + HEAD_DIM] = 1.0
    return sel_head, sel_pair


def _mlstm_block(blk_f, blk_b, cn_ref, m_prev, selh_ref, selp_ref, want_out):
    L = ML_BLOCK
    lane = _iota((1, LANES), 1)
    is_b = (lane >= _gate_lane(1, 0)) & (lane < _gate_lane(1, 0) + N_HEADS)
    valid = ((lane >= _gate_lane(0, 0)) & (lane < _gate_lane(0, 0) + N_HEADS)) | is_b
    r, c = _iota((L, L), 0), _iota((L, L), 1)
    keep_f, keep_b = c <= r, c >= r
    lf_f = pltpu.roll(_log_sigmoid(blk_f[2]), LANES - 4, 1)
    lf_b = pltpu.roll(_log_sigmoid(blk_b[2]), LANES - 4, 1)
    cum_f = _tri_dot(jnp.where(keep_f, 1.0, 0.0), lf_f)
    cum_b = _tri_dot(jnp.where(keep_b, 1.0, 0.0), lf_b)
    cum = jnp.where(valid, jnp.where(is_b, cum_b, cum_f), 0.0)
    ig = jnp.where(valid, jnp.where(is_b, blk_b[2], blk_f[2]), 0.0)
    tot = jnp.where(valid, jnp.where(is_b, cum_b[0:1, :], cum_f[L - 1:L, :]), 0.0)
    b = ig - cum
    row = _iota((L, 1), 0)
    pm = b
    k = 1
    while k < L:
        from_prev = jnp.where(row >= k, pltpu.roll(pm, k, 0), -jnp.inf)
        from_next = jnp.where(row < L - k, pltpu.roll(pm, L - k, 0), -jnp.inf)
        pm = jnp.maximum(pm, jnp.where(is_b, from_next, from_prev))
        k *= 2
    mm = jnp.maximum(m_prev, pm)
    w_end = tot + b
    m_chunk = jnp.max(w_end, axis=0, keepdims=True)
    m_new = jnp.maximum(tot + m_prev, m_chunk)
    a_row = jnp.exp(tot + m_prev - m_new)
    g_row = jnp.exp(m_chunk - m_new)
    cols = [jnp.exp(w_end - m_chunk)]
    if want_out:
        cols += [jnp.exp(m_prev - mm), jnp.exp(-(cum + mm))]
        mm_hi, mm_lo, _ = _split3(mm)
        mm_t = jnp.dot(jnp.concatenate([mm_hi, mm_lo], axis=1), jnp.concatenate([selh_ref[...]] * 2, axis=0),
                       preferred_element_type=F32)
        b_rows = b.T
    pair_t = _bdot(jnp.concatenate(cols, axis=0), selp_ref[...])
    lo_half = lane < HEAD_DIM
    ones_t = jnp.ones((L, LANES), F32)
    zeros_2t = jnp.zeros((L, 2 * LANES), BF16)
    lane_h = _iota((1, GROUP_W), 1) // HEAD_DIM
    bd = (_iota((LANES, 2 * LANES), 0) // HEAD_DIM) == ((_iota((LANES, 2 * LANES), 1) % LANES) // HEAD_DIM)
    outs = {}
    for d, (blk, keep) in enumerate(((blk_f, keep_f), (blk_b, keep_b))):
        qk, v, _ = blk
        q_all, k_all = qk[:, :GROUP_W], qk[:, GROUP_W:]
        cn = [cn_ref[d * N_PAIRS + p] for p in range(N_PAIRS)]
        tcol = [(d * N_PAIRS + p) * LANES for p in range(N_PAIRS)]
        if want_out:
            qs = jnp.concatenate([jnp.where(lane_h == h, q_all, 0.0) for h in range(N_HEADS)], axis=0)
            sc = _bdot_nt(qs, k_all)
            s_parts, va_rows = [], []
            for h in range(N_HEADS):
                p, hh = divmod(h, 2)
                mine = lo_half if hh == 0 else jnp.logical_not(lo_half)
                gl = _gate_lane(d, h)
                hcol = (d * N_HEADS + h) * LANES
                e = jnp.where(keep, jnp.exp(b_rows[gl:gl + 1, :] - mm_t[:, hcol:hcol + LANES]), 0.0)
                s_parts.append((e * sc[h * L:(h + 1) * L, :]).astype(BF16))
                vp = v[:, p * LANES:(p + 1) * LANES]
                va = jnp.concatenate([jnp.where(mine, vp, 0.0), jnp.where(mine, ones_t, 0.0)], axis=1).astype(BF16)
                va_rows.append(jnp.concatenate([va, zeros_2t] if p == 0 else [zeros_2t, va], axis=1))
            nd = jnp.dot(jnp.concatenate(s_parts, axis=1), jnp.concatenate(va_rows, axis=0),
                         preferred_element_type=F32)
            cn_bd = jnp.concatenate([jnp.concatenate([cn[0].astype(BF16), zeros_2t], axis=1),
                                     jnp.concatenate([zeros_2t, cn[1].astype(BF16)], axis=1)], axis=0)
            w_int = [pair_t[L:2 * L, tc0:tc0 + LANES] for tc0 in tcol]
            nd = nd + (jnp.concatenate([w_int[0], w_int[0], w_int[1], w_int[1]], axis=1)
                       * jnp.dot(q_all.astype(BF16), cn_bd, preferred_element_type=F32))
            for p in range(N_PAIRS):
                c0 = 2 * p * LANES
                outs[(d, p)] = nd[:, c0:c0 + LANES] / jnp.maximum(jnp.abs(nd[:, c0 + LANES:c0 + 2 * LANES]),
                                                                  pair_t[2 * L:3 * L, tcol[p]:tcol[p] + LANES])
        w_k = jnp.concatenate([pair_t[0:L, tc0:tc0 + LANES] for tc0 in tcol], axis=1)
        kw_t = (k_all * w_k).T
        upd_all = _bdot(kw_t, jnp.concatenate([v, ones_t], axis=1))
        for p in range(N_PAIRS):
            rows = slice(p * LANES, (p + 1) * LANES)
            upd = jnp.concatenate([upd_all[rows, p * LANES:(p + 1) * LANES], upd_all[rows, GROUP_W:]], axis=1)
            ga, gb = _gate_lane(d, 2 * p), _gate_lane(d, 2 * p + 1)
            half = (HEAD_DIM, 2 * LANES)
            a_t = jnp.concatenate([jnp.broadcast_to(a_row[:, ga:ga + 1], half),
                                   jnp.broadcast_to(a_row[:, gb:gb + 1], half)], axis=0)
            g_t = jnp.concatenate([jnp.broadcast_to(g_row[:, ga:ga + 1], half),
                                   jnp.broadcast_to(g_row[:, gb:gb + 1], half)], axis=0)
            cn_ref[d * N_PAIRS + p] = a_t * cn[p] + g_t * jnp.where(bd, upd, 0.0)
    return (outs if want_out else None), m_new


def _mlstm_kernel(qk_ref, v_ref, og_ref, sm_ref, qkc_ref, vc_ref, ogc_ref, smc_ref, cw_ref, gb_ref,
                  selh_ref, selp_ref, *rest, ctx_out):
    if ctx_out:
        o_ref, oc_ref, qc_sc, qcc_sc, hf_sc, hb_sc, hfc_sc, hbc_sc, cn_sc = rest
    else:
        o_ref, qc_sc, qcc_sc, hf_sc, hb_sc, cn_sc = rest
        oc_ref = hfc_sc = hbc_sc = None
    t = qk_ref.shape[1]
    tc = qkc_ref.shape[1]
    w2 = 2 * GROUP_W
    kscale = HEAD_DIM ** -0.5
    n_rows_grid = t // GRID_W

    def tap(dr, dc):
        return cw_ref[dr * 3 + dc:dr * 3 + dc + 1, :]

    def shifted(blk, dc):
        n = blk.shape[0]
        if dc == 1:
            return blk
        ridx = _iota((n, 1), 0)
        if dc == 0:
            return jnp.where(ridx == 0, 0.0, pltpu.roll(blk, 1, 0))
        return jnp.where(ridx == n - 1, 0.0, pltpu.roll(blk, n - 1, 0))

    def finish_qk(acc):
        a = _silu(acc)
        lane = _iota((1, w2), 1)
        return jnp.where(lane >= GROUP_W, a * kscale, a)

    def conv_body(i, carry):
        r0 = pl.multiple_of(i * GRID_W, GRID_W)
        r_up = pl.multiple_of(jnp.maximum(i - 1, 0) * GRID_W, GRID_W)
        r_dn = pl.multiple_of(jnp.minimum(i + 1, n_rows_grid - 1) * GRID_W, GRID_W)
        up = qk_ref[0, pl.ds(r_up, GRID_W), :].astype(F32)
        mid = qk_ref[0, pl.ds(r0, GRID_W), :].astype(F32)
        dn = qk_ref[0, pl.ds(r_dn, GRID_W), :].astype(F32)
        has_up = jnp.where(i > 0, 1.0, 0.0)
        has_dn = jnp.where(i < n_rows_grid - 1, 1.0, 0.0)
        acc = None
        for dc in range(3):
            col = shifted(up * (tap(0, dc) * has_up) + mid * tap(1, dc) + dn * (tap(2, dc) * has_dn), dc)
            acc = col if acc is None else acc + col
        qc_sc[pl.ds(r0, GRID_W), :] = finish_qk(acc)
        return carry

    lax.fori_loop(0, n_rows_grid, conv_body, 0)
    xc = qkc_ref[0].astype(F32)
    acc = jnp.zeros((tc, w2), F32)
    for dc in range(3):
        acc = acc + shifted(xc, dc) * tap(1, dc)
    qcc_sc[...] = finish_qk(acc)

    cn_sc[...] = jnp.zeros_like(cn_sc)

    def run(n_rows, qc, v, sm, hf, hb, want_out, m0):
        n = n_rows // ML_BLOCK

        def body(i, m):
            rf = pl.multiple_of(i * ML_BLOCK, ML_BLOCK)
            rb = pl.multiple_of((n - 1 - i) * ML_BLOCK, ML_BLOCK)
            blks = [(qc[pl.ds(r0, ML_BLOCK), :], v[0, pl.ds(r0, ML_BLOCK), :].astype(F32),
                     sm[0, pl.ds(r0, ML_BLOCK), :] + gb_ref[...]) for r0 in (rf, rb)]
            outs, m_new = _mlstm_block(blks[0], blks[1], cn_sc, m, selh_ref, selp_ref, want_out)
            if want_out:
                for p in range(N_PAIRS):
                    hf[pl.ds(rf, ML_BLOCK), p * LANES:(p + 1) * LANES] = outs[(0, p)]
                    hb[pl.ds(rb, ML_BLOCK), p * LANES:(p + 1) * LANES] = outs[(1, p)]
            return m_new

        return lax.fori_loop(0, n, body, m0)

    m1 = run(tc, qcc_sc, vc_ref, smc_ref, hfc_sc, hbc_sc, ctx_out, jnp.zeros((1, LANES), F32))
    run(t, qc_sc, v_ref, sm_ref, hf_sc, hb_sc, True, m1)

    def finish(n_rows, hf, hb, og, dst):
        blk = min(n_rows, 256)

        def body(i, carry):
            r0 = pl.multiple_of(i * blk, blk)
            res = _sigmoid(og[0, pl.ds(r0, blk), :].astype(F32)) * (hf[pl.ds(r0, blk), :] + hb[pl.ds(r0, blk), :])
            dst[0, pl.ds(r0, blk), :] = res.astype(dst.dtype)
            return carry

        lax.fori_loop(0, n_rows // blk, body, 0)

    finish(t, hf_sc, hb_sc, og_ref, o_ref)
    if ctx_out:
        finish(tc, hfc_sc, hbc_sc, ogc_ref, oc_ref)


def _mlstm(p, ps, pc, pcs, conv_w, gate_b, ctx_out):
    b, t, _ = p.shape
    tc = pc.shape[1]
    sel_head, sel_pair = (jnp.asarray(a).astype(BF16) for a in _mlstm_selectors())

    def col(width, idx, rows):
        return pl.BlockSpec((1, rows, width), lambda i: (i, 0, idx))

    full = lambda shape: pl.BlockSpec(shape, lambda i: tuple(0 for _ in shape))
    in_specs = [col(2 * GROUP_W, P_ML_QK, t), col(GROUP_W, P_ML_V, t), col(GROUP_W, P_ML_OG, t),
                col(LANES, 0, t),
                col(2 * GROUP_W, P_ML_QK, tc), col(GROUP_W, P_ML_V, tc), col(GROUP_W, P_ML_OG, tc),
                col(LANES, 0, tc),
                full(conv_w.shape), full(gate_b.shape), full(sel_head.shape), full(sel_pair.shape)]
    out_specs = [pl.BlockSpec((1, t, GROUP_W), lambda i: (i, 0, 0))]
    out_shape = [jax.ShapeDtypeStruct((b, t, GROUP_W), BF16)]
    scratch = [pltpu.VMEM((t, 2 * GROUP_W), F32), pltpu.VMEM((tc, 2 * GROUP_W), F32),
               pltpu.VMEM((t, GROUP_W), F32), pltpu.VMEM((t, GROUP_W), F32)]
    if ctx_out:
        out_specs.append(pl.BlockSpec((1, tc, GROUP_W), lambda i: (i, 0, 0)))
        out_shape.append(jax.ShapeDtypeStruct((b, tc, GROUP_W), BF16))
        scratch += [pltpu.VMEM((tc, GROUP_W), F32), pltpu.VMEM((tc, GROUP_W), F32)]
    scratch += [pltpu.VMEM((2 * N_PAIRS, LANES, 2 * LANES), F32)]
    res = pl.pallas_call(
        functools.partial(_mlstm_kernel, ctx_out=ctx_out),
        grid=(b,),
        in_specs=in_specs, out_specs=out_specs, out_shape=out_shape, scratch_shapes=scratch,
        compiler_params=_cparams(("parallel",)),
        name="mlstm",
    )(p, p, p, ps, pc, pc, pc, pcs, conv_w, gate_b, sel_head, sel_pair)
    return (res[0], res[1]) if ctx_out else (res[0], None)


def _sgu_kernel(u_ref, v_ref, f_ref, ws_ref, bias_ref, dft_ref, o_ref, z_ref):
    t = u_ref.shape[1]
    lane_g = _iota((1, GROUP_W), 1) // HEAD_DIM

    def body(i, carry):
        r0 = pl.multiple_of(i * SGU_CHUNK, SGU_CHUNK)
        v = v_ref[0, pl.ds(r0, SGU_CHUNK), :].astype(F32)
        mu = jnp.mean(v, axis=-1, keepdims=True)
        vc = v - mu
        vn = vc * lax.rsqrt(jnp.mean(vc * vc, axis=-1, keepdims=True) + EPS)
        mixed = bias_ref[...]
        for g in range(N_HEADS):
            mixed = mixed + _bdot(ws_ref[g], jnp.where(lane_g == g, vn, 0.0))
        o_ref[0, pl.ds(r0, SGU_CHUNK), :] = (u_ref[0, pl.ds(r0, SGU_CHUNK), :].astype(F32) * mixed).astype(o_ref.dtype)
        z = _bdot(f_ref[0, pl.ds(r0, SGU_CHUNK), :], dft_ref[...])
        z_ref[0, 0, pl.ds(r0, SGU_CHUNK), :] = z[:, :GROUP_W].astype(z_ref.dtype)
        z_ref[1, 0, pl.ds(r0, SGU_CHUNK), :] = z[:, GROUP_W:].astype(z_ref.dtype)
        return carry

    lax.fori_loop(0, t // SGU_CHUNK, body, 0)


def _sgu_fourier1(p, ws, bias, dft_c):
    b, t, _ = p.shape

    def col(idx):
        return pl.BlockSpec((1, t, GROUP_W), lambda i: (i, 0, idx))

    full = lambda shape: pl.BlockSpec(shape, lambda i: tuple(0 for _ in shape))
    return pl.pallas_call(
        _sgu_kernel,
        grid=(b,),
        in_specs=[col(P_SGU_U), col(P_SGU_V), col(P_FOUR), full(ws.shape), full(bias.shape), full(dft_c.shape)],
        out_specs=[pl.BlockSpec((1, t, GROUP_W), lambda i: (i, 0, 0)),
                   pl.BlockSpec((2, 1, t, GROUP_W), lambda i: (0, i, 0, 0))],
        out_shape=[jax.ShapeDtypeStruct((b, t, GROUP_W), BF16), jax.ShapeDtypeStruct((2, b, t, GROUP_W), BF16)],
        compiler_params=_cparams(("parallel",)),
        name="sgu_fourier1",
    )(p, p, p, ws, bias, dft_c)


def _fourier2_kernel(ct_ref, st_ref, z_ref, o_ref, *, scale):
    for i in range(z_ref.shape[1]):
        acc = jnp.dot(ct_ref[...], z_ref[0, i], preferred_element_type=F32)
        acc -= jnp.dot(st_ref[...], z_ref[1, i], preferred_element_type=F32)
        o_ref[i] = (acc * scale).astype(o_ref.dtype)


def _fourier2(z, cos_t, sin_t):
    _, b, t, w = z.shape
    tm = min(t, 256)
    return pl.pallas_call(
        functools.partial(_fourier2_kernel, scale=float((t * HEAD_DIM) ** -0.5)),
        grid=(t // tm,),
        in_specs=[pl.BlockSpec((tm, t), lambda i: (i, 0)),
                  pl.BlockSpec((tm, t), lambda i: (i, 0)),
                  pl.BlockSpec((2, b, t, w), lambda i: (0, 0, 0, 0), pipeline_mode=pl.Buffered(1))],
        out_specs=pl.BlockSpec((b, tm, w), lambda i: (0, i, 0)),
        out_shape=jax.ShapeDtypeStruct((b, t, w), BF16),
        compiler_params=_cparams(("parallel",)),
        name="fourier2",
    )(cos_t, sin_t, z)


def _dft_tables(n):
    k = np.arange(n, dtype=np.int64)
    ang = 2.0 * np.pi * ((k[:, None] * k[None, :]) % n).astype(np.float64) / n
    return np.cos(ang).astype(np.float32), np.sin(ang).astype(np.float32)


def _channel_dft():
    c, s = _dft_tables(HEAD_DIM)
    eye = np.eye(N_HEADS, dtype=np.float32)
    return np.concatenate([np.kron(eye, c), np.kron(eye, s)], axis=1)


def _route_kernel(x_ref, mod_ref, g_ref, wr_ref, h_ref, mi_ref, mf_ref, cnt_ref, cnt_sc):
    first = (pl.program_id(0) == 0) & (pl.program_id(1) == 0)

    @pl.when(first)
    def _():
        cnt_sc[...] = jnp.zeros_like(cnt_sc)

    tm = x_ref.shape[1]
    h = _norm_mod(x_ref[0], g_ref[...], mod_ref[0, 3:4, :], mod_ref[0, 4:5, :])
    for j in range(D_MODEL // LANES):
        h_ref[pl.ds(j, tm, stride=SUBLANES), :] = h[:, j * LANES:(j + 1) * LANES]
    logits = _fdot(h, wr_ref[...])
    lane = _iota((tm, LANES), 1).astype(F32)
    lg = jnp.where(lane < N_EXPERTS, logits, -jnp.inf)
    m1 = jnp.max(lg, axis=1, keepdims=True)
    i1 = jnp.min(jnp.where(lg == m1, lane, float(LANES)), axis=1, keepdims=True)
    lg2 = jnp.where(lane == i1, -jnp.inf, lg)
    m2 = jnp.max(lg2, axis=1, keepdims=True)
    i2 = jnp.min(jnp.where(lg2 == m2, lane, float(LANES)), axis=1, keepdims=True)
    e = jnp.exp(m2 - m1)
    g0 = 1.0 / (1.0 + e)
    g1 = e / (1.0 + e)
    onehot = jnp.where((lane == i1) | (lane == i2), 1.0, 0.0)
    strict = jnp.where(_iota((tm, tm), 1) < _iota((tm, tm), 0), 1.0, 0.0)
    before = _bdot(strict, onehot) + cnt_sc[...]
    r0 = jnp.sum(jnp.where(lane == i1, before, 0.0), axis=1, keepdims=True)
    r1 = jnp.sum(jnp.where(lane == i2, before, 0.0), axis=1, keepdims=True)
    cnt_sc[...] += jnp.sum(onehot, axis=0, keepdims=True)
    cnt_ref[...] = cnt_sc[...]
    l8 = _iota((tm, SUBLANES), 1)
    mi = jnp.where(l8 == 0, i1, jnp.where(l8 == 1, i2, jnp.where(l8 == 2, r0, r1)))
    mi_ref[...] = mi.astype(jnp.int32)
    mf_ref[...] = jnp.where(l8 == 0, g0, g1)


def _route(x, mod, g, w_router):
    b, t, d = x.shape
    m = b * t
    tm = 512
    nt = t // tm
    wr = jnp.zeros((d, LANES), F32).at[:, :N_EXPERTS].set(w_router)
    return pl.pallas_call(
        _route_kernel,
        grid=(b, nt),
        in_specs=[pl.BlockSpec((1, tm, d), lambda i, j: (i, j, 0)),
                  pl.BlockSpec((1, 6, d), lambda i, j: (i, 0, 0)),
                  pl.BlockSpec((1, d), lambda i, j: (0, 0)),
                  pl.BlockSpec((d, LANES), lambda i, j: (0, 0))],
        out_specs=[pl.BlockSpec((tm * SUBLANES, LANES), lambda i, j: (i * nt + j, 0)),
                   pl.BlockSpec((tm, SUBLANES), lambda i, j: (i * nt + j, 0)),
                   pl.BlockSpec((tm, SUBLANES), lambda i, j: (i * nt + j, 0)),
                   pl.BlockSpec((1, LANES), lambda i, j: (0, 0))],
        out_shape=[jax.ShapeDtypeStruct((m * SUBLANES, LANES), F32),
                   jax.ShapeDtypeStruct((m, SUBLANES), jnp.int32),
                   jax.ShapeDtypeStruct((m, SUBLANES), F32),
                   jax.ShapeDtypeStruct((1, LANES), F32)],
        scratch_shapes=[pltpu.VMEM((1, LANES), F32)],
        compiler_params=_cparams(("arbitrary", "arbitrary")),
        name="route",
    )(x, mod, g.reshape(1, d), wr)


def _moe_kernel(texp_ref, trows_ref, pos0_ref, pos1_ref, h_hbm, wg_ref, wu_ref, wd_ref, y_hbm,
                order_sm, xbuf, x_sc, acc_sc, ybuf, gsem, ssem):
    i = pl.program_id(0)
    j = pl.program_id(1)
    nf = pl.num_programs(1)
    tm = MOE_TM
    n_tok = pos0_ref.shape[0]
    rows = trows_ref[i]

    nt = pl.num_programs(0)
    buf_rows = tm * SUBLANES
    dump = TOP_K * n_tok
    slot = lax.rem(i, 2)
    prev_rows = trows_ref[jnp.maximum(i - 1, 0)]
    next_rows = trows_ref[jnp.minimum(i + 1, nt - 1)]
    n_full, n_rest = divmod(tm, MOE_COPY_ROWS)

    def gather_copy(tile, n_real, r, dst_slot):
        idx = jnp.where(n_real > 0, tile * tm + jnp.minimum(r, n_real - 1), 0)
        tok = jnp.right_shift(order_sm[idx], 1)
        off = pl.multiple_of(dst_slot * buf_rows + r * SUBLANES, SUBLANES)
        return pltpu.make_async_copy(h_hbm.at[tok], xbuf.at[pl.ds(off, SUBLANES), :], gsem)

    def scatter_copy(tile, n_real, r, src_slot):
        idx = jnp.where(n_real > 0, tile * tm + jnp.minimum(r, n_real - 1), 0)
        dst = jnp.where(r < n_real, order_sm[idx], dump + r)
        off = pl.multiple_of(src_slot * buf_rows + r * SUBLANES, SUBLANES)
        return pltpu.make_async_copy(ybuf.at[pl.ds(off, SUBLANES), :], y_hbm.at[dst], ssem)

    def wait_all(buf, sem):
        pltpu.make_async_copy(buf.at[pl.ds(0, buf_rows), :], buf.at[pl.ds(buf_rows, buf_rows), :], sem).wait()

    @pl.when((i == 0) & (j == 0))
    def _():
        def inv_body(t, carry):
            order_sm[pos0_ref[t]] = 2 * t
            order_sm[pos1_ref[t]] = 2 * t + 1
            return carry

        lax.fori_loop(0, n_tok, inv_body, 0, unroll=8)
        ybuf[...] = jnp.zeros_like(ybuf)

        def start(r, carry):
            gather_copy(0, rows, r, 0).start()
            return carry

        lax.fori_loop(0, tm, start, 0)

    @pl.when((j == 0) & (rows > 0))
    def _():
        wait_all(xbuf, gsem)
        base = pl.multiple_of(slot * buf_rows, SUBLANES)
        for c in range(D_MODEL // LANES):
            x_sc[:, c * LANES:(c + 1) * LANES] = xbuf[pl.ds(base + c, tm, stride=SUBLANES), :].astype(BF16)
        acc_sc[...] = jnp.zeros_like(acc_sc)

    def step(n_copies):
        r0 = j * MOE_COPY_ROWS
        for q in range(n_copies):
            gather_copy(i + 1, next_rows, r0 + q, 1 - slot).start()
            scatter_copy(i - 1, jnp.where(i > 0, prev_rows, 0), r0 + q, 1 - slot).start()
        x = x_sc[...]
        gate = jnp.dot(x, wg_ref[0].astype(BF16), preferred_element_type=F32)
        up = jnp.dot(x, wu_ref[0].astype(BF16), preferred_element_type=F32)
        acc_sc[...] += jnp.dot((_silu(gate) * up).astype(BF16), wd_ref[0].astype(BF16),
                               preferred_element_type=F32)

    pl.when((rows > 0) & (j < n_full))(functools.partial(step, MOE_COPY_ROWS))
    pl.when((rows > 0) & (j == n_full))(functools.partial(step, n_rest))
    pl.when((rows > 0) & (j > n_full))(functools.partial(step, 0))

    @pl.when((j == nf - 1) & (rows > 0))
    def _():
        wait_all(ybuf, ssem)
        base = pl.multiple_of(slot * buf_rows, SUBLANES)
        for c in range(D_MODEL // LANES):
            ybuf[pl.ds(base + c, tm, stride=SUBLANES), :] = acc_sc[:, c * LANES:(c + 1) * LANES]

    @pl.when((j == 0) & (rows == 0) & (i > 0) & (prev_rows > 0))
    def _():
        wait_all(xbuf, gsem)

        def start(r, carry):
            scatter_copy(i - 1, prev_rows, r, 1 - slot).start()
            return carry

        lax.fori_loop(0, tm, start, 0)
        wait_all(ybuf, ssem)


def _moe(h_tiles, pos0, pos1, tile_expert, tile_rows, wg, wu, wd):
    m = h_tiles.shape[0]
    n_exp, d, f = wg.shape
    nt = tile_expert.shape[0]
    tm, tf = MOE_TM, MOE_TF
    nf = f // tf
    assert nf * MOE_COPY_ROWS >= tm, "each tile needs enough column steps to issue its neighbours' row copies"
    assert nt * tm > TOP_K * m + n_exp * (tm - 1), "the last tile must be unused: it drains the copy pipeline"

    def col_tile(i, j, tr):
        return jnp.where(tr[i] > 0, j, nf - 1)

    grid_spec = pltpu.PrefetchScalarGridSpec(
        num_scalar_prefetch=4,
        grid=(nt, nf),
        in_specs=[pl.BlockSpec(memory_space=pl.ANY),
                  pl.BlockSpec((1, d, tf), lambda i, j, te, tr, p0, p1: (te[i], 0, col_tile(i, j, tr))),
                  pl.BlockSpec((1, d, tf), lambda i, j, te, tr, p0, p1: (te[i], 0, col_tile(i, j, tr))),
                  pl.BlockSpec((1, tf, d), lambda i, j, te, tr, p0, p1: (te[i], col_tile(i, j, tr), 0))],
        out_specs=pl.BlockSpec(memory_space=pl.ANY),
        scratch_shapes=[pltpu.SMEM((nt * tm,), jnp.int32),
                        pltpu.VMEM((2 * tm * SUBLANES, LANES), F32),
                        pltpu.VMEM((tm, d), BF16),
                        pltpu.VMEM((tm, d), F32),
                        pltpu.VMEM((2 * tm * SUBLANES, LANES), F32),
                        pltpu.SemaphoreType.DMA(()),
                        pltpu.SemaphoreType.DMA(())],
    )
    return pl.pallas_call(
        _moe_kernel,
        grid_spec=grid_spec,
        out_shape=jax.ShapeDtypeStruct((TOP_K * m + tm, SUBLANES, LANES), F32),
        compiler_params=_cparams(("arbitrary", "arbitrary")),
        name="moe",
    )(tile_expert, tile_rows, pos0, pos1, h_tiles, wg, wu, wd)


def _combine_kernel(x_ref, mod_ref, y_ref, mf_ref, fg_ref, o_ref):
    tm = x_ref.shape[1]
    stride = TOP_K * SUBLANES
    g0 = mf_ref[:, 0:1]
    g1 = mf_ref[:, 1:2]
    parts = []
    for c in range(D_MODEL // LANES):
        y0 = y_ref[pl.ds(c, tm, stride=stride), :]
        y1 = y_ref[pl.ds(SUBLANES + c, tm, stride=stride), :]
        parts.append(g0 * y0 + g1 * y1)
    moe = jnp.concatenate(parts, axis=1)
    x = x_ref[0] + mod_ref[0, 5:6, :] * moe
    ms = jnp.mean(x * x, axis=-1, keepdims=True)
    o_ref[0] = x * lax.rsqrt(ms + EPS) * fg_ref[...]


def _combine(x, mod, y, mf, final_g):
    b, t, d = x.shape
    tm = 256
    nt = t // tm
    rows = tm * TOP_K * SUBLANES
    return pl.pallas_call(
        _combine_kernel,
        grid=(b, nt),
        in_specs=[pl.BlockSpec((1, tm, d), lambda i, j: (i, j, 0)),
                  pl.BlockSpec((1, 6, d), lambda i, j: (i, 0, 0)),
                  pl.BlockSpec((rows, LANES), lambda i, j: (i * nt + j, 0)),
                  pl.BlockSpec((tm, SUBLANES), lambda i, j: (i * nt + j, 0)),
                  pl.BlockSpec((1, d), lambda i, j: (0, 0))],
        out_specs=pl.BlockSpec((1, tm, d), lambda i, j: (i, j, 0)),
        out_shape=jax.ShapeDtypeStruct((b, t, d), F32),
        compiler_params=_cparams(("parallel", "parallel")),
        name="combine",
    )(x, mod, y, mf, final_g.reshape(1, d))


def _moe_layer(x, mod, norm_g, w_router, wg, wu, wd, final_g):
    b, t, d = x.shape
    m = b * t
    h_tiles, mi, mf, cnt = _route(x, mod, norm_g, w_router)
    counts = cnt[0, :N_EXPERTS].astype(jnp.int32)
    tiles_per = (counts + MOE_TM - 1) // MOE_TM
    tile_end = jnp.cumsum(tiles_per)
    tile_start = tile_end - tiles_per
    seg_start = tile_start * MOE_TM
    nt = (TOP_K * m) // MOE_TM + N_EXPERTS
    tile_id = jnp.arange(nt, dtype=jnp.int32)
    used_id = jnp.minimum(tile_id, tile_end[-1] - 1)
    tile_expert = jnp.sum(used_id[:, None] >= tile_end[None, :], axis=1).astype(jnp.int32)
    tile_rows = jnp.clip(counts[tile_expert] - (tile_id - tile_start[tile_expert]) * MOE_TM, 0, MOE_TM)
    tile_rows = jnp.where(tile_id < tile_end[-1], tile_rows, 0).astype(jnp.int32)
    pos0 = seg_start[mi[:, 0]] + mi[:, 2]
    pos1 = seg_start[mi[:, 1]] + mi[:, 3]
    y = _moe(h_tiles.reshape(m, SUBLANES, LANES), pos0, pos1, tile_expert, tile_rows, wg, wu, wd)
    return _combine(x, mod, y.reshape(-1, LANES), mf, final_g)


def _permute_w_in(w):
    d = w.shape[0]
    return jnp.concatenate([w[:, 0:768], w[:, 1312:1568], w[:, 800:1312], w[:, 1568:1824], w[:, 1840:2608],
                            w[:, 768:800], w[:, 1824:1840], jnp.zeros((d, P_WIDTH - 2608), w.dtype)],
                           axis=1).astype(BF16)


def _mixers(proj, proj_c, gla_w_lr, gla_b_lr, gla_norm_g, mlstm_conv, mlstm_gate_b, sgu_w, sgu_b, ctx_out, tables):
    (p, ps), (pc, pcs) = proj, proj_c
    wlr = jnp.zeros((LANES, 2 * GLA_QK_W), F32)
    wlr = wlr.at[0:GLA_RANK, 0:GLA_QK_W].set(gla_w_lr[0]).at[GLA_RANK:2 * GLA_RANK, GLA_QK_W:].set(gla_w_lr[1])
    blr = gla_b_lr.reshape(1, 2 * GLA_QK_W)
    gla, gla_c = _gla(p, ps, pc, pcs, wlr, blr, gla_norm_g.reshape(1, GROUP_W), ctx_out)
    conv_w = mlstm_conv.reshape(9, 2 * GROUP_W)
    gate_b = jnp.zeros((1, LANES), F32).at[0, SMALL_GATE_LANE:SMALL_GATE_LANE + 4 * N_HEADS].set(
        mlstm_gate_b.reshape(-1))
    ml, ml_c = _mlstm(p, ps, pc, pcs, conv_w, gate_b, ctx_out)
    bias = jnp.repeat(sgu_b.T, HEAD_DIM, axis=1)
    dft_c, tabs = tables
    sg, z = _sgu_fourier1(p, sgu_w, bias, dft_c)
    fo = _fourier2(z, *tabs[p.shape[1]])
    mix = (gla, ml, sg, fo)
    if not ctx_out:
        return mix, None
    sg_c, z_c = _sgu_fourier1(pc, sgu_w, bias, dft_c)
    fo_c = _fourier2(z_c, *tabs[pc.shape[1]])
    return mix, (gla_c, ml_c, sg_c, fo_c)


def kernel(x, c, ctx, c_ctx, w_ada, b_ada, norm_mix_g, norm_ffn_g, w_in, w_out, gla_w_lr, gla_b_lr, gla_norm_g,
           mlstm_conv, mlstm_gate_b, sgu_w, sgu_b, ffn_w_gate, ffn_w_up, ffn_w_down, moe_router, moe_w_gate,
           moe_w_up, moe_w_down, final_norm_g):
    depth = w_ada.shape[0]
    assert depth == 2, "layer 0 is the dense layer with context output, layer 1 the MoE layer"
    b, t, d = x.shape
    tc = ctx.shape[1]
    cvec = jnp.zeros((16, d), F32).at[:b].set(c).at[b].set(c_ctx)
    mods = _ada(cvec, w_ada, b_ada).reshape(depth, 16, 6, d)
    tables = (jnp.asarray(_channel_dft()).astype(BF16),
              {n: tuple(jnp.asarray(a).astype(BF16) for a in _dft_tables(n)) for n in {t, tc}})
    xc = ctx
    for l in range(depth):
        ctx_out = l < depth - 1
        mod, mod_c = mods[l, :b], mods[l, b:b + 1]
        w_in_l = _permute_w_in(w_in[l])
        w_out_l = w_out[l].astype(BF16)
        p = _in_proj(x, mod, norm_mix_g[l], w_in_l)
        pc = _in_proj(xc, mod_c, norm_mix_g[l], w_in_l)
        mix, mix_c = _mixers(p, pc, gla_w_lr[l], gla_b_lr[l], gla_norm_g[l], mlstm_conv[l], mlstm_gate_b[l],
                             sgu_w[l], sgu_b[l], ctx_out, tables)
        x = _out_proj(x, mod, mix, w_out_l)
        if ctx_out:
            xc = _out_proj(xc, mod_c, mix_c, w_out_l)
        i = l // 2
        if l % 2 == 0:
            wg, wu, wd = ffn_w_gate[i].astype(BF16), ffn_w_up[i].astype(BF16), ffn_w_down[i].astype(BF16)
            x = _ffn(x, mod, norm_ffn_g[l], wg, wu, wd)
            if ctx_out:
                xc = _ffn(xc, mod_c, norm_ffn_g[l], wg, wu, wd)
        else:
            x = _moe_layer(x, mod, norm_ffn_g[l], moe_router[i], moe_w_gate[i], moe_w_up[i], moe_w_down[i],
                           final_norm_g)
    return x
```

```python
import functools

import numpy as np
import jax
import jax.numpy as jnp
from jax import lax
from jax.experimental import pallas as pl
from jax.experimental.pallas import tpu as pltpu

F32 = jnp.float32
BF16 = jnp.bfloat16
HIGHEST = lax.Precision.HIGHEST

D_MODEL = 1024
GROUP_W = 256
N_HEADS = 4
HEAD_DIM = 64
GLA_DK = 32
GLA_QK_W = N_HEADS * GLA_DK
GLA_RANK = 16
GLA_NORMALIZER = 16.0
GLA_BLOCK = 128
GRID_W = 64
SGU_CHUNK = 128
N_EXPERTS = 8
TOP_K = 2
EPS = 1e-6

LANES = 128
SUBLANES = 8
VMEM_LIMIT = 52 * 1024 * 1024

P_GLA_QK, P_GLA_V, P_GLA_G, P_ML_V = 0, 1, 2, 3
P_ML_QK = 2
P_ML_OG, P_SGU_U, P_SGU_V, P_FOUR = 6, 7, 8, 9
P_MAIN_W = 2560
P_WIDTH = 2688
SMALL_GATE_LANE = 32

MOE_TM = 1024
MOE_TF = 512
MOE_COPY_ROWS = 171


def _cparams(sem, **kw):
    return pltpu.CompilerParams(dimension_semantics=sem, vmem_limit_bytes=VMEM_LIMIT, **kw)


def _sigmoid(x):
    return 1.0 / (1.0 + jnp.exp(-x))


def _silu(x):
    return x * _sigmoid(x)


def _log_sigmoid(x):
    return jnp.minimum(x, 0.0) - jnp.log(1.0 + jnp.exp(-jnp.abs(x)))


def _norm_mod(x, g, shift, scale):
    ms = jnp.mean(x * x, axis=-1, keepdims=True)
    return (x * lax.rsqrt(ms + EPS) * g) * (1.0 + scale) + shift


def _bdot(a, b):
    return jnp.dot(a.astype(BF16), b.astype(BF16), preferred_element_type=F32)


def _bdot_nt(a, b):
    return lax.dot_general(a.astype(BF16), b.astype(BF16), (((1,), (1,)), ((), ())), preferred_element_type=F32)


def _fdot(a, b):
    return jnp.dot(a, b, precision=HIGHEST, preferred_element_type=F32)


def _split3(x):
    hi = x.astype(BF16)
    r = x - hi.astype(F32)
    mid = r.astype(BF16)
    lo = (r - mid.astype(F32)).astype(BF16)
    return hi, mid, lo


def _dot3(a, b):
    a_hi, a_lo, _ = _split3(a)
    b_hi, b_lo, _ = _split3(b)
    dot = functools.partial(jnp.dot, preferred_element_type=F32)
    return dot(a_hi, b_hi) + dot(a_lo, b_hi) + dot(a_hi, b_lo)


def _tri_dot(tri, x):
    n = x.shape[1]
    y = jnp.dot(tri.astype(BF16), jnp.concatenate(_split3(x), axis=1), preferred_element_type=F32)
    return y[:, :n] + y[:, n:2 * n] + y[:, 2 * n:]


def _ada_kernel(c_ref, w_ref, b_ref, o_ref):
    o_ref[0] = _fdot(_silu(c_ref[...]), w_ref[0]) + b_ref[0]


def _ada(cvec, w_ada, b_ada):
    depth, d, n = w_ada.shape
    rows = cvec.shape[0]
    tn = 1536
    return pl.pallas_call(
        _ada_kernel,
        grid=(depth, n // tn),
        in_specs=[pl.BlockSpec((rows, d), lambda l, j: (0, 0)),
                  pl.BlockSpec((1, d, tn), lambda l, j: (l, 0, j)),
                  pl.BlockSpec((1, 1, tn), lambda l, j: (l, 0, j))],
        out_specs=pl.BlockSpec((1, rows, tn), lambda l, j: (l, 0, j)),
        out_shape=jax.ShapeDtypeStruct((depth, rows, n), F32),
        compiler_params=_cparams(("parallel", "parallel")),
        name="ada",
    )(cvec, w_ada, b_ada.reshape(depth, 1, n))


def _in_kernel(x_ref, mod_ref, g_ref, w_ref, o_ref, os_ref):
    h = _norm_mod(x_ref[0], g_ref[...], mod_ref[0, 0:1, :], mod_ref[0, 1:2, :])
    acc = jnp.dot(h.astype(BF16), w_ref[...], preferred_element_type=F32)
    o_ref[0] = acc[:, :P_MAIN_W].astype(o_ref.dtype)
    os_ref[0] = acc[:, P_MAIN_W:]


def _in_proj(x, mod, g, w):
    b, t, d = x.shape
    n = w.shape[1]
    tm = min(t, 512)
    mod_map = (lambda i, j: (i, 0, 0)) if mod.shape[0] == b else (lambda i, j: (0, 0, 0))
    return pl.pallas_call(
        _in_kernel,
        grid=(b, t // tm),
        in_specs=[pl.BlockSpec((1, tm, d), lambda i, j: (i, j, 0)),
                  pl.BlockSpec((1, 6, d), mod_map),
                  pl.BlockSpec((1, d), lambda i, j: (0, 0)),
                  pl.BlockSpec((d, n), lambda i, j: (0, 0))],
        out_specs=[pl.BlockSpec((1, tm, P_MAIN_W), lambda i, j: (i, j, 0)),
                   pl.BlockSpec((1, tm, n - P_MAIN_W), lambda i, j: (i, j, 0))],
        out_shape=[jax.ShapeDtypeStruct((b, t, P_MAIN_W), BF16), jax.ShapeDtypeStruct((b, t, n - P_MAIN_W), F32)],
        compiler_params=_cparams(("parallel", "parallel")),
        name="in_proj",
    )(x, mod, g.reshape(1, d), w)


def _out_kernel(x_ref, mod_ref, a_ref, b_ref, c_ref, d_ref, w_ref, o_ref):
    acc = jnp.dot(a_ref[0], w_ref[0:GROUP_W, :], preferred_element_type=F32)
    acc += jnp.dot(b_ref[0], w_ref[GROUP_W:2 * GROUP_W, :], preferred_element_type=F32)
    acc += jnp.dot(c_ref[0], w_ref[2 * GROUP_W:3 * GROUP_W, :], preferred_element_type=F32)
    acc += jnp.dot(d_ref[0], w_ref[3 * GROUP_W:4 * GROUP_W, :], preferred_element_type=F32)
    o_ref[0] = x_ref[0] + mod_ref[0, 2:3, :] * acc


def _out_proj(x, mod, mixes, w):
    b, t, d = x.shape
    tm = min(t, 512)
    mod_map = (lambda i, j: (i, 0, 0)) if mod.shape[0] == b else (lambda i, j: (0, 0, 0))
    mix_spec = pl.BlockSpec((1, tm, GROUP_W), lambda i, j: (i, j, 0))
    return pl.pallas_call(
        _out_kernel,
        grid=(b, t // tm),
        in_specs=[pl.BlockSpec((1, tm, d), lambda i, j: (i, j, 0)),
                  pl.BlockSpec((1, 6, d), mod_map),
                  mix_spec, mix_spec, mix_spec, mix_spec,
                  pl.BlockSpec((d, d), lambda i, j: (0, 0))],
        out_specs=pl.BlockSpec((1, tm, d), lambda i, j: (i, j, 0)),
        out_shape=jax.ShapeDtypeStruct((b, t, d), F32),
        compiler_params=_cparams(("parallel", "parallel")),
        name="out_proj",
    )(x, mod, *mixes, w)


def _ffn_kernel(x_ref, mod_ref, g_ref, wg_ref, wu_ref, wd_ref, o_ref, h_sc, acc_sc):
    j = pl.program_id(2)

    @pl.when(j == 0)
    def _():
        h = _norm_mod(x_ref[0], g_ref[...], mod_ref[0, 3:4, :], mod_ref[0, 4:5, :])
        h_sc[...] = h.astype(BF16)
        acc_sc[...] = jnp.zeros_like(acc_sc)

    h = h_sc[...]
    gate = jnp.dot(h, wg_ref[...], preferred_element_type=F32)
    up = jnp.dot(h, wu_ref[...], preferred_element_type=F32)
    acc_sc[...] += jnp.dot((_silu(gate) * up).astype(BF16), wd_ref[...], preferred_element_type=F32)

    @pl.when(j == pl.num_programs(2) - 1)
    def _():
        o_ref[0] = x_ref[0] + mod_ref[0, 5:6, :] * acc_sc[...]


def _ffn(x, mod, g, wg, wu, wd):
    b, t, d = x.shape
    f = wg.shape[1]
    tm = min(t, 512)
    tf = f // 2
    mod_map = (lambda i, j, k: (i, 0, 0)) if mod.shape[0] == b else (lambda i, j, k: (0, 0, 0))
    return pl.pallas_call(
        _ffn_kernel,
        grid=(b, t // tm, f // tf),
        in_specs=[pl.BlockSpec((1, tm, d), lambda i, j, k: (i, j, 0)),
                  pl.BlockSpec((1, 6, d), mod_map),
                  pl.BlockSpec((1, d), lambda i, j, k: (0, 0)),
                  pl.BlockSpec((d, tf), lambda i, j, k: (0, k)),
                  pl.BlockSpec((d, tf), lambda i, j, k: (0, k)),
                  pl.BlockSpec((tf, d), lambda i, j, k: (k, 0))],
        out_specs=pl.BlockSpec((1, tm, d), lambda i, j, k: (i, j, 0)),
        out_shape=jax.ShapeDtypeStruct((b, t, d), F32),
        scratch_shapes=[pltpu.VMEM((tm, d), BF16), pltpu.VMEM((tm, d), F32)],
        compiler_params=_cparams(("parallel", "parallel", "arbitrary")),
        name="ffn",
    )(x, mod, g.reshape(1, d), wg, wu, wd)


def _iota(shape, dim):
    return lax.broadcasted_iota(jnp.int32, shape, dim)


def _gla_step(fwd, bwd, sf_ref, sb_ref, want_out):
    (q_f, k_f, v_f, la_f), (q_b, k_b, v_b, la_b) = fwd, bwd
    L, n = q_f.shape
    r, c = _iota((L, L), 0), _iota((L, L), 1)
    keep_f, keep_b = c <= r, c >= r
    z3 = jnp.zeros((L, 3 * n), BF16)
    tri2 = jnp.concatenate([jnp.where(keep_f, 1.0, 0.0), jnp.where(keep_b, 1.0, 0.0)], axis=1).astype(BF16)
    terms = jnp.concatenate([jnp.concatenate(list(_split3(la_f)) + [z3], axis=1),
                             jnp.concatenate([z3] + list(_split3(la_b)), axis=1)], axis=0)
    y = jnp.dot(tri2, terms, preferred_element_type=F32)
    cum_f = y[:, :n] + y[:, n:2 * n] + y[:, 2 * n:3 * n]
    cum_b = y[:, 3 * n:4 * n] + y[:, 4 * n:5 * n] + y[:, 5 * n:]
    tot_f = jnp.sum(la_f, axis=0, keepdims=True)
    tot_b = jnp.sum(la_b, axis=0, keepdims=True)
    sp_f, sp_b = sf_ref[...], sb_ref[...]
    zq = jnp.zeros((L, n), F32)
    outs = (None, None)
    if want_out:
        ref_f = cum_f[L // 2 - 1:L // 2, :]
        ref_b = cum_b[L // 2:L // 2 + 1, :]
        lane_q = _iota((1, n), 1) // GLA_DK
        lane_v = _iota((1, GROUP_W), 1) // HEAD_DIM

        def stack_q(qr, first):
            rows = [jnp.where(lane_q == h, qr, 0.0) for h in range(N_HEADS)]
            return jnp.concatenate([jnp.concatenate([m, zq] if first else [zq, m], axis=1) for m in rows], axis=0)

        qs = jnp.concatenate([stack_q(q_f * jnp.exp(cum_f - ref_f), True),
                              stack_q(q_b * jnp.exp(cum_b - ref_b), False)], axis=0)
        kc = jnp.concatenate([k_f * jnp.exp(ref_f - cum_f), k_b * jnp.exp(ref_b - cum_b)], axis=1)
        sc = _bdot_nt(qs, kc)

        def heads_along_lanes(part, keep):
            return jnp.concatenate([jnp.where(keep, part[h * L:(h + 1) * L, :], 0.0).astype(BF16)
                                    for h in range(N_HEADS)], axis=1)

        def heads_along_rows(v):
            return jnp.concatenate([jnp.where(lane_v == h, v, 0.0).astype(BF16) for h in range(N_HEADS)], axis=0)

        z4 = jnp.zeros((L, N_HEADS * L), BF16)
        lhs = jnp.concatenate([jnp.concatenate([heads_along_lanes(sc[:N_HEADS * L], keep_f), z4], axis=1),
                               jnp.concatenate([z4, heads_along_lanes(sc[N_HEADS * L:], keep_b)], axis=1)], axis=0)
        pv = jnp.dot(lhs, jnp.concatenate([heads_along_rows(v_f), heads_along_rows(v_b)], axis=0),
                     preferred_element_type=F32)
        qd = jnp.concatenate([jnp.concatenate([q_f * jnp.exp(cum_f), zq], axis=1),
                              jnp.concatenate([zq, q_b * jnp.exp(cum_b)], axis=1)], axis=0)
        out = pv + _bdot_nt(qd, jnp.concatenate([sp_f, sp_b], axis=1))
        outs = (out[:L], out[L:])
    ke = jnp.concatenate([jnp.concatenate([k_f * jnp.exp(tot_f - cum_f), zq], axis=1),
                          jnp.concatenate([zq, k_b * jnp.exp(tot_b - cum_b)], axis=1)], axis=0)
    v_t = jnp.concatenate([v_f, v_b], axis=0).T
    upd = _bdot(v_t, ke)
    bd = (_iota((GROUP_W, n), 0) // HEAD_DIM) == (_iota((GROUP_W, n), 1) // GLA_DK)
    sf_ref[...] = jnp.exp(tot_f) * sp_f + jnp.where(bd, upd[:, :n], 0.0)
    sb_ref[...] = jnp.exp(tot_b) * sp_b + jnp.where(bd, upd[:, n:], 0.0)
    return outs


def _gla_kernel(qk_ref, v_ref, g_ref, sm_ref, qkc_ref, vc_ref, gc_ref, smc_ref, wlr_ref, blr_ref, ng_ref,
                *rest, ctx_out):
    if ctx_out:
        o_ref, oc_ref, la_sc, lac_sc, of_sc, ob_sc, ofc_sc, obc_sc, sf_sc, sb_sc = rest
    else:
        o_ref, la_sc, lac_sc, of_sc, ob_sc, sf_sc, sb_sc = rest
        oc_ref = ofc_sc = obc_sc = None
    t = qk_ref.shape[1]
    tc = qkc_ref.shape[1]
    qscale = GLA_DK ** -0.5

    def log_decay(sm):
        z = _dot3(sm, wlr_ref[...]) + blr_ref[...]
        return _log_sigmoid(z) * (1.0 / GLA_NORMALIZER)

    lac_sc[...] = log_decay(smc_ref[0])
    rows_blk = 256

    def la_body(i, carry):
        r0 = pl.multiple_of(i * rows_blk, rows_blk)
        la_sc[pl.ds(r0, rows_blk), :] = log_decay(sm_ref[0, pl.ds(r0, rows_blk), :])
        return carry

    lax.fori_loop(0, t // rows_blk, la_body, 0)
    sf_sc[...] = jnp.zeros_like(sf_sc)
    sb_sc[...] = jnp.zeros_like(sb_sc)

    def run(n_rows, qk, v, la, of, ob, want_out):
        n = n_rows // GLA_BLOCK

        def body(i, carry):
            rf = pl.multiple_of(i * GLA_BLOCK, GLA_BLOCK)
            rb = pl.multiple_of((n - 1 - i) * GLA_BLOCK, GLA_BLOCK)
            blks = []
            for r0, lane0 in ((rf, 0), (rb, GLA_QK_W)):
                qkb = qk[0, pl.ds(r0, GLA_BLOCK), :].astype(F32)
                blks.append((qkb[:, :GLA_QK_W] * qscale, qkb[:, GLA_QK_W:], v[0, pl.ds(r0, GLA_BLOCK), :].astype(F32),
                             la[pl.ds(r0, GLA_BLOCK), lane0:lane0 + GLA_QK_W]))
            out_f, out_b = _gla_step(blks[0], blks[1], sf_sc, sb_sc, want_out)
            if want_out:
                of[pl.ds(rf, GLA_BLOCK), :] = out_f
                ob[pl.ds(rb, GLA_BLOCK), :] = out_b
            return carry

        lax.fori_loop(0, n, body, 0)

    run(tc, qkc_ref, vc_ref, lac_sc, ofc_sc, obc_sc, ctx_out)
    run(t, qk_ref, v_ref, la_sc, of_sc, ob_sc, True)

    grp = jnp.where((_iota((GROUP_W, GROUP_W), 0) // HEAD_DIM) == (_iota((GROUP_W, GROUP_W), 1) // HEAD_DIM),
                    1.0 / HEAD_DIM, 0.0).astype(BF16)

    def finish(n_rows, of, ob, g, dst):
        blk = min(n_rows, 256)

        def body(i, carry):
            r0 = pl.multiple_of(i * blk, blk)
            o = of[pl.ds(r0, blk), :] + ob[pl.ds(r0, blk), :]
            sq_hi, sq_lo, _ = _split3(o * o)
            ms = (jnp.dot(sq_hi, grp, preferred_element_type=F32) + jnp.dot(sq_lo, grp, preferred_element_type=F32))
            res = o * lax.rsqrt(ms + EPS) * ng_ref[...] * _silu(g[0, pl.ds(r0, blk), :].astype(F32))
            dst[0, pl.ds(r0, blk), :] = res.astype(dst.dtype)
            return carry

        lax.fori_loop(0, n_rows // blk, body, 0)

    finish(t, of_sc, ob_sc, g_ref, o_ref)
    if ctx_out:
        finish(tc, ofc_sc, obc_sc, gc_ref, oc_ref)


def _gla(p, ps, pc, pcs, wlr, blr, ng, ctx_out):
    b, t, _ = p.shape
    tc = pc.shape[1]

    def col(width, idx, rows):
        return pl.BlockSpec((1, rows, width), lambda i: (i, 0, idx))

    full = lambda shape: pl.BlockSpec(shape, lambda i: tuple(0 for _ in shape))
    in_specs = [col(GROUP_W, P_GLA_QK, t), col(GROUP_W, P_GLA_V, t), col(GROUP_W, P_GLA_G, t), col(LANES, 0, t),
                col(GROUP_W, P_GLA_QK, tc), col(GROUP_W, P_GLA_V, tc), col(GROUP_W, P_GLA_G, tc),
                col(LANES, 0, tc),
                full(wlr.shape), full(blr.shape), full(ng.shape)]
    out_specs = [pl.BlockSpec((1, t, GROUP_W), lambda i: (i, 0, 0))]
    out_shape = [jax.ShapeDtypeStruct((b, t, GROUP_W), BF16)]
    scratch = [pltpu.VMEM((t, 2 * GLA_QK_W), F32), pltpu.VMEM((tc, 2 * GLA_QK_W), F32),
               pltpu.VMEM((t, GROUP_W), F32), pltpu.VMEM((t, GROUP_W), F32)]
    if ctx_out:
        out_specs.append(pl.BlockSpec((1, tc, GROUP_W), lambda i: (i, 0, 0)))
        out_shape.append(jax.ShapeDtypeStruct((b, tc, GROUP_W), BF16))
        scratch += [pltpu.VMEM((tc, GROUP_W), F32), pltpu.VMEM((tc, GROUP_W), F32)]
    scratch += [pltpu.VMEM((GROUP_W, GLA_QK_W), F32), pltpu.VMEM((GROUP_W, GLA_QK_W), F32)]
    res = pl.pallas_call(
        functools.partial(_gla_kernel, ctx_out=ctx_out),
        grid=(b,),
        in_specs=in_specs, out_specs=out_specs, out_shape=out_shape, scratch_shapes=scratch,
        compiler_params=_cparams(("parallel",)),
        name="gla",
    )(p, p, p, ps, pc, pc, pc, pcs, wlr, blr, ng)
    return (res[0], res[1]) if ctx_out else (res[0], None)


ML_BLOCK = 128
N_PAIRS = N_HEADS // 2


def _gate_lane(d, h):
    return SMALL_GATE_LANE + 8 * d + h


def _mlstm_selectors():
    sel_head = np.zeros((LANES, 2 * N_HEADS * LANES), np.float32)
    sel_pair = np.zeros((LANES, 2 * N_PAIRS * LANES), np.float32)
    for d in range(2):
        for h in range(N_HEADS):
            sel_head[_gate_lane(d, h), (d * N_HEADS + h) * LANES:(d * N_HEADS + h + 1) * LANES] = 1.0
            p, hh = divmod(h, 2)
            lo = (d * N_PAIRS + p) * LANES + hh * HEAD_DIM
            sel_pair[_gate_lane(d, h), lo:lo + HEAD_DIM] = 1.0
    return sel_head, sel_pair


def _mlstm_block(blk_f, blk_b, cn_ref, m_prev, selh_ref, selp_ref, want_out):
    L = ML_BLOCK
    lane = _iota((1, LANES), 1)
    is_b = (lane >= _gate_lane(1, 0)) & (lane < _gate_lane(1, 0) + N_HEADS)
    valid = ((lane >= _gate_lane(0, 0)) & (lane < _gate_lane(0, 0) + N_HEADS)) | is_b
    r, c = _iota((L, L), 0), _iota((L, L), 1)
    keep_f, keep_b = c <= r, c >= r
    lf_f = pltpu.roll(_log_sigmoid(blk_f[2]), LANES - 4, 1)
    lf_b = pltpu.roll(_log_sigmoid(blk_b[2]), LANES - 4, 1)
    cum_f = _tri_dot(jnp.where(keep_f, 1.0, 0.0), lf_f)
    cum_b = _tri_dot(jnp.where(keep_b, 1.0, 0.0), lf_b)
    cum = jnp.where(valid, jnp.where(is_b, cum_b, cum_f), 0.0)
    ig = jnp.where(valid, jnp.where(is_b, blk_b[2], blk_f[2]), 0.0)
    tot = jnp.where(valid, jnp.where(is_b, cum_b[0:1, :], cum_f[L - 1:L, :]), 0.0)
    b = ig - cum
    row = _iota((L, 1), 0)
    pm = b
    k = 1
    while k < L:
        from_prev = jnp.where(row >= k, pltpu.roll(pm, k, 0), -jnp.inf)
        from_next = jnp.where(row < L - k, pltpu.roll(pm, L - k, 0), -jnp.inf)
        pm = jnp.maximum(pm, jnp.where(is_b, from_next, from_prev))
        k *= 2
    mm = jnp.maximum(m_prev, pm)
    w_end = tot + b
    m_chunk = jnp.max(w_end, axis=0, keepdims=True)
    m_new = jnp.maximum(tot + m_prev, m_chunk)
    a_row = jnp.exp(tot + m_prev - m_new)
    g_row = jnp.exp(m_chunk - m_new)
    cols = [jnp.exp(w_end - m_chunk)]
    if want_out:
        cols += [jnp.exp(m_prev - mm), jnp.exp(-(cum + mm))]
        mm_hi, mm_lo, _ = _split3(mm)
        mm_t = jnp.dot(jnp.concatenate([mm_hi, mm_lo], axis=1), jnp.concatenate([selh_ref[...]] * 2, axis=0),
                       preferred_element_type=F32)
        b_rows = b.T
    pair_t = _bdot(jnp.concatenate(cols, axis=0), selp_ref[...])
    lo_half = lane < HEAD_DIM
    ones_t = jnp.ones((L, LANES), F32)
    zeros_2t = jnp.zeros((L, 2 * LANES), BF16)
    lane_h = _iota((1, GROUP_W), 1) // HEAD_DIM
    bd = (_iota((LANES, 2 * LANES), 0) // HEAD_DIM) == ((_iota((LANES, 2 * LANES), 1) % LANES) // HEAD_DIM)
    outs = {}
    for d, (blk, keep) in enumerate(((blk_f, keep_f), (blk_b, keep_b))):
        qk, v, _ = blk
        q_all, k_all = qk[:, :GROUP_W], qk[:, GROUP_W:]
        cn = [cn_ref[d * N_PAIRS + p] for p in range(N_PAIRS)]
        tcol = [(d * N_PAIRS + p) * LANES for p in range(N_PAIRS)]
        if want_out:
            qs = jnp.concatenate([jnp.where(lane_h == h, q_all, 0.0) for h in range(N_HEADS)], axis=0)
            sc = _bdot_nt(qs, k_all)
            s_parts, va_rows = [], []
            for h in range(N_HEADS):
                p, hh = divmod(h, 2)
                mine = lo_half if hh == 0 else jnp.logical_not(lo_half)
                gl = _gate_lane(d, h)
                hcol = (d * N_HEADS + h) * LANES
                e = jnp.where(keep, jnp.exp(b_rows[gl:gl + 1, :] - mm_t[:, hcol:hcol + LANES]), 0.0)
                s_parts.append((e * sc[h * L:(h + 1) * L, :]).astype(BF16))
                vp = v[:, p * LANES:(p + 1) * LANES]
                va = jnp.concatenate([jnp.where(mine, vp, 0.0), jnp.where(mine, ones_t, 0.0)], axis=1).astype(BF16)
                va_rows.append(jnp.concatenate([va, zeros_2t] if p == 0 else [zeros_2t, va], axis=1))
            nd = jnp.dot(jnp.concatenate(s_parts, axis=1), jnp.concatenate(va_rows, axis=0),
                         preferred_element_type=F32)
            cn_bd = jnp.concatenate([jnp.concatenate([cn[0].astype(BF16), zeros_2t], axis=1),
                                     jnp.concatenate([zeros_2t, cn[1].astype(BF16)], axis=1)], axis=0)
            w_int = [pair_t[L:2 * L, tc0:tc0 + LANES] for tc0 in tcol]
            nd = nd + (jnp.concatenate([w_int[0], w_int[0], w_int[1], w_int[1]], axis=1)
                       * jnp.dot(q_all.astype(BF16), cn_bd, preferred_element_type=F32))
            for p in range(N_PAIRS):
                c0 = 2 * p * LANES
                outs[(d, p)] = nd[:, c0:c0 + LANES] / jnp.maximum(jnp.abs(nd[:, c0 + LANES:c0 + 2 * LANES]),
                                                                  pair_t[2 * L:3 * L, tcol[p]:tcol[p] + LANES])
        w_k = jnp.concatenate([pair_t[0:L, tc0:tc0 + LANES] for tc0 in tcol], axis=1)
        kw_t = (k_all * w_k).T
        upd_all = _bdot(kw_t, jnp.concatenate([v, ones_t], axis=1))
        for p in range(N_PAIRS):
            rows = slice(p * LANES, (p + 1) * LANES)
            upd = jnp.concatenate([upd_all[rows, p * LANES:(p + 1) * LANES], upd_all[rows, GROUP_W:]], axis=1)
            ga, gb = _gate_lane(d, 2 * p), _gate_lane(d, 2 * p + 1)
            half = (HEAD_DIM, 2 * LANES)
            a_t = jnp.concatenate([jnp.broadcast_to(a_row[:, ga:ga + 1], half),
                                   jnp.broadcast_to(a_row[:, gb:gb + 1], half)], axis=0)
            g_t = jnp.concatenate([jnp.broadcast_to(g_row[:, ga:ga + 1], half),
                                   jnp.broadcast_to(g_row[:, gb:gb + 1], half)], axis=0)
            cn_ref[d * N_PAIRS + p] = a_t * cn[p] + g_t * jnp.where(bd, upd, 0.0)
    return (outs if want_out else None), m_new


def _mlstm_kernel(qk_ref, v_ref, og_ref, sm_ref, qkc_ref, vc_ref, ogc_ref, smc_ref, cw_ref, gb_ref,
                  selh_ref, selp_ref, *rest, ctx_out):
    if ctx_out:
        o_ref, oc_ref, qc_sc, qcc_sc, hf_sc, hb_sc, hfc_sc, hbc_sc, cn_sc = rest
    else:
        o_ref, qc_sc, qcc_sc, hf_sc, hb_sc, cn_sc = rest
        oc_ref = hfc_sc = hbc_sc = None
    t = qk_ref.shape[1]
    tc = qkc_ref.shape[1]
    w2 = 2 * GROUP_W
    kscale = HEAD_DIM ** -0.5
    n_rows_grid = t // GRID_W

    def tap(dr, dc):
        return cw_ref[dr * 3 + dc:dr * 3 + dc + 1, :]

    def shifted(blk, dc):
        n = blk.shape[0]
        if dc == 1:
            return blk
        ridx = _iota((n, 1), 0)
        if dc == 0:
            return jnp.where(ridx == 0, 0.0, pltpu.roll(blk, 1, 0))
        return jnp.where(ridx == n - 1, 0.0, pltpu.roll(blk, n - 1, 0))

    def finish_qk(acc):
        a = _silu(acc)
        lane = _iota((1, w2), 1)
        return jnp.where(lane >= GROUP_W, a * kscale, a)

    def conv_body(i, carry):
        r0 = pl.multiple_of(i * GRID_W, GRID_W)
        r_up = pl.multiple_of(jnp.maximum(i - 1, 0) * GRID_W, GRID_W)
        r_dn = pl.multiple_of(jnp.minimum(i + 1, n_rows_grid - 1) * GRID_W, GRID_W)
        up = qk_ref[0, pl.ds(r_up, GRID_W), :].astype(F32)
        mid = qk_ref[0, pl.ds(r0, GRID_W), :].astype(F32)
        dn = qk_ref[0, pl.ds(r_dn, GRID_W), :].astype(F32)
        has_up = jnp.where(i > 0, 1.0, 0.0)
        has_dn = jnp.where(i < n_rows_grid - 1, 1.0, 0.0)
        acc = None
        for dc in range(3):
            col = shifted(up * (tap(0, dc) * has_up) + mid * tap(1, dc) + dn * (tap(2, dc) * has_dn), dc)
            acc = col if acc is None else acc + col
        qc_sc[pl.ds(r0, GRID_W), :] = finish_qk(acc)
        return carry

    lax.fori_loop(0, n_rows_grid, conv_body, 0)
    xc = qkc_ref[0].astype(F32)
    acc = jnp.zeros((tc, w2), F32)
    for dc in range(3):
        acc = acc + shifted(xc, dc) * tap(1, dc)
    qcc_sc[...] = finish_qk(acc)

    cn_sc[...] = jnp.zeros_like(cn_sc)

    def run(n_rows, qc, v, sm, hf, hb, want_out, m0):
        n = n_rows // ML_BLOCK

        def body(i, m):
            rf = pl.multiple_of(i * ML_BLOCK, ML_BLOCK)
            rb = pl.multiple_of((n - 1 - i) * ML_BLOCK, ML_BLOCK)
            blks = [(qc[pl.ds(r0, ML_BLOCK), :], v[0, pl.ds(r0, ML_BLOCK), :].astype(F32),
                     sm[0, pl.ds(r0, ML_BLOCK), :] + gb_ref[...]) for r0 in (rf, rb)]
            outs, m_new = _mlstm_block(blks[0], blks[1], cn_sc, m, selh_ref, selp_ref, want_out)
            if want_out:
                for p in range(N_PAIRS):
                    hf[pl.ds(rf, ML_BLOCK), p * LANES:(p + 1) * LANES] = outs[(0, p)]
                    hb[pl.ds(rb, ML_BLOCK), p * LANES:(p + 1) * LANES] = outs[(1, p)]
            return m_new

        return lax.fori_loop(0, n, body, m0)

    m1 = run(tc, qcc_sc, vc_ref, smc_ref, hfc_sc, hbc_sc, ctx_out, jnp.zeros((1, LANES), F32))
    run(t, qc_sc, v_ref, sm_ref, hf_sc, hb_sc, True, m1)

    def finish(n_rows, hf, hb, og, dst):
        blk = min(n_rows, 256)

        def body(i, carry):
            r0 = pl.multiple_of(i * blk, blk)
            res = _sigmoid(og[0, pl.ds(r0, blk), :].astype(F32)) * (hf[pl.ds(r0, blk), :] + hb[pl.ds(r0, blk), :])
            dst[0, pl.ds(r0, blk), :] = res.astype(dst.dtype)
            return carry

        lax.fori_loop(0, n_rows // blk, body, 0)

    finish(t, hf_sc, hb_sc, og_ref, o_ref)
    if ctx_out:
        finish(tc, hfc_sc, hbc_sc, ogc_ref, oc_ref)


def _mlstm(p, ps, pc, pcs, conv_w, gate_b, ctx_out):
    b, t, _ = p.shape
    tc = pc.shape[1]
    sel_head, sel_pair = (jnp.asarray(a).astype(BF16) for a in _mlstm_selectors())

    def col(width, idx, rows):
        return pl.BlockSpec((1, rows, width), lambda i: (i, 0, idx))

    full = lambda shape: pl.BlockSpec(shape, lambda i: tuple(0 for _ in shape))
    in_specs = [col(2 * GROUP_W, P_ML_QK, t), col(GROUP_W, P_ML_V, t), col(GROUP_W, P_ML_OG, t),
                col(LANES, 0, t),
                col(2 * GROUP_W, P_ML_QK, tc), col(GROUP_W, P_ML_V, tc), col(GROUP_W, P_ML_OG, tc),
                col(LANES, 0, tc),
                full(conv_w.shape), full(gate_b.shape), full(sel_head.shape), full(sel_pair.shape)]
    out_specs = [pl.BlockSpec((1, t, GROUP_W), lambda i: (i, 0, 0))]
    out_shape = [jax.ShapeDtypeStruct((b, t, GROUP_W), BF16)]
    scratch = [pltpu.VMEM((t, 2 * GROUP_W), F32), pltpu.VMEM((tc, 2 * GROUP_W), F32),
               pltpu.VMEM((t, GROUP_W), F32), pltpu.VMEM((t, GROUP_W), F32)]
    if ctx_out:
        out_specs.append(pl.BlockSpec((1, tc, GROUP_W), lambda i: (i, 0, 0)))
        out_shape.append(jax.ShapeDtypeStruct((b, tc, GROUP_W), BF16))
        scratch += [pltpu.VMEM((tc, GROUP_W), F32), pltpu.VMEM((tc, GROUP_W), F32)]
    scratch += [pltpu.VMEM((2 * N_PAIRS, LANES, 2 * LANES), F32)]
    res = pl.pallas_call(
        functools.partial(_mlstm_kernel, ctx_out=ctx_out),
        grid=(b,),
        in_specs=in_specs, out_specs=out_specs, out_shape=out_shape, scratch_shapes=scratch,
        compiler_params=_cparams(("parallel",)),
        name="mlstm",
    )(p, p, p, ps, pc, pc, pc, pcs, conv_w, gate_b, sel_head, sel_pair)
    return (res[0], res[1]) if ctx_out else (res[0], None)


def _sgu_kernel(u_ref, v_ref, f_ref, ws_ref, bias_ref, dft_ref, o_ref, z_ref):
    t = u_ref.shape[1]
    lane_g = _iota((1, GROUP_W), 1) // HEAD_DIM

    def body(i, carry):
        r0 = pl.multiple_of(i * SGU_CHUNK, SGU_CHUNK)
        v = v_ref[0, pl.ds(r0, SGU_CHUNK), :].astype(F32)
        mu = jnp.mean(v, axis=-1, keepdims=True)
        vc = v - mu
        vn = vc * lax.rsqrt(jnp.mean(vc * vc, axis=-1, keepdims=True) + EPS)
        mixed = bias_ref[...]
        for g in range(N_HEADS):
            mixed = mixed + _bdot(ws_ref[g], jnp.where(lane_g == g, vn, 0.0))
        o_ref[0, pl.ds(r0, SGU_CHUNK), :] = (u_ref[0, pl.ds(r0, SGU_CHUNK), :].astype(F32) * mixed).astype(o_ref.dtype)
        z = _bdot(f_ref[0, pl.ds(r0, SGU_CHUNK), :], dft_ref[...])
        z_ref[0, 0, pl.ds(r0, SGU_CHUNK), :] = z[:, :GROUP_W].astype(z_ref.dtype)
        z_ref[1, 0, pl.ds(r0, SGU_CHUNK), :] = z[:, GROUP_W:].astype(z_ref.dtype)
        return carry

    lax.fori_loop(0, t // SGU_CHUNK, body, 0)


def _sgu_fourier1(p, ws, bias, dft_c):
    b, t, _ = p.shape

    def col(idx):
        return pl.BlockSpec((1, t, GROUP_W), lambda i: (i, 0, idx))

    full = lambda shape: pl.BlockSpec(shape, lambda i: tuple(0 for _ in shape))
    return pl.pallas_call(
        _sgu_kernel,
        grid=(b,),
        in_specs=[col(P_SGU_U), col(P_SGU_V), col(P_FOUR), full(ws.shape), full(bias.shape), full(dft_c.shape)],
        out_specs=[pl.BlockSpec((1, t, GROUP_W), lambda i: (i, 0, 0)),
                   pl.BlockSpec((2, 1, t, GROUP_W), lambda i: (0, i, 0, 0))],
        out_shape=[jax.ShapeDtypeStruct((b, t, GROUP_W), BF16), jax.ShapeDtypeStruct((2, b, t, GROUP_W), BF16)],
        compiler_params=_cparams(("parallel",)),
        name="sgu_fourier1",
    )(p, p, p, ws, bias, dft_c)


def _fourier2_kernel(ct_ref, st_ref, z_ref, o_ref, *, scale):
    for i in range(z_ref.shape[1]):
        acc = jnp.dot(ct_ref[...], z_ref[0, i], preferred_element_type=F32)
        acc -= jnp.dot(st_ref[...], z_ref[1, i], preferred_element_type=F32)
        o_ref[i] = (acc * scale).astype(o_ref.dtype)


def _fourier2(z, cos_t, sin_t):
    _, b, t, w = z.shape
    tm = min(t, 256)
    return pl.pallas_call(
        functools.partial(_fourier2_kernel, scale=float((t * HEAD_DIM) ** -0.5)),
        grid=(t // tm,),
        in_specs=[pl.BlockSpec((tm, t), lambda i: (i, 0)),
                  pl.BlockSpec((tm, t), lambda i: (i, 0)),
                  pl.BlockSpec((2, b, t, w), lambda i: (0, 0, 0, 0), pipeline_mode=pl.Buffered(1))],
        out_specs=pl.BlockSpec((b, tm, w), lambda i: (0, i, 0)),
        out_shape=jax.ShapeDtypeStruct((b, t, w), BF16),
        compiler_params=_cparams(("parallel",)),
        name="fourier2",
    )(cos_t, sin_t, z)


def _dft_tables(n):
    k = np.arange(n, dtype=np.int64)
    ang = 2.0 * np.pi * ((k[:, None] * k[None, :]) % n).astype(np.float64) / n
    return np.cos(ang).astype(np.float32), np.sin(ang).astype(np.float32)


def _channel_dft():
    c, s = _dft_tables(HEAD_DIM)
    eye = np.eye(N_HEADS, dtype=np.float32)
    return np.concatenate([np.kron(eye, c), np.kron(eye, s)], axis=1)


def _route_kernel(x_ref, mod_ref, g_ref, wr_ref, h_ref, mi_ref, mf_ref, cnt_ref, cnt_sc):
    first = (pl.program_id(0) == 0) & (pl.program_id(1) == 0)

    @pl.when(first)
    def _():
        cnt_sc[...] = jnp.zeros_like(cnt_sc)

    tm = x_ref.shape[1]
    h = _norm_mod(x_ref[0], g_ref[...], mod_ref[0, 3:4, :], mod_ref[0, 4:5, :])
    for j in range(D_MODEL // LANES):
        h_ref[pl.ds(j, tm, stride=SUBLANES), :] = h[:, j * LANES:(j + 1) * LANES]
    logits = _fdot(h, wr_ref[...])
    lane = _iota((tm, LANES), 1).astype(F32)
    lg = jnp.where(lane < N_EXPERTS, logits, -jnp.inf)
    m1 = jnp.max(lg, axis=1, keepdims=True)
    i1 = jnp.min(jnp.where(lg == m1, lane, float(LANES)), axis=1, keepdims=True)
    lg2 = jnp.where(lane == i1, -jnp.inf, lg)
    m2 = jnp.max(lg2, axis=1, keepdims=True)
    i2 = jnp.min(jnp.where(lg2 == m2, lane, float(LANES)), axis=1, keepdims=True)
    e = jnp.exp(m2 - m1)
    g0 = 1.0 / (1.0 + e)
    g1 = e / (1.0 + e)
    onehot = jnp.where((lane == i1) | (lane == i2), 1.0, 0.0)
    strict = jnp.where(_iota((tm, tm), 1) < _iota((tm, tm), 0), 1.0, 0.0)
    before = _bdot(strict, onehot) + cnt_sc[...]
    r0 = jnp.sum(jnp.where(lane == i1, before, 0.0), axis=1, keepdims=True)
    r1 = jnp.sum(jnp.where(lane == i2, before, 0.0), axis=1, keepdims=True)
    cnt_sc[...] += jnp.sum(onehot, axis=0, keepdims=True)
    cnt_ref[...] = cnt_sc[...]
    l8 = _iota((tm, SUBLANES), 1)
    mi = jnp.where(l8 == 0, i1, jnp.where(l8 == 1, i2, jnp.where(l8 == 2, r0, r1)))
    mi_ref[...] = mi.astype(jnp.int32)
    mf_ref[...] = jnp.where(l8 == 0, g0, g1)


def _route(x, mod, g, w_router):
    b, t, d = x.shape
    m = b * t
    tm = 512
    nt = t // tm
    wr = jnp.zeros((d, LANES), F32).at[:, :N_EXPERTS].set(w_router)
    return pl.pallas_call(
        _route_kernel,
        grid=(b, nt),
        in_specs=[pl.BlockSpec((1, tm, d), lambda i, j: (i, j, 0)),
                  pl.BlockSpec((1, 6, d), lambda i, j: (i, 0, 0)),
                  pl.BlockSpec((1, d), lambda i, j: (0, 0)),
                  pl.BlockSpec((d, LANES), lambda i, j: (0, 0))],
        out_specs=[pl.BlockSpec((tm * SUBLANES, LANES), lambda i, j: (i * nt + j, 0)),
                   pl.BlockSpec((tm, SUBLANES), lambda i, j: (i * nt + j, 0)),
                   pl.BlockSpec((tm, SUBLANES), lambda i, j: (i * nt + j, 0)),
                   pl.BlockSpec((1, LANES), lambda i, j: (0, 0))],
        out_shape=[jax.ShapeDtypeStruct((m * SUBLANES, LANES), F32),
                   jax.ShapeDtypeStruct((m, SUBLANES), jnp.int32),
                   jax.ShapeDtypeStruct((m, SUBLANES), F32),
                   jax.ShapeDtypeStruct((1, LANES), F32)],
        scratch_shapes=[pltpu.VMEM((1, LANES), F32)],
        compiler_params=_cparams(("arbitrary", "arbitrary")),
        name="route",
    )(x, mod, g.reshape(1, d), wr)


def _moe_kernel(texp_ref, trows_ref, pos0_ref, pos1_ref, h_hbm, wg_ref, wu_ref, wd_ref, y_hbm,
                order_sm, xbuf, x_sc, acc_sc, ybuf, gsem, ssem):
    i = pl.program_id(0)
    j = pl.program_id(1)
    nf = pl.num_programs(1)
    tm = MOE_TM
    n_tok = pos0_ref.shape[0]
    rows = trows_ref[i]

    nt = pl.num_programs(0)
    buf_rows = tm * SUBLANES
    dump = TOP_K * n_tok
    slot = lax.rem(i, 2)
    prev_rows = trows_ref[jnp.maximum(i - 1, 0)]
    next_rows = trows_ref[jnp.minimum(i + 1, nt - 1)]
    n_full, n_rest = divmod(tm, MOE_COPY_ROWS)

    def gather_copy(tile, n_real, r, dst_slot):
        idx = jnp.where(n_real > 0, tile * tm + jnp.minimum(r, n_real - 1), 0)
        tok = jnp.right_shift(order_sm[idx], 1)
        off = pl.multiple_of(dst_slot * buf_rows + r * SUBLANES, SUBLANES)
        return pltpu.make_async_copy(h_hbm.at[tok], xbuf.at[pl.ds(off, SUBLANES), :], gsem)

    def scatter_copy(tile, n_real, r, src_slot):
        idx = jnp.where(n_real > 0, tile * tm + jnp.minimum(r, n_real - 1), 0)
        dst = jnp.where(r < n_real, order_sm[idx], dump + r)
        off = pl.multiple_of(src_slot * buf_rows + r * SUBLANES, SUBLANES)
        return pltpu.make_async_copy(ybuf.at[pl.ds(off, SUBLANES), :], y_hbm.at[dst], ssem)

    def wait_all(buf, sem):
        pltpu.make_async_copy(buf.at[pl.ds(0, buf_rows), :], buf.at[pl.ds(buf_rows, buf_rows), :], sem).wait()

    @pl.when((i == 0) & (j == 0))
    def _():
        def inv_body(t, carry):
            order_sm[pos0_ref[t]] = 2 * t
            order_sm[pos1_ref[t]] = 2 * t + 1
            return carry

        lax.fori_loop(0, n_tok, inv_body, 0, unroll=8)
        ybuf[...] = jnp.zeros_like(ybuf)

        def start(r, carry):
            gather_copy(0, rows, r, 0).start()
            return carry

        lax.fori_loop(0, tm, start, 0)

    @pl.when((j == 0) & (rows > 0))
    def _():
        wait_all(xbuf, gsem)
        base = pl.multiple_of(slot * buf_rows, SUBLANES)
        for c in range(D_MODEL // LANES):
            x_sc[:, c * LANES:(c + 1) * LANES] = xbuf[pl.ds(base + c, tm, stride=SUBLANES), :].astype(BF16)
        acc_sc[...] = jnp.zeros_like(acc_sc)

    def step(n_copies):
        x = x_sc[...]
        gate = jnp.dot(x, wg_ref[0].astype(BF16), preferred_element_type=F32)
        up = jnp.dot(x, wu_ref[0].astype(BF16), preferred_element_type=F32)
        acc_sc[...] += jnp.dot((_silu(gate) * up).astype(BF16), wd_ref[0].astype(BF16),
                               preferred_element_type=F32)
        r0 = j * MOE_COPY_ROWS
        for q in range(n_copies):
            gather_copy(i + 1, next_rows, r0 + q, 1 - slot).start()
            scatter_copy(i - 1, jnp.where(i > 0, prev_rows, 0), r0 + q, 1 - slot).start()

    pl.when((rows > 0) & (j < n_full))(functools.partial(step, MOE_COPY_ROWS))
    pl.when((rows > 0) & (j == n_full))(functools.partial(step, n_rest))
    pl.when((rows > 0) & (j > n_full))(functools.partial(step, 0))

    @pl.when((j == nf - 1) & (rows > 0))
    def _():
        wait_all(ybuf, ssem)
        base = pl.multiple_of(slot * buf_rows, SUBLANES)
        for c in range(D_MODEL // LANES):
            ybuf[pl.ds(base + c, tm, stride=SUBLANES), :] = acc_sc[:, c * LANES:(c + 1) * LANES]

    @pl.when((j == 0) & (rows == 0) & (i > 0) & (prev_rows > 0))
    def _():
        wait_all(xbuf, gsem)

        def start(r, carry):
            scatter_copy(i - 1, prev_rows, r, 1 - slot).start()
            return carry

        lax.fori_loop(0, tm, start, 0)
        wait_all(ybuf, ssem)


def _moe(h_tiles, pos0, pos1, tile_expert, tile_rows, wg, wu, wd):
    m = h_tiles.shape[0]
    n_exp, d, f = wg.shape
    nt = tile_expert.shape[0]
    tm, tf = MOE_TM, MOE_TF
    nf = f // tf
    assert nf * MOE_COPY_ROWS >= tm, "each tile needs enough column steps to issue its neighbours' row copies"
    assert nt * tm > TOP_K * m + n_exp * (tm - 1), "the last tile must be unused: it drains the copy pipeline"

    def col_tile(i, j, tr):
        return jnp.where(tr[i] > 0, j, nf - 1)

    grid_spec = pltpu.PrefetchScalarGridSpec(
        num_scalar_prefetch=4,
        grid=(nt, nf),
        in_specs=[pl.BlockSpec(memory_space=pl.ANY),
                  pl.BlockSpec((1, d, tf), lambda i, j, te, tr, p0, p1: (te[i], 0, col_tile(i, j, tr))),
                  pl.BlockSpec((1, d, tf), lambda i, j, te, tr, p0, p1: (te[i], 0, col_tile(i, j, tr))),
                  pl.BlockSpec((1, tf, d), lambda i, j, te, tr, p0, p1: (te[i], col_tile(i, j, tr), 0))],
        out_specs=pl.BlockSpec(memory_space=pl.ANY),
        scratch_shapes=[pltpu.SMEM((nt * tm,), jnp.int32),
                        pltpu.VMEM((2 * tm * SUBLANES, LANES), F32),
                        pltpu.VMEM((tm, d), BF16),
                        pltpu.VMEM((tm, d), F32),
                        pltpu.VMEM((2 * tm * SUBLANES, LANES), F32),
                        pltpu.SemaphoreType.DMA(()),
                        pltpu.SemaphoreType.DMA(())],
    )
    return pl.pallas_call(
        _moe_kernel,
        grid_spec=grid_spec,
        out_shape=jax.ShapeDtypeStruct((TOP_K * m + tm, SUBLANES, LANES), F32),
        compiler_params=_cparams(("arbitrary", "arbitrary")),
        name="moe",
    )(tile_expert, tile_rows, pos0, pos1, h_tiles, wg, wu, wd)


def _combine_kernel(x_ref, mod_ref, y_ref, mf_ref, fg_ref, o_ref):
    tm = x_ref.shape[1]
    stride = TOP_K * SUBLANES
    g0 = mf_ref[:, 0:1]
    g1 = mf_ref[:, 1:2]
    parts = []
    for c in range(D_MODEL // LANES):
        y0 = y_ref[pl.ds(c, tm, stride=stride), :]
        y1 = y_ref[pl.ds(SUBLANES + c, tm, stride=stride), :]
        parts.append(g0 * y0 + g1 * y1)
    moe = jnp.concatenate(parts, axis=1)
    x = x_ref[0] + mod_ref[0, 5:6, :] * moe
    ms = jnp.mean(x * x, axis=-1, keepdims=True)
    o_ref[0] = x * lax.rsqrt(ms + EPS) * fg_ref[...]


def _combine(x, mod, y, mf, final_g):
    b, t, d = x.shape
    tm = 256
    nt = t // tm
    rows = tm * TOP_K * SUBLANES
    return pl.pallas_call(
        _combine_kernel,
        grid=(b, nt),
        in_specs=[pl.BlockSpec((1, tm, d), lambda i, j: (i, j, 0)),
                  pl.BlockSpec((1, 6, d), lambda i, j: (i, 0, 0)),
                  pl.BlockSpec((rows, LANES), lambda i, j: (i * nt + j, 0)),
                  pl.BlockSpec((tm, SUBLANES), lambda i, j: (i * nt + j, 0)),
                  pl.BlockSpec((1, d), lambda i, j: (0, 0))],
        out_specs=pl.BlockSpec((1, tm, d), lambda i, j: (i, j, 0)),
        out_shape=jax.ShapeDtypeStruct((b, t, d), F32),
        compiler_params=_cparams(("parallel", "parallel")),
        name="combine",
    )(x, mod, y, mf, final_g.reshape(1, d))


def _moe_layer(x, mod, norm_g, w_router, wg, wu, wd, final_g):
    b, t, d = x.shape
    m = b * t
    h_tiles, mi, mf, cnt = _route(x, mod, norm_g, w_router)
    counts = cnt[0, :N_EXPERTS].astype(jnp.int32)
    tiles_per = (counts + MOE_TM - 1) // MOE_TM
    tile_end = jnp.cumsum(tiles_per)
    tile_start = tile_end - tiles_per
    seg_start = tile_start * MOE_TM
    nt = (TOP_K * m) // MOE_TM + N_EXPERTS
    tile_id = jnp.arange(nt, dtype=jnp.int32)
    used_id = jnp.minimum(tile_id, tile_end[-1] - 1)
    tile_expert = jnp.sum(used_id[:, None] >= tile_end[None, :], axis=1).astype(jnp.int32)
    tile_rows = jnp.clip(counts[tile_expert] - (tile_id - tile_start[tile_expert]) * MOE_TM, 0, MOE_TM)
    tile_rows = jnp.where(tile_id < tile_end[-1], tile_rows, 0).astype(jnp.int32)
    pos0 = seg_start[mi[:, 0]] + mi[:, 2]
    pos1 = seg_start[mi[:, 1]] + mi[:, 3]
    y = _moe(h_tiles.reshape(m, SUBLANES, LANES), pos0, pos1, tile_expert, tile_rows, wg, wu, wd)
    return _combine(x, mod, y.reshape(-1, LANES), mf, final_g)


def _permute_w_in(w):
    d = w.shape[0]
    return jnp.concatenate([w[:, 0:768], w[:, 1312:1568], w[:, 800:1312], w[:, 1568:1824], w[:, 1840:2608],
                            w[:, 768:800], w[:, 1824:1840], jnp.zeros((d, P_WIDTH - 2608), w.dtype)],
                           axis=1).astype(BF16)


def _mixers(proj, proj_c, gla_w_lr, gla_b_lr, gla_norm_g, mlstm_conv, mlstm_gate_b, sgu_w, sgu_b, ctx_out, tables):
    (p, ps), (pc, pcs) = proj, proj_c
    wlr = jnp.zeros((LANES, 2 * GLA_QK_W), F32)
    wlr = wlr.at[0:GLA_RANK, 0:GLA_QK_W].set(gla_w_lr[0]).at[GLA_RANK:2 * GLA_RANK, GLA_QK_W:].set(gla_w_lr[1])
    blr = gla_b_lr.reshape(1, 2 * GLA_QK_W)
    gla, gla_c = _gla(p, ps, pc, pcs, wlr, blr, gla_norm_g.reshape(1, GROUP_W), ctx_out)
    conv_w = mlstm_conv.reshape(9, 2 * GROUP_W)
    gate_b = jnp.zeros((1, LANES), F32).at[0, SMALL_GATE_LANE:SMALL_GATE_LANE + 4 * N_HEADS].set(
        mlstm_gate_b.reshape(-1))
    ml, ml_c = _mlstm(p, ps, pc, pcs, conv_w, gate_b, ctx_out)
    bias = jnp.repeat(sgu_b.T, HEAD_DIM, axis=1)
    dft_c, tabs = tables
    sg, z = _sgu_fourier1(p, sgu_w, bias, dft_c)
    fo = _fourier2(z, *tabs[p.shape[1]])
    mix = (gla, ml, sg, fo)
    if not ctx_out:
        return mix, None
    sg_c, z_c = _sgu_fourier1(pc, sgu_w, bias, dft_c)
    fo_c = _fourier2(z_c, *tabs[pc.shape[1]])
    return mix, (gla_c, ml_c, sg_c, fo_c)


def kernel(x, c, ctx, c_ctx, w_ada, b_ada, norm_mix_g, norm_ffn_g, w_in, w_out, gla_w_lr, gla_b_lr, gla_norm_g,
           mlstm_conv, mlstm_gate_b, sgu_w, sgu_b, ffn_w_gate, ffn_w_up, ffn_w_down, moe_router, moe_w_gate,
           moe_w_up, moe_w_down, final_norm_g):
    depth = w_ada.shape[0]
    assert depth == 2, "layer 0 is the dense layer with context output, layer 1 the MoE layer"
    b, t, d = x.shape
    tc = ctx.shape[1]
    cvec = jnp.zeros((16, d), F32).at[:b].set(c).at[b].set(c_ctx)
    mods = _ada(cvec, w_ada, b_ada).reshape(depth, 16, 6, d)
    tables = (jnp.asarray(_channel_dft()).astype(BF16),
              {n: tuple(jnp.asarray(a).astype(BF16) for a in _dft_tables(n)) for n in {t, tc}})
    xc = ctx
    for l in range(depth):
        ctx_out = l < depth - 1
        mod, mod_c = mods[l, :b], mods[l, b:b + 1]
        w_in_l = _permute_w_in(w_in[l])
        w_out_l = w_out[l].astype(BF16)
        p = _in_proj(x, mod, norm_mix_g[l], w_in_l)
        pc = _in_proj(xc, mod_c, norm_mix_g[l], w_in_l)
        mix, mix_c = _mixers(p, pc, gla_w_lr[l], gla_b_lr[l], gla_norm_g[l], mlstm_conv[l], mlstm_gate_b[l],
                             sgu_w[l], sgu_b[l], ctx_out, tables)
        x = _out_proj(x, mod, mix, w_out_l)
        if ctx_out:
            xc = _out_proj(xc, mod_c, mix_c, w_out_l)
        i = l // 2
        if l % 2 == 0:
            wg, wu, wd = ffn_w_gate[i].astype(BF16), ffn_w_up[i].astype(BF16), ffn_w_down[i].astype(BF16)
            x = _ffn(x, mod, norm_ffn_g[l], wg, wu, wd)
            if ctx_out:
                xc = _ffn(xc, mod_c, norm_ffn_g[l], wg, wu, wd)
        else:
            x = _moe_layer(x, mod, norm_ffn_g[l], moe_router[i], moe_w_gate[i], moe_w_up[i], moe_w_down[i],
                           final_norm_g)
    return x
```
